```python
import math
import jax, jax.numpy as jnp
from jax import lax
import numpy as np

D_MODEL = 1024
BATCH = 4
SEQ = 8192
DEPTH = 4
DEC_BATCH = 8
DEC_SEQ = 32
PAST_LEN = 2048

CHUNK = 64
N_A = DEPTH // 2
N_B = DEPTH - N_A
ALPHA = (2.0 * DEPTH) ** 0.25
BETA = (8.0 * DEPTH) ** -0.25
LN_EPS = 1e-5
RMS_EPS = 1e-6

GM_CHUNK = 128
GM_HALF = 2 * D_MODEL
GM_GROUPS = 8
GM_GROUP_DIM = GM_HALF // GM_GROUPS

MLA_HEADS = 8
QK_NOPE = 128
QK_ROPE = 64
V_HEAD = 128
KV_LORA = D_MODEL // 4
Q_LORA = 3 * D_MODEL // 8
ROPE_BASE = 10000.0
Q_BLOCK = 128
ATTN_SCALE = (QK_NOPE + QK_ROPE) ** -0.5

PEER_HEADS = 8
PEER_NKEYS = 128
PEER_EXPERTS = PEER_NKEYS * PEER_NKEYS
PEER_DK = 256
PEER_TOPK = 16
PEER_BLOCK = 256

kernel_name = "yoco_gmlp_mla_peer_stream_step"


def layer_norm(x, g, b):
    xf = x.astype(jnp.float32)
    mu = jnp.mean(xf, axis=-1, keepdims=True)
    var = jnp.mean(jnp.square(xf - mu), axis=-1, keepdims=True)
    return ((xf - mu) * lax.rsqrt(var + LN_EPS)).astype(x.dtype) * g + b


def rms_norm(x, g):
    xf = x.astype(jnp.float32)
    return (xf * lax.rsqrt(jnp.mean(xf * xf, axis=-1, keepdims=True) + RMS_EPS)).astype(x.dtype) * g


def rope_tables(pos, dtype):
    inv = 1.0 / (ROPE_BASE ** (jnp.arange(0, QK_ROPE, 2, dtype=jnp.float32) / QK_ROPE))
    ang = pos.astype(jnp.float32)[:, None] * inv[None, :]
    return jnp.cos(ang).astype(dtype), jnp.sin(ang).astype(dtype)


def apply_rope(x, cos, sin):
    x1, x2 = jnp.split(x, 2, axis=-1)
    return jnp.concatenate([x1 * cos - x2 * sin, x1 * sin + x2 * cos], axis=-1)


def gm_mask():
    i = jnp.arange(GM_CHUNK)
    return (i[None, :] // CHUNK) <= (i[:, None] // CHUNK)


def gmlp_mixer(x, w_in, b_in, ln_g, ln_b, w_s, b_s, w_out):
    B, L, _ = x.shape
    z = jax.nn.gelu(x @ w_in + b_in, approximate=False)
    u, v = z[..., :GM_HALF], z[..., GM_HALF:]
    v = layer_norm(v, ln_g, ln_b)
    n = min(L, GM_CHUNK)
    w = jnp.where(gm_mask()[None], w_s, jnp.zeros_like(w_s))[:, :n, :n]
    vr = v.reshape(B, L // n, n, GM_GROUPS, GM_GROUP_DIM)
    sv = jnp.einsum('gij,bnjgc->bnigc', w, vr) + jnp.transpose(b_s[:, :n])[None, None, :, :, None]
    s = u * sv.reshape(B, L, GM_HALF)
    return s @ w_out, v


def mla_shared_kv(h, pos, w_dkv, kv_g):
    kv = h @ w_dkv
    c = rms_norm(kv[..., :KV_LORA], kv_g)
    cos, sin = rope_tables(pos, h.dtype)
    kpe = apply_rope(kv[..., KV_LORA:], cos[None], sin[None])
    return c, kpe


def mla_expand(c, w_ukv):
    B, K, _ = c.shape
    kv = (c @ w_ukv).reshape(B, K, MLA_HEADS, QK_NOPE + V_HEAD)
    return kv[..., :QK_NOPE], kv[..., QK_NOPE:]


def attend(qn, qp, q_pos, kn, kp, v, k_pos):
    s = (jnp.einsum('bqhd,bkhd->bhqk', qn, kn) + jnp.einsum('bqhr,bkr->bhqk', qp, kp)).astype(jnp.float32) * ATTN_SCALE
    mask = (k_pos[None, :] // CHUNK) <= (q_pos[:, None] // CHUNK)
    s = jnp.where(mask[None, None], s, -jnp.inf)
    p = jax.nn.softmax(s, axis=-1).astype(v.dtype)
    return jnp.einsum('bhqk,bkhd->bqhd', p, v)


def mla_layer(x, q_pos, k_nope, k_pe, v, k_pos, w_dq, q_g, w_uq, w_o):
    B, L, _ = x.shape
    cq = rms_norm(x @ w_dq, q_g)
    q = (cq @ w_uq).reshape(B, L, MLA_HEADS, QK_NOPE + QK_ROPE)
    cos, sin = rope_tables(q_pos, x.dtype)
    q_nope = q[..., :QK_NOPE]
    q_pe = apply_rope(q[..., QK_NOPE:], cos[None, :, None], sin[None, :, None])
    qb = min(L, Q_BLOCK)
    nb = L // qb

    def blocks(t):
        return jnp.moveaxis(t.reshape((B, nb, qb) + t.shape[2:]), 1, 0)

    def one_block(args):
        qn, qp, qpos = args
        return attend(qn, qp, qpos, k_nope, k_pe, v, k_pos)

    o = lax.map(one_block, (blocks(q_nope), blocks(q_pe), q_pos.reshape(nb, qb)))
    o = jnp.moveaxis(o, 0, 1).reshape(B, L, MLA_HEADS * V_HEAD)
    return o @ w_o


def peer(x, w_q, subkeys, u_tab, v_tab):
    shp = x.shape
    xt = x.reshape(-1, D_MODEL)
    T = xt.shape[0]
    nb = -(-T // PEER_BLOCK)
    xt = jnp.pad(xt, ((0, nb * PEER_BLOCK - T), (0, 0))).reshape(nb, PEER_BLOCK, D_MODEL)
    half = PEER_DK // 2

    def one_block(xb):
        q = (xb @ w_q).reshape(PEER_BLOCK, PEER_HEADS, PEER_DK)
        s1 = jnp.einsum('thd,nd->thn', q[..., :half], subkeys[0])
        s2 = jnp.einsum('thd,nd->thn', q[..., half:], subkeys[1])
        v1, i1 = lax.top_k(s1, PEER_TOPK)
        v2, i2 = lax.top_k(s2, PEER_TOPK)
        cand = (v1[..., :, None] + v2[..., None, :]).reshape(PEER_BLOCK, PEER_HEADS, PEER_TOPK * PEER_TOPK)
        cid = (i1[..., :, None] * PEER_NKEYS + i2[..., None, :]).reshape(PEER_BLOCK, PEER_HEADS, PEER_TOPK * PEER_TOPK)
        sc, sel = lax.top_k(cand, PEER_TOPK)
        eid = jnp.take_along_axis(cid, sel, axis=-1)
        g = jax.nn.softmax(sc.astype(jnp.float32), axis=-1).astype(xb.dtype)
        u = jnp.take(u_tab, eid, axis=0)
        hdn = jnp.einsum('td,thkd->thk', xb, u)
        a = g * jax.nn.gelu(hdn, approximate=False)
        return jnp.einsum('thk,thkd->td', a, jnp.take(v_tab, eid, axis=0))

    out = lax.map(one_block, xt).reshape(nb * PEER_BLOCK, D_MODEL)[:T]
    return out.reshape(shp)


def trunk(x, pos, past_c, past_kpe, p):
    gm_rows = []
    shared = None
    c_new = kpe_new = None
    for l in range(DEPTH):
        if l < N_A:
            mix, v_rows = gmlp_mixer(x, p['gm_w_in'][l], p['gm_b_in'][l], p['gm_ln_g'][l], p['gm_ln_b'][l],
                                     p['gm_w_s'][l], p['gm_b_s'][l], p['gm_w_out'][l])
            gm_rows.append(v_rows)
        else:
            if shared is None:
                c_new, kpe_new = mla_shared_kv(x, pos, p['mla_w_dkv'], p['mla_kv_norm_g'])
                if past_c is None:
                    c_all, kpe_all, k_pos = c_new, kpe_new, pos
                else:
                    c_all = jnp.concatenate([past_c, c_new], axis=1)
                    kpe_all = jnp.concatenate([past_kpe, kpe_new], axis=1)
                    k_pos = jnp.concatenate([jnp.arange(past_c.shape[1], dtype=pos.dtype), pos])
                k_nope, v_all = mla_expand(c_all, p['mla_w_ukv'])
                shared = (k_nope, kpe_all, v_all, k_pos)
            j = l - N_A
            mix = mla_layer(x, pos, shared[0], shared[1], shared[2], shared[3],
                            p['mla_w_dq'][j], p['mla_q_norm_g'][j], p['mla_w_uq'][j], p['mla_w_o'][j])
        x = layer_norm(ALPHA * x + mix, p['ln1_g'][l], p['ln1_b'][l])
        x = layer_norm(ALPHA * x + peer(x, p['peer_w_q'][l], p['peer_subkeys'][l], p['peer_u'][l], p['peer_v'][l]),
                       p['ln2_g'][l], p['ln2_b'][l])
    return x, gm_rows, c_new, kpe_new


def setup_inputs(seed: int = 0) -> dict:
    key = jax.random.key(seed)
    ks = jax.random.split(key, 32)
    f32 = jnp.float32

    def nrm(k, shape, scale):
        return jax.random.normal(k, shape, f32) * scale

    w_uk = nrm(ks[17], (KV_LORA, MLA_HEADS, QK_NOPE), KV_LORA ** -0.5)
    w_uv = nrm(ks[18], (KV_LORA, MLA_HEADS, V_HEAD), BETA * KV_LORA ** -0.5)
    return {
        'x_prompt': nrm(ks[0], (BATCH, SEQ, D_MODEL), 1.0),
        'x_sample': nrm(ks[1], (DEC_BATCH, DEC_SEQ, D_MODEL), 1.0),
        'cache_ckv': nrm(ks[2], (DEC_BATCH, PAST_LEN, KV_LORA), 1.0),
        'cache_kpe': nrm(ks[3], (DEC_BATCH, PAST_LEN, QK_ROPE), 1.0),
        'ln1_g': 1.0 + nrm(ks[4], (DEPTH, D_MODEL), 0.02),
        'ln1_b': nrm(ks[5], (DEPTH, D_MODEL), 0.02),
        'ln2_g': 1.0 + nrm(ks[6], (DEPTH, D_MODEL), 0.02),
        'ln2_b': nrm(ks[7], (DEPTH, D_MODEL), 0.02),
        'gm_w_in': nrm(ks[8], (N_A, D_MODEL, 2 * GM_HALF), D_MODEL ** -0.5),
        'gm_b_in': nrm(ks[9], (N_A, 2 * GM_HALF), 0.02),
        'gm_ln_g': 1.0 + nrm(ks[10], (N_A, GM_HALF), 0.02),
        'gm_ln_b': nrm(ks[11], (N_A, GM_HALF), 0.02),
        'gm_w_s': nrm(ks[12], (N_A, GM_GROUPS, GM_CHUNK, GM_CHUNK), GM_CHUNK ** -0.5),
        'gm_b_s': 1.0 + nrm(ks[13], (N_A, GM_GROUPS, GM_CHUNK), 0.1),
        'gm_w_out': nrm(ks[14], (N_A, GM_HALF, D_MODEL), BETA * GM_HALF ** -0.5),
        'mla_w_dkv': nrm(ks[15], (D_MODEL, KV_LORA + QK_ROPE), D_MODEL ** -0.5),
        'mla_kv_norm_g': 1.0 + nrm(ks[16], (KV_LORA,), 0.02),
        'mla_w_ukv': jnp.concatenate([w_uk, w_uv], axis=-1).reshape(KV_LORA, MLA_HEADS * (QK_NOPE + V_HEAD)),
        'mla_w_dq': nrm(ks[19], (N_B, D_MODEL, Q_LORA), D_MODEL ** -0.5),
        'mla_q_norm_g': 1.0 + nrm(ks[20], (N_B, Q_LORA), 0.02),
        'mla_w_uq': nrm(ks[21], (N_B, Q_LORA, MLA_HEADS * (QK_NOPE + QK_ROPE)), Q_LORA ** -0.5),
        'mla_w_o': nrm(ks[22], (N_B, MLA_HEADS * V_HEAD, D_MODEL), BETA * (MLA_HEADS * V_HEAD) ** -0.5),
        'peer_w_q': nrm(ks[23], (DEPTH, D_MODEL, PEER_HEADS * PEER_DK), D_MODEL ** -0.5),
        'peer_subkeys': nrm(ks[24], (DEPTH, 2, PEER_NKEYS, PEER_DK // 2), (PEER_DK // 2) ** -0.5),
        'peer_u': nrm(ks[25], (DEPTH, PEER_EXPERTS, D_MODEL), D_MODEL ** -0.5),
        'peer_v': nrm(ks[26], (DEPTH, PEER_EXPERTS, D_MODEL), BETA * PEER_HEADS ** -0.5),
    }


def reference(x_prompt, x_sample, cache_ckv, cache_kpe, ln1_g, ln1_b, ln2_g, ln2_b,
              gm_w_in, gm_b_in, gm_ln_g, gm_ln_b, gm_w_s, gm_b_s, gm_w_out,
              mla_w_dkv, mla_kv_norm_g, mla_w_ukv, mla_w_dq, mla_q_norm_g, mla_w_uq, mla_w_o,
              peer_w_q, peer_subkeys, peer_u, peer_v):
    p = dict(ln1_g=ln1_g, ln1_b=ln1_b, ln2_g=ln2_g, ln2_b=ln2_b,
             gm_w_in=gm_w_in, gm_b_in=gm_b_in, gm_ln_g=gm_ln_g, gm_ln_b=gm_ln_b,
             gm_w_s=gm_w_s, gm_b_s=gm_b_s, gm_w_out=gm_w_out,
             mla_w_dkv=mla_w_dkv, mla_kv_norm_g=mla_kv_norm_g, mla_w_ukv=mla_w_ukv,
             mla_w_dq=mla_w_dq, mla_q_norm_g=mla_q_norm_g, mla_w_uq=mla_w_uq, mla_w_o=mla_w_o,
             peer_w_q=peer_w_q, peer_subkeys=peer_subkeys, peer_u=peer_u, peer_v=peer_v)
    pos_prompt = jnp.arange(x_prompt.shape[1], dtype=jnp.int32)
    y_prompt, _, new_ckv_prompt, new_kpe_prompt = trunk(x_prompt, pos_prompt, None, None, p)
    pos_sample = cache_ckv.shape[1] + jnp.arange(x_sample.shape[1], dtype=jnp.int32)
    y_sample, gm_rows, new_ckv_sample, new_kpe_sample = trunk(x_sample, pos_sample, cache_ckv, cache_kpe, p)
    new_gmlp_v = jnp.stack(gm_rows, axis=0)
    return (y_prompt, y_sample, new_gmlp_v, new_ckv_prompt, new_kpe_prompt, new_ckv_sample, new_kpe_sample)
```

```python
import functools
import math

import jax
import jax.numpy as jnp
from jax import lax
from jax.experimental import pallas as pl
from jax.experimental.pallas import tpu as pltpu

F32 = jnp.float32
BF16 = jnp.bfloat16

D_MODEL = 1024
DEPTH = 4
N_A = DEPTH // 2
CHUNK = 64
ALPHA = (2.0 * DEPTH) ** 0.25
LN_EPS = 1e-5
RMS_EPS = 1e-6

GM_CHUNK = 128
GM_HALF = 2 * D_MODEL
GM_GROUPS = 8
GM_GROUP_DIM = GM_HALF // GM_GROUPS

MLA_HEADS = 8
QK_NOPE = 128
QK_ROPE = 64
V_HEAD = 128
KV_LORA = D_MODEL // 4
Q_LORA = 3 * D_MODEL // 8
ROPE_BASE = 10000.0
ATTN_SCALE = (QK_NOPE + QK_ROPE) ** -0.5
HEAD_SLOT = 256

PEER_HEADS = 8
PEER_NKEYS = 128
PEER_EXPERTS = PEER_NKEYS * PEER_NKEYS
PEER_DK = 256
PEER_TOPK = 16
PEER_PICKS = PEER_HEADS * PEER_TOPK
ROW_WORDS = 4
D_CHUNKS = D_MODEL // 128

ROW_TILE = 256
PEER_TILE = 128
ATTN_TILE = 512
VMEM_LIMIT = 56 * 1024 * 1024


def _cparams(sem):
    return pltpu.CompilerParams(dimension_semantics=sem, vmem_limit_bytes=VMEM_LIMIT)


def _gelu(x):
    return 0.5 * x * (1.0 + lax.erf(x * (1.0 / math.sqrt(2.0))))


def _ln(x, g, b):
    mu = jnp.mean(x, axis=-1, keepdims=True)
    xc = x - mu
    var = jnp.mean(xc * xc, axis=-1, keepdims=True)
    return xc * lax.rsqrt(var + LN_EPS) * g + b


def _rms(x, g):
    return x * lax.rsqrt(jnp.mean(x * x, axis=-1, keepdims=True) + RMS_EPS) * g


def _row_spec(tile, width):
    return pl.BlockSpec((tile, width), lambda i: (i, 0))


def _full_spec(shape):
    nd = len(shape)
    return pl.BlockSpec(shape, lambda i: (0,) * nd)


def _gm_in_kernel(x_ref, w_ref, b_ref, lg_ref, lb_ref, u_ref, v_ref):
    z = jnp.dot(x_ref[...].astype(BF16), w_ref[...], preferred_element_type=F32) + b_ref[...]
    z = _gelu(z)
    u_ref[...] = z[:, :GM_HALF]
    v_ref[...] = _ln(z[:, GM_HALF:], lg_ref[...], lb_ref[...])


def gm_in(x, w, b, lg, lb):
    T = x.shape[0]
    return pl.pallas_call(
        _gm_in_kernel,
        grid=(T // ROW_TILE,),
        in_specs=[_row_spec(ROW_TILE, D_MODEL), _full_spec(w.shape), _full_spec(b.shape),
                  _full_spec(lg.shape), _full_spec(lb.shape)],
        out_specs=[_row_spec(ROW_TILE, GM_HALF), _row_spec(ROW_TILE, GM_HALF)],
        out_shape=[jax.ShapeDtypeStruct((T, GM_HALF), F32)] * 2,
        compiler_params=_cparams(("parallel",)),
        name="gm_in",
    )(x, w, b, lg, lb)


def _gm_mix_kernel(x_ref, u_ref, v_ref, ws_ref, bs_ref, wo_ref, g_ref, b_ref, o_ref, s_scr):
    for c in range(ROW_TILE // GM_CHUNK):
        r = slice(c * GM_CHUNK, (c + 1) * GM_CHUNK)
        for g in range(GM_GROUPS):
            cs = slice(g * GM_GROUP_DIM, (g + 1) * GM_GROUP_DIM)
            sv = jnp.dot(ws_ref[0, g], v_ref[r, cs].astype(BF16), preferred_element_type=F32) + bs_ref[0, g]
            s_scr[r, cs] = (u_ref[r, cs] * sv).astype(BF16)
    mix = jnp.dot(s_scr[...], wo_ref[...], preferred_element_type=F32)
    o_ref[...] = _ln(ALPHA * x_ref[...] + mix, g_ref[...], b_ref[...])


def gm_mix(x, u, v, ws, bs, wo, g, b, n_prompt_tiles):
    T = x.shape[0]
    sel = lambda i: (jnp.where(i >= n_prompt_tiles, 1, 0), 0, 0, 0)
    return pl.pallas_call(
        _gm_mix_kernel,
        grid=(T // ROW_TILE,),
        in_specs=[_row_spec(ROW_TILE, D_MODEL), _row_spec(ROW_TILE, GM_HALF), _row_spec(ROW_TILE, GM_HALF),
                  pl.BlockSpec((1,) + ws.shape[1:], sel), pl.BlockSpec((1,) + bs.shape[1:], sel),
                  _full_spec(wo.shape), _full_spec(g.shape), _full_spec(b.shape)],
        out_specs=_row_spec(ROW_TILE, D_MODEL),
        out_shape=jax.ShapeDtypeStruct((T, D_MODEL), F32),
        scratch_shapes=[pltpu.VMEM((ROW_TILE, GM_HALF), BF16)],
        compiler_params=_cparams(("parallel",)),
        name="gm_mix",
    )(x, u, v, ws, bs, wo, g, b)


def _resid_ln_kernel(x_ref, r_ref, g_ref, b_ref, o_ref):
    o_ref[...] = _ln(ALPHA * x_ref[...] + r_ref[...], g_ref[...], b_ref[...])


def resid_ln(x, r, g, b):
    T = x.shape[0]
    return pl.pallas_call(
        _resid_ln_kernel,
        grid=(T // ROW_TILE,),
        in_specs=[_row_spec(ROW_TILE, D_MODEL), _row_spec(ROW_TILE, D_MODEL), _full_spec(g.shape), _full_spec(b.shape)],
        out_specs=_row_spec(ROW_TILE, D_MODEL),
        out_shape=jax.ShapeDtypeStruct((T, D_MODEL), F32),
        compiler_params=_cparams(("parallel",)),
        name="resid_ln",
    )(x, r, g, b)


def _topk_rows(s, n_rows, pick=None):
    rows = lax.broadcasted_iota(jnp.int32, s.shape, 0)
    krow = lax.broadcasted_iota(jnp.int32, (PEER_TOPK, s.shape[1]), 0)
    vals, idxs, pa, pb = [], [], [], []
    for _ in range(PEER_TOPK):
        m = jnp.max(s, axis=0, keepdims=True)
        idx = jnp.min(jnp.where(s == m, rows, n_rows), axis=0, keepdims=True)
        vals.append(m)
        idxs.append(idx)
        if pick is not None:
            hi = idx >> 4
            lo = idx & (PEER_TOPK - 1)
            pa.append(jnp.sum(jnp.where(krow == hi, pick[0], 0), axis=0, keepdims=True))
            pb.append(jnp.sum(jnp.where(krow == lo, pick[1], 0), axis=0, keepdims=True))
        s = jnp.where(rows == idx, -jnp.inf, s)
    v = jnp.concatenate(vals, axis=0)
    if pick is None:
        return v, jnp.concatenate(idxs, axis=0)
    return v, jnp.concatenate(pa, axis=0), jnp.concatenate(pb, axis=0)


def _peer_topk_kernel(x_ref, wq_ref, k_ref, eid_ref, gate_ref):
    q = jnp.dot(x_ref[...].astype(BF16), wq_ref[...], preferred_element_type=F32)
    half = PEER_DK // 2
    nt = (((1,), (1,)), ((), ()))
    eids, gates = [], []
    for h in range(PEER_HEADS):
        q1 = q[:, h * PEER_DK:h * PEER_DK + half].astype(BF16)
        q2 = q[:, h * PEER_DK + half:(h + 1) * PEER_DK].astype(BF16)
        s1 = lax.dot_general(k_ref[0], q1, nt, preferred_element_type=F32)
        s2 = lax.dot_general(k_ref[1], q2, nt, preferred_element_type=F32)
        v1, i1 = _topk_rows(s1, PEER_NKEYS)
        v2, i2 = _topk_rows(s2, PEER_NKEYS)
        cand = jnp.concatenate([v1[a:a + 1] + v2 for a in range(PEER_TOPK)], axis=0)
        sc, e1, e2 = _topk_rows(cand, PEER_TOPK * PEER_TOPK, pick=(i1, i2))
        e = jnp.exp(sc - sc[0:1])
        gates.append(e / jnp.sum(e, axis=0, keepdims=True))
        eids.append((e1 * PEER_NKEYS + e2) * ROW_WORDS)
    eid_ref[...] = jnp.transpose(jnp.concatenate(eids, axis=0))
    gate_ref[...] = jnp.transpose(jnp.concatenate(gates, axis=0))


def peer_topk(x, wq, keys):
    T = x.shape[0]
    return pl.pallas_call(
        _peer_topk_kernel,
        grid=(T // PEER_TILE,),
        in_specs=[_row_spec(PEER_TILE, D_MODEL), _full_spec(wq.shape), _full_spec(keys.shape)],
        out_specs=[_row_spec(PEER_TILE, PEER_PICKS), _row_spec(PEER_TILE, PEER_PICKS)],
        out_shape=[jax.ShapeDtypeStruct((T, PEER_PICKS), jnp.int32), jax.ShapeDtypeStruct((T, PEER_PICKS), F32)],
        compiler_params=_cparams(("parallel",)),
        name="peer_topk",
    )(x, wq, keys)


def _gather_rows(eid_ref, t, tab_ref, dst_ref):
    for p in range(PEER_PICKS):
        row = pl.multiple_of(eid_ref[t, p], ROW_WORDS)
        dst_ref[pl.ds(ROW_WORDS * p, ROW_WORDS), :] = tab_ref[pl.ds(row, ROW_WORDS), :]


def _chunk_mask():
    sub = lax.broadcasted_iota(jnp.int32, (D_CHUNKS, PEER_PICKS * D_CHUNKS), 0)
    col = lax.broadcasted_iota(jnp.int32, (D_CHUNKS, PEER_PICKS * D_CHUNKS), 1)
    return ((col & (D_CHUNKS - 1)) == sub).astype(F32)


def _for_tokens_pipelined(eid_ref, tab_ref, buf0, buf1, compute):
    _gather_rows(eid_ref, 0, tab_ref, buf0)

    def body(i, carry):
        t0 = 2 * i
        _gather_rows(eid_ref, t0 + 1, tab_ref, buf1)
        compute(t0, buf0)
        _gather_rows(eid_ref, jnp.minimum(t0 + 2, PEER_TILE - 1), tab_ref, buf0)
        compute(t0 + 1, buf1)
        return carry

    lax.fori_loop(0, PEER_TILE // 2, body, 0)


def _peer_u_kernel(eid_ref, x_ref, tab_ref, fold_ref, h_ref, buf0, buf1, hrep):
    mask = _chunk_mask()
    nt = (((1,), (1,)), ((), ()))

    def compute(t, rows_ref):
        u = pltpu.bitcast(rows_ref[...], BF16)
        o = lax.dot_general(x_ref[t], u, nt, preferred_element_type=F32)
        o8 = (o[0:D_CHUNKS] + o[D_CHUNKS:2 * D_CHUNKS]) * mask
        hrep[pl.ds(t, 1), :] = jnp.sum(o8, axis=0, keepdims=True)

    _for_tokens_pipelined(eid_ref, tab_ref, buf0, buf1, compute)
    h_ref[...] = jnp.dot(hrep[...], fold_ref[...], precision=lax.Precision.HIGHEST, preferred_element_type=F32)


def _peer_v_kernel(eid_ref, h_ref, gate_ref, tab_ref, spread_ref, o_ref, buf0, buf1, a_hi, a_lo):
    a = gate_ref[...] * _gelu(h_ref[...])
    hi = a.astype(BF16)
    lo = (a - hi.astype(F32)).astype(BF16)
    a_hi[...] = jnp.dot(hi, spread_ref[...], preferred_element_type=F32)
    a_lo[...] = jnp.dot(lo, spread_ref[...], preferred_element_type=F32)
    mask = _chunk_mask()

    def compute(t, rows_ref):
        v = pltpu.bitcast(rows_ref[...], BF16)
        lhs = jnp.concatenate([(a_hi[pl.ds(t, 1), :] * mask).astype(BF16),
                               (a_lo[pl.ds(t, 1), :] * mask).astype(BF16)], axis=0)
        o = jnp.dot(lhs, v, preferred_element_type=F32)
        o_ref[t] = o[0:D_CHUNKS] + o[D_CHUNKS:2 * D_CHUNKS]

    _for_tokens_pipelined(eid_ref, tab_ref, buf0, buf1, compute)


def _peer_specs():
    eid_spec = pl.BlockSpec((PEER_TILE, PEER_PICKS), lambda i: (i, 0), memory_space=pltpu.SMEM)
    tab_spec = pl.BlockSpec((PEER_EXPERTS * ROW_WORDS, 128), lambda i: (0, 0), pipeline_mode=pl.Buffered(1))
    bufs = [pltpu.VMEM((PEER_PICKS * ROW_WORDS, 128), jnp.int32)] * 2
    return eid_spec, tab_spec, bufs


def peer_u(eid, x16, tab, fold):
    T = eid.shape[0]
    eid_spec, tab_spec, bufs = _peer_specs()
    return pl.pallas_call(
        _peer_u_kernel,
        grid=(T // PEER_TILE,),
        in_specs=[eid_spec, pl.BlockSpec((PEER_TILE, 2 * D_CHUNKS, 128), lambda i: (i, 0, 0)), tab_spec,
                  _full_spec(fold.shape)],
        out_specs=_row_spec(PEER_TILE, PEER_PICKS),
        out_shape=jax.ShapeDtypeStruct((T, PEER_PICKS), F32),
        scratch_shapes=bufs + [pltpu.VMEM((PEER_TILE, PEER_PICKS * D_CHUNKS), F32)],
        compiler_params=_cparams(("arbitrary",)),
        name="peer_u",
    )(eid, x16, tab, fold)


def peer_v(eid, hdn, gate, tab, spread):
    T = eid.shape[0]
    eid_spec, tab_spec, bufs = _peer_specs()
    return pl.pallas_call(
        _peer_v_kernel,
        grid=(T // PEER_TILE,),
        in_specs=[eid_spec, _row_spec(PEER_TILE, PEER_PICKS), _row_spec(PEER_TILE, PEER_PICKS), tab_spec,
                  _full_spec(spread.shape)],
        out_specs=pl.BlockSpec((PEER_TILE, D_CHUNKS, 128), lambda i: (i, 0, 0)),
        out_shape=jax.ShapeDtypeStruct((T, D_CHUNKS, 128), F32),
        scratch_shapes=bufs + [pltpu.VMEM((PEER_TILE, PEER_PICKS * D_CHUNKS), F32)] * 2,
        compiler_params=_cparams(("arbitrary",)),
        name="peer_v",
    )(eid, hdn, gate, tab, spread)


def pack_table(tab):
    e = tab.shape[0]
    t = tab.astype(BF16).reshape(e, ROW_WORDS, 2, 128)
    t = jnp.transpose(t, (0, 1, 3, 2))
    return lax.bitcast_convert_type(t, jnp.int32).reshape(e * ROW_WORDS, 128)


def split_chunks(x):
    T = x.shape[0]
    hi = x.astype(BF16)
    lo = (x - hi.astype(F32)).astype(BF16)
    return jnp.concatenate([hi.reshape(T, D_CHUNKS, 128), lo.reshape(T, D_CHUNKS, 128)], axis=1)


def peer_consts():
    col = jnp.arange(PEER_PICKS * D_CHUNKS) // D_CHUNKS
    spread = (col[None, :] == jnp.arange(PEER_PICKS)[:, None])
    return spread.astype(BF16), jnp.transpose(spread).astype(F32)


def peer(x, wq, keys, u_packed, v_packed, spread, fold):
    eid, gate = peer_topk(x, wq, keys)
    hdn = peer_u(eid, split_chunks(x), u_packed, fold)
    out = peer_v(eid, hdn, gate, v_packed, spread)
    return out.reshape(x.shape[0], D_MODEL)


def _mla_kv_kernel(x_ref, w_ref, g_ref, cos_ref, sin_ref, c_ref, kpe_ref):
    kv = jnp.dot(x_ref[...].astype(BF16), w_ref[...], preferred_element_type=F32)
    c_ref[...] = _rms(kv[:, :KV_LORA], g_ref[...])
    kp = kv[:, KV_LORA:KV_LORA + QK_ROPE]
    sw = jnp.concatenate([kp[:, QK_ROPE // 2:], kp[:, :QK_ROPE // 2]], axis=1)
    kpe_ref[...] = kp * cos_ref[...] + sw * sin_ref[...]


def mla_kv(x, w, g, cos2, sin2):
    T = x.shape[0]
    return pl.pallas_call(
        _mla_kv_kernel,
        grid=(T // ROW_TILE,),
        in_specs=[_row_spec(ROW_TILE, D_MODEL), _full_spec(w.shape), _full_spec(g.shape),
                  _row_spec(ROW_TILE, QK_ROPE), _row_spec(ROW_TILE, QK_ROPE)],
        out_specs=[_row_spec(ROW_TILE, KV_LORA), _row_spec(ROW_TILE, QK_ROPE)],
        out_shape=[jax.ShapeDtypeStruct((T, KV_LORA), F32), jax.ShapeDtypeStruct((T, QK_ROPE), F32)],
        compiler_params=_cparams(("parallel",)),
        name="mla_kv",
    )(x, w, g, cos2, sin2)


def _mla_expand_kernel(c_ref, kpe_ref, wk_ref, wv_ref, k_ref, v_ref):
    cb = c_ref[...].astype(BF16)
    kn = jnp.dot(cb, wk_ref[...], preferred_element_type=F32)
    v_ref[...] = jnp.dot(cb, wv_ref[...], preferred_element_type=F32).astype(BF16)
    kp = kpe_ref[...]
    kpad = jnp.concatenate([kp, jnp.zeros_like(kp)], axis=1).astype(BF16)
    for h in range(MLA_HEADS):
        k_ref[:, h * HEAD_SLOT:h * HEAD_SLOT + QK_NOPE] = kn[:, h * QK_NOPE:(h + 1) * QK_NOPE].astype(BF16)
        k_ref[:, h * HEAD_SLOT + QK_NOPE:(h + 1) * HEAD_SLOT] = kpad


def mla_expand(c, kpe, wk, wv):
    R = c.shape[0]
    return pl.pallas_call(
        _mla_expand_kernel,
        grid=(R // ROW_TILE,),
        in_specs=[_row_spec(ROW_TILE, KV_LORA), _row_spec(ROW_TILE, QK_ROPE), _full_spec(wk.shape), _full_spec(wv.shape)],
        out_specs=[_row_spec(ROW_TILE, MLA_HEADS * HEAD_SLOT), _row_spec(ROW_TILE, MLA_HEADS * V_HEAD)],
        out_shape=[jax.ShapeDtypeStruct((R, MLA_HEADS * HEAD_SLOT), BF16), jax.ShapeDtypeStruct((R, MLA_HEADS * V_HEAD), BF16)],
        compiler_params=_cparams(("parallel",)),
        name="mla_expand",
    )(c, kpe, wk, wv)


def _mla_q_kernel(x_ref, wdq_ref, g_ref, wuq_ref, cos_ref, sin_ref, q_ref):
    cq = _rms(jnp.dot(x_ref[...].astype(BF16), wdq_ref[...], preferred_element_type=F32), g_ref[...])
    q = jnp.dot(cq.astype(BF16), wuq_ref[...], preferred_element_type=F32)
    lane = lax.broadcasted_iota(jnp.int32, (ROW_TILE, 128), 1)
    cos = cos_ref[...]
    sin = sin_ref[...]
    for h in range(MLA_HEADS):
        q_ref[:, h * HEAD_SLOT:h * HEAD_SLOT + QK_NOPE] = q[:, h * HEAD_SLOT:h * HEAD_SLOT + QK_NOPE].astype(BF16)
        seg = q[:, h * HEAD_SLOT + QK_NOPE:(h + 1) * HEAD_SLOT]
        sw = jnp.where(lane < QK_ROPE // 2, pltpu.roll(seg, 128 - QK_ROPE // 2, 1),
                       jnp.where(lane < QK_ROPE, pltpu.roll(seg, QK_ROPE // 2, 1), 0.0))
        q_ref[:, h * HEAD_SLOT + QK_NOPE:(h + 1) * HEAD_SLOT] = (seg * cos + sw * sin).astype(BF16)


def mla_q(x, wdq, g, wuq, cosq, sinq):
    T = x.shape[0]
    return pl.pallas_call(
        _mla_q_kernel,
        grid=(T // ROW_TILE,),
        in_specs=[_row_spec(ROW_TILE, D_MODEL), _full_spec(wdq.shape), _full_spec(g.shape), _full_spec(wuq.shape),
                  _row_spec(ROW_TILE, 128), _row_spec(ROW_TILE, 128)],
        out_specs=_row_spec(ROW_TILE, MLA_HEADS * HEAD_SLOT),
        out_shape=jax.ShapeDtypeStruct((T, MLA_HEADS * HEAD_SLOT), BF16),
        compiler_params=_cparams(("parallel",)),
        name="mla_q",
    )(x, wdq, g, wuq, cosq, sinq)


def _attn_prompt_kernel(q_ref, k_ref, v_ref, o_ref):
    qi = pl.program_id(2)
    q = q_ref[...]
    nt = (((1,), (1,)), ((), ()))

    def step(ki, carry, diagonal):
        m, l, acc = carry
        off = pl.multiple_of(ki * ATTN_TILE, ATTN_TILE)
        s = lax.dot_general(q, k_ref[pl.ds(off, ATTN_TILE), :], nt, preferred_element_type=F32) * ATTN_SCALE
        if diagonal:
            qc = lax.broadcasted_iota(jnp.int32, s.shape, 0) // CHUNK
            kc = lax.broadcasted_iota(jnp.int32, s.shape, 1) // CHUNK
            s = jnp.where(kc <= qc, s, -jnp.inf)
        m_new = jnp.maximum(m, jnp.max(s, axis=1, keepdims=True))
        alpha = jnp.exp(m - m_new)
        p = jnp.exp(s - m_new)
        l = alpha * l + jnp.sum(p, axis=1, keepdims=True)
        acc = alpha * acc + jnp.dot(p.astype(BF16), v_ref[pl.ds(off, ATTN_TILE), :], preferred_element_type=F32)
        return m_new, l, acc

    init = (jnp.full((ATTN_TILE, 1), -jnp.inf, F32), jnp.zeros((ATTN_TILE, 1), F32), jnp.zeros((ATTN_TILE, V_HEAD), F32))
    carry = lax.fori_loop(0, qi, lambda ki, c: step(ki, c, False), init)
    m, l, acc = step(qi, carry, True)
    o_ref[...] = (acc / l).astype(BF16)


def attn_prompt(q, k, v, batch, seq):
    nq = seq // ATTN_TILE
    return pl.pallas_call(
        _attn_prompt_kernel,
        grid=(batch, MLA_HEADS, nq),
        in_specs=[pl.BlockSpec((ATTN_TILE, HEAD_SLOT), lambda b, h, i: (b * nq + i, h)),
                  pl.BlockSpec((seq, HEAD_SLOT), lambda b, h, i: (b, h)),
                  pl.BlockSpec((seq, V_HEAD), lambda b, h, i: (b, h))],
        out_specs=pl.BlockSpec((ATTN_TILE, V_HEAD), lambda b, h, i: (b * nq + i, h)),
        out_shape=jax.ShapeDtypeStruct((batch * seq, MLA_HEADS * V_HEAD), BF16),
        compiler_params=_cparams(("parallel", "parallel", "arbitrary")),
        name="attn_prompt",
    )(q, k, v)


def _attn_sample_kernel(q_ref, k_ref, v_ref, o_ref):
    nt = (((1,), (1,)), ((), ()))
    s = lax.dot_general(q_ref[...], k_ref[...], nt, preferred_element_type=F32) * ATTN_SCALE
    p = jnp.exp(s - jnp.max(s, axis=1, keepdims=True))
    o = jnp.dot(p.astype(BF16), v_ref[...], preferred_element_type=F32)
    o_ref[...] = (o / jnp.sum(p, axis=1, keepdims=True)).astype(BF16)


def attn_sample(q, k, v, batch, q_len, kv_len, q_block0):
    return pl.pallas_call(
        _attn_sample_kernel,
        grid=(batch, MLA_HEADS),
        in_specs=[pl.BlockSpec((q_len, HEAD_SLOT), lambda b, h: (q_block0 + b, h)),
                  pl.BlockSpec((kv_len, HEAD_SLOT), lambda b, h: (b, h)),
                  pl.BlockSpec((kv_len, V_HEAD), lambda b, h: (b, h))],
        out_specs=pl.BlockSpec((q_len, V_HEAD), lambda b, h: (b, h)),
        out_shape=jax.ShapeDtypeStruct((batch * q_len, MLA_HEADS * V_HEAD), BF16),
        compiler_params=_cparams(("parallel", "parallel")),
        name="attn_sample",
    )(q, k, v)


def _proj_ln_kernel(x_ref, o_ref_in, w_ref, g_ref, b_ref, y_ref):
    mix = jnp.dot(o_ref_in[...], w_ref[...], preferred_element_type=F32)
    y_ref[...] = _ln(ALPHA * x_ref[...] + mix, g_ref[...], b_ref[...])


def proj_ln(x, o, w, g, b):
    T = x.shape[0]
    return pl.pallas_call(
        _proj_ln_kernel,
        grid=(T // ROW_TILE,),
        in_specs=[_row_spec(ROW_TILE, D_MODEL), _row_spec(ROW_TILE, o.shape[1]), _full_spec(w.shape),
                  _full_spec(g.shape), _full_spec(b.shape)],
        out_specs=_row_spec(ROW_TILE, D_MODEL),
        out_shape=jax.ShapeDtypeStruct((T, D_MODEL), F32),
        compiler_params=_cparams(("parallel",)),
        name="proj_ln",
    )(x, o, w, g, b)


def _rope_tables(pos):
    inv = 1.0 / (ROPE_BASE ** (jnp.arange(0, QK_ROPE, 2, dtype=F32) / QK_ROPE))
    ang = pos.astype(F32)[:, None] * inv[None, :]
    cos, sin = jnp.cos(ang), jnp.sin(ang)
    cos2 = jnp.concatenate([cos, cos], axis=1)
    sin2 = jnp.concatenate([-sin, sin], axis=1)
    pad = jnp.zeros_like(cos2)
    return cos2, sin2, jnp.concatenate([cos2, pad], axis=1), jnp.concatenate([sin2, pad], axis=1)


def _gm_chunk_mats(w_s, b_s, dec_seq):
    i = jnp.arange(GM_CHUNK)
    mask = (i[None, :] // CHUNK) <= (i[:, None] // CHUNK)
    full = jnp.where(mask[None], w_s, 0.0)
    rep = GM_CHUNK // dec_seq
    blk = w_s[:, :dec_seq, :dec_seq]
    same = (i[:, None] // dec_seq) == (i[None, :] // dec_seq)
    diag = jnp.where(same[None], jnp.tile(blk, (1, rep, rep)), 0.0)
    ws = jnp.stack([full, diag], axis=0).astype(BF16)
    bias = jnp.stack([b_s, jnp.tile(b_s[:, :dec_seq], (1, rep))], axis=0)
    bs = jnp.broadcast_to(bias[..., None], bias.shape + (GM_GROUP_DIM,)).astype(F32)
    return ws, bs


def _pad_q_weight(w_uq):
    w = w_uq.reshape(Q_LORA, MLA_HEADS, QK_NOPE + QK_ROPE)
    w = jnp.pad(w, ((0, 0), (0, 0), (0, HEAD_SLOT - QK_NOPE - QK_ROPE)))
    return w.reshape(Q_LORA, MLA_HEADS * HEAD_SLOT).astype(BF16)


def kernel(x_prompt, x_sample, cache_ckv, cache_kpe, ln1_g, ln1_b, ln2_g, ln2_b, gm_w_in, gm_b_in, gm_ln_g, gm_ln_b,
           gm_w_s, gm_b_s, gm_w_out, mla_w_dkv, mla_kv_norm_g, mla_w_ukv, mla_w_dq, mla_q_norm_g, mla_w_uq, mla_w_o,
           peer_w_q, peer_subkeys, peer_u, peer_v):
    B, S, _ = x_prompt.shape
    DB, DS, _ = x_sample.shape
    PAST = cache_ckv.shape[1]
    TP, TS = B * S, DB * DS
    T = TP + TS
    assert TP % ROW_TILE == 0 and TS % ROW_TILE == 0 and T % PEER_TILE == 0 and S % ATTN_TILE == 0
    assert GM_CHUNK % DS == 0 and DS <= CHUNK and PAST % CHUNK == 0 and (PAST + DS) % 16 == 0

    x = jnp.concatenate([x_prompt.reshape(TP, D_MODEL), x_sample.reshape(TS, D_MODEL)], axis=0)
    pos = jnp.concatenate([jnp.tile(jnp.arange(S, dtype=jnp.int32), B),
                           jnp.tile(PAST + jnp.arange(DS, dtype=jnp.int32), DB)])
    cos2, sin2, cosq, sinq = _rope_tables(pos)
    spread, fold = peer_consts()
    row = lambda a: a.reshape(1, -1)

    def peer_block(x, l):
        out = peer(x, peer_w_q[l].astype(BF16), peer_subkeys[l].astype(BF16),
                   pack_table(peer_u[l]), pack_table(peer_v[l]), spread, fold)
        return resid_ln(x, out, row(ln2_g[l]), row(ln2_b[l]))

    gm_rows = []
    for l in range(N_A):
        u, v = gm_in(x, gm_w_in[l].astype(BF16), row(gm_b_in[l]), row(gm_ln_g[l]), row(gm_ln_b[l]))
        gm_rows.append(v[TP:].reshape(DB, DS, GM_HALF))
        ws, bs = _gm_chunk_mats(gm_w_s[l], gm_b_s[l], DS)
        x = gm_mix(x, u, v, ws, bs, gm_w_out[l].astype(BF16), row(ln1_g[l]), row(ln1_b[l]), TP // ROW_TILE)
        x = peer_block(x, l)

    w_dkv = jnp.pad(mla_w_dkv, ((0, 0), (0, 128 - QK_ROPE))).astype(BF16)
    c_new, kpe_new = mla_kv(x, w_dkv, row(mla_kv_norm_g), cos2, sin2)
    w_ukv = mla_w_ukv.reshape(KV_LORA, MLA_HEADS, QK_NOPE + V_HEAD)
    w_uk = w_ukv[:, :, :QK_NOPE].reshape(KV_LORA, MLA_HEADS * QK_NOPE).astype(BF16)
    w_uv = w_ukv[:, :, QK_NOPE:].reshape(KV_LORA, MLA_HEADS * V_HEAD).astype(BF16)
    kp_cat, vp = mla_expand(c_new[:TP], kpe_new[:TP], w_uk, w_uv)
    KV = PAST + DS
    c_all = jnp.concatenate([cache_ckv, c_new[TP:].reshape(DB, DS, KV_LORA)], axis=1).reshape(DB * KV, KV_LORA)
    kpe_all = jnp.concatenate([cache_kpe, kpe_new[TP:].reshape(DB, DS, QK_ROPE)], axis=1).reshape(DB * KV, QK_ROPE)
    ks_cat, vs = mla_expand(c_all, kpe_all, w_uk, w_uv)

    for j in range(DEPTH - N_A):
        l = N_A + j
        q = mla_q(x, mla_w_dq[j].astype(BF16), row(mla_q_norm_g[j]), _pad_q_weight(mla_w_uq[j]), cosq, sinq)
        o = jnp.concatenate([attn_prompt(q, kp_cat, vp, B, S),
                             attn_sample(q, ks_cat, vs, DB, DS, KV, TP // DS)], axis=0)
        x = proj_ln(x, o, mla_w_o[j].astype(BF16), row(ln1_g[l]), row(ln1_b[l]))
        x = peer_block(x, l)

    return (x[:TP].reshape(B, S, D_MODEL), x[TP:].reshape(DB, DS, D_MODEL), jnp.stack(gm_rows, axis=0),
            c_new[:TP].reshape(B, S, KV_LORA), kpe_new[:TP].reshape(B, S, QK_ROPE),
            c_new[TP:].reshape(DB, DS, KV_LORA), kpe_new[TP:].reshape(DB, DS, QK_ROPE))
```

```python
import functools
import math

import jax
import jax.numpy as jnp
from jax import lax
from jax.experimental import pallas as pl
from jax.experimental.pallas import tpu as pltpu

F32 = jnp.float32
BF16 = jnp.bfloat16

D_MODEL = 1024
DEPTH = 4
N_A = DEPTH // 2
CHUNK = 64
ALPHA = (2.0 * DEPTH) ** 0.25
LN_EPS = 1e-5
RMS_EPS = 1e-6

GM_CHUNK = 128
GM_HALF = 2 * D_MODEL
GM_GROUPS = 8
GM_GROUP_DIM = GM_HALF // GM_GROUPS

MLA_HEADS = 8
QK_NOPE = 128
QK_ROPE = 64
V_HEAD = 128
KV_LORA = D_MODEL // 4
Q_LORA = 3 * D_MODEL // 8
ROPE_BASE = 10000.0
ATTN_SCALE = (QK_NOPE + QK_ROPE) ** -0.5
HEAD_SLOT = 256

PEER_HEADS = 8
PEER_NKEYS = 128
PEER_EXPERTS = PEER_NKEYS * PEER_NKEYS
PEER_DK = 256
PEER_TOPK = 16
PEER_PICKS = PEER_HEADS * PEER_TOPK
TOPK_SHIFT = PEER_TOPK.bit_length() - 1
PAIR_SENTINEL = 1 << 20
ROW_WORDS = 4
D_CHUNKS = D_MODEL // 128

ROW_TILE = 256
PEER_TILE = 128
TOKENS_PER_TRIP = 8
ATTN_TILE = 512
VMEM_LIMIT = 56 * 1024 * 1024


def _cparams(sem):
    return pltpu.CompilerParams(dimension_semantics=sem, vmem_limit_bytes=VMEM_LIMIT)


def _gelu(x):
    return 0.5 * x * (1.0 + lax.erf(x * (1.0 / math.sqrt(2.0))))


def _ln(x, g, b):
    mu = jnp.mean(x, axis=-1, keepdims=True)
    xc = x - mu
    var = jnp.mean(xc * xc, axis=-1, keepdims=True)
    return xc * lax.rsqrt(var + LN_EPS) * g + b


def _rms(x, g):
    return x * lax.rsqrt(jnp.mean(x * x, axis=-1, keepdims=True) + RMS_EPS) * g


def _row_spec(tile, width):
    return pl.BlockSpec((tile, width), lambda i: (i, 0))


def _full_spec(shape):
    nd = len(shape)
    return pl.BlockSpec(shape, lambda i: (0,) * nd)


def _gm_in_kernel(x_ref, w_ref, b_ref, lg_ref, lb_ref, u_ref, v_ref):
    z = jnp.dot(x_ref[...].astype(BF16), w_ref[...], preferred_element_type=F32) + b_ref[...]
    z = _gelu(z)
    u_ref[...] = z[:, :GM_HALF]
    v_ref[...] = _ln(z[:, GM_HALF:], lg_ref[...], lb_ref[...])


def gm_in(x, w, b, lg, lb):
    T = x.shape[0]
    return pl.pallas_call(
        _gm_in_kernel,
        grid=(T // ROW_TILE,),
        in_specs=[_row_spec(ROW_TILE, D_MODEL), _full_spec(w.shape), _full_spec(b.shape),
                  _full_spec(lg.shape), _full_spec(lb.shape)],
        out_specs=[_row_spec(ROW_TILE, GM_HALF), _row_spec(ROW_TILE, GM_HALF)],
        out_shape=[jax.ShapeDtypeStruct((T, GM_HALF), F32)] * 2,
        compiler_params=_cparams(("parallel",)),
        name="gm_in",
    )(x, w, b, lg, lb)


def _gm_mix_kernel(x_ref, u_ref, v_ref, ws_ref, bs_ref, wo_ref, g_ref, b_ref, o_ref, s_scr):
    for c in range(ROW_TILE // GM_CHUNK):
        r = slice(c * GM_CHUNK, (c + 1) * GM_CHUNK)
        for g in range(GM_GROUPS):
            cs = slice(g * GM_GROUP_DIM, (g + 1) * GM_GROUP_DIM)
            sv = jnp.dot(ws_ref[0, g], v_ref[r, cs].astype(BF16), preferred_element_type=F32) + bs_ref[0, g]
            s_scr[r, cs] = (u_ref[r, cs] * sv).astype(BF16)
    mix = jnp.dot(s_scr[...], wo_ref[...], preferred_element_type=F32)
    o_ref[...] = _ln(ALPHA * x_ref[...] + mix, g_ref[...], b_ref[...])


def gm_mix(x, u, v, ws, bs, wo, g, b, n_prompt_tiles):
    T = x.shape[0]
    sel = lambda i: (jnp.where(i >= n_prompt_tiles, 1, 0), 0, 0, 0)
    return pl.pallas_call(
        _gm_mix_kernel,
        grid=(T // ROW_TILE,),
        in_specs=[_row_spec(ROW_TILE, D_MODEL), _row_spec(ROW_TILE, GM_HALF), _row_spec(ROW_TILE, GM_HALF),
                  pl.BlockSpec((1,) + ws.shape[1:], sel), pl.BlockSpec((1,) + bs.shape[1:], sel),
                  _full_spec(wo.shape), _full_spec(g.shape), _full_spec(b.shape)],
        out_specs=_row_spec(ROW_TILE, D_MODEL),
        out_shape=jax.ShapeDtypeStruct((T, D_MODEL), F32),
        scratch_shapes=[pltpu.VMEM((ROW_TILE, GM_HALF), BF16)],
        compiler_params=_cparams(("parallel",)),
        name="gm_mix",
    )(x, u, v, ws, bs, wo, g, b)


def _resid_ln_kernel(x_ref, r_ref, g_ref, b_ref, o_ref):
    o_ref[...] = _ln(ALPHA * x_ref[...] + r_ref[...], g_ref[...], b_ref[...])


def resid_ln(x, r, g, b):
    T = x.shape[0]
    return pl.pallas_call(
        _resid_ln_kernel,
        grid=(T // ROW_TILE,),
        in_specs=[_row_spec(ROW_TILE, D_MODEL), _row_spec(ROW_TILE, D_MODEL), _full_spec(g.shape), _full_spec(b.shape)],
        out_specs=_row_spec(ROW_TILE, D_MODEL),
        out_shape=jax.ShapeDtypeStruct((T, D_MODEL), F32),
        compiler_params=_cparams(("parallel",)),
        name="resid_ln",
    )(x, r, g, b)


def _extract_max(s, codes, sentinel):
    m = jnp.max(s, axis=0, keepdims=True)
    c = jnp.min(jnp.where(s == m, codes, sentinel), axis=0, keepdims=True)
    return m, c, jnp.where(codes == c, -jnp.inf, s)


def _topk_rows(s):
    rows = lax.broadcasted_iota(jnp.int32, s.shape, 0)
    vals, idxs = [], []
    for _ in range(PEER_TOPK):
        m, idx, s = _extract_max(s, rows, s.shape[0])
        vals.append(m)
        idxs.append(idx)
    return jnp.concatenate(vals, axis=0), jnp.concatenate(idxs, axis=0)


_PAIR_GROUPS = ((None, 0, 16), (None, 1, 8), (0, None, 16), (1, None, 8), (2, None, 8), (3, None, 8), (4, None, 8))


def pair_codes():
    codes = []
    for b_fix, a_fix, n in _PAIR_GROUPS:
        for r in range(n):
            a, b = (a_fix, r) if b_fix is None else (r, b_fix)
            first = a_fix is not None or a >= 2
            ok = first and (a + 1) * (b + 1) <= PEER_TOPK
            codes.append(a * PEER_TOPK + b if ok else PAIR_SENTINEL)
    assert sorted(c for c in codes if c != PAIR_SENTINEL) == sorted(
        a * PEER_TOPK + b for a in range(PEER_TOPK) for b in range(PEER_TOPK) if (a + 1) * (b + 1) <= PEER_TOPK)
    return jnp.broadcast_to(jnp.asarray(codes, jnp.int32)[:, None], (len(codes), PEER_TILE))


def _topk_pairs(v1, i1, v2, i2, codes):
    parts = []
    for b_fix, a_fix, n in _PAIR_GROUPS:
        if b_fix is None:
            parts.append(v1[a_fix:a_fix + 1] + v2[0:n])
        else:
            parts.append(v1[0:n] + v2[b_fix:b_fix + 1])
    cand = jnp.where(codes < PAIR_SENTINEL, jnp.concatenate(parts, axis=0), -jnp.inf)
    krow = lax.broadcasted_iota(jnp.int32, v1.shape, 0)
    vals, k1, k2 = [], [], []
    for _ in range(PEER_TOPK):
        m, c, cand = _extract_max(cand, codes, PAIR_SENTINEL)
        vals.append(m)
        k1.append(jnp.sum(jnp.where(krow == (c >> TOPK_SHIFT), i1, 0), axis=0, keepdims=True))
        k2.append(jnp.sum(jnp.where(krow == (c & (PEER_TOPK - 1)), i2, 0), axis=0, keepdims=True))
    return jnp.concatenate(vals, axis=0), jnp.concatenate(k1, axis=0), jnp.concatenate(k2, axis=0)


def _peer_topk_kernel(x_ref, wq_ref, k_ref, codes_ref, eid_ref, gate_ref):
    q = jnp.dot(x_ref[...].astype(BF16), wq_ref[...], preferred_element_type=F32)
    half = PEER_DK // 2
    nt = (((1,), (1,)), ((), ()))
    codes = codes_ref[...]
    eids, gates = [], []
    for h in range(PEER_HEADS):
        q1 = q[:, h * PEER_DK:h * PEER_DK + half].astype(BF16)
        q2 = q[:, h * PEER_DK + half:(h + 1) * PEER_DK].astype(BF16)
        s1 = lax.dot_general(k_ref[0], q1, nt, preferred_element_type=F32)
        s2 = lax.dot_general(k_ref[1], q2, nt, preferred_element_type=F32)
        v1, i1 = _topk_rows(s1)
        v2, i2 = _topk_rows(s2)
        sc, e1, e2 = _topk_pairs(v1, i1, v2, i2, codes)
        e = jnp.exp(sc - sc[0:1])
        gates.append(e / jnp.sum(e, axis=0, keepdims=True))
        eids.append((e1 * PEER_NKEYS + e2) * ROW_WORDS)
    eid_ref[...] = jnp.transpose(jnp.concatenate(eids, axis=0))
    gate_ref[...] = jnp.transpose(jnp.concatenate(gates, axis=0))


def peer_topk(x, wq, keys, codes):
    T = x.shape[0]
    return pl.pallas_call(
        _peer_topk_kernel,
        grid=(T // PEER_TILE,),
        in_specs=[_row_spec(PEER_TILE, D_MODEL), _full_spec(wq.shape), _full_spec(keys.shape), _full_spec(codes.shape)],
        out_specs=[_row_spec(PEER_TILE, PEER_PICKS), _row_spec(PEER_TILE, PEER_PICKS)],
        out_shape=[jax.ShapeDtypeStruct((T, PEER_PICKS), jnp.int32), jax.ShapeDtypeStruct((T, PEER_PICKS), F32)],
        compiler_params=_cparams(("parallel",)),
        name="peer_topk",
    )(x, wq, keys, codes)


def _gather_rows(eid_ref, t, tab_ref, dst_ref):
    picks = eid_ref.at[t]
    for p in range(PEER_PICKS):
        row = pl.multiple_of(picks[p], ROW_WORDS)
        dst_ref[pl.ds(ROW_WORDS * p, ROW_WORDS), :] = tab_ref[pl.ds(row, ROW_WORDS), :]


def _chunk_mask():
    sub = lax.broadcasted_iota(jnp.int32, (D_CHUNKS, PEER_PICKS * D_CHUNKS), 0)
    col = lax.broadcasted_iota(jnp.int32, (D_CHUNKS, PEER_PICKS * D_CHUNKS), 1)
    return ((col & (D_CHUNKS - 1)) == sub).astype(F32)


def _for_tokens_pipelined(eid_ref, tab_ref, buf0, buf1, compute):
    bufs = (buf0, buf1)
    _gather_rows(eid_ref, 0, tab_ref, buf0)

    def body(i, carry):
        t0 = TOKENS_PER_TRIP * i
        for k in range(TOKENS_PER_TRIP):
            _gather_rows(eid_ref, jnp.minimum(t0 + k + 1, PEER_TILE - 1), tab_ref, bufs[(k + 1) % 2])
            compute(t0 + k, bufs[k % 2])
        return carry

    lax.fori_loop(0, PEER_TILE // TOKENS_PER_TRIP, body, 0)


def _peer_u_kernel(eid_ref, x_ref, tab_ref, fold_ref, h_ref, buf0, buf1, hrep):
    mask = _chunk_mask()
    nt = (((1,), (1,)), ((), ()))

    def compute(t, rows_ref):
        u = pltpu.bitcast(rows_ref[...], BF16)
        o = lax.dot_general(x_ref[t], u, nt, preferred_element_type=F32)
        o8 = (o[0:D_CHUNKS] + o[D_CHUNKS:2 * D_CHUNKS]) * mask
        hrep[pl.ds(t, 1), :] = jnp.sum(o8, axis=0, keepdims=True)

    _for_tokens_pipelined(eid_ref, tab_ref, buf0, buf1, compute)
    h_ref[...] = jnp.dot(hrep[...], fold_ref[...], precision=lax.Precision.HIGHEST, preferred_element_type=F32)


def _peer_v_kernel(eid_ref, h_ref, gate_ref, tab_ref, spread_ref, o_ref, buf0, buf1, a_hi, a_lo):
    a = gate_ref[...] * _gelu(h_ref[...])
    hi = a.astype(BF16)
    lo = (a - hi.astype(F32)).astype(BF16)
    a_hi[...] = jnp.dot(hi, spread_ref[...], preferred_element_type=F32)
    a_lo[...] = jnp.dot(lo, spread_ref[...], preferred_element_type=F32)
    mask = _chunk_mask()

    def compute(t, rows_ref):
        v = pltpu.bitcast(rows_ref[...], BF16)
        lhs = jnp.concatenate([(a_hi[pl.ds(t, 1), :] * mask).astype(BF16),
                               (a_lo[pl.ds(t, 1), :] * mask).astype(BF16)], axis=0)
        o = jnp.dot(lhs, v, preferred_element_type=F32)
        o_ref[t] = o[0:D_CHUNKS] + o[D_CHUNKS:2 * D_CHUNKS]

    _for_tokens_pipelined(eid_ref, tab_ref, buf0, buf1, compute)


def _peer_specs():
    eid_spec = pl.BlockSpec((PEER_TILE, PEER_PICKS), lambda i: (i, 0), memory_space=pltpu.SMEM)
    tab_spec = pl.BlockSpec((PEER_EXPERTS * ROW_WORDS, 128), lambda i: (0, 0), pipeline_mode=pl.Buffered(1))
    bufs = [pltpu.VMEM((PEER_PICKS * ROW_WORDS, 128), jnp.int32)] * 2
    return eid_spec, tab_spec, bufs


def peer_u(eid, x16, tab, fold):
    T = eid.shape[0]
    eid_spec, tab_spec, bufs = _peer_specs()
    return pl.pallas_call(
        _peer_u_kernel,
        grid=(T // PEER_TILE,),
        in_specs=[eid_spec, pl.BlockSpec((PEER_TILE, 2 * D_CHUNKS, 128), lambda i: (i, 0, 0)), tab_spec,
                  _full_spec(fold.shape)],
        out_specs=_row_spec(PEER_TILE, PEER_PICKS),
        out_shape=jax.ShapeDtypeStruct((T, PEER_PICKS), F32),
        scratch_shapes=bufs + [pltpu.VMEM((PEER_TILE, PEER_PICKS * D_CHUNKS), F32)],
        compiler_params=_cparams(("arbitrary",)),
        name="peer_u",
    )(eid, x16, tab, fold)


def peer_v(eid, hdn, gate, tab, spread):
    T = eid.shape[0]
    eid_spec, tab_spec, bufs = _peer_specs()
    return pl.pallas_call(
        _peer_v_kernel,
        grid=(T // PEER_TILE,),
        in_specs=[eid_spec, _row_spec(PEER_TILE, PEER_PICKS), _row_spec(PEER_TILE, PEER_PICKS), tab_spec,
                  _full_spec(spread.shape)],
        out_specs=pl.BlockSpec((PEER_TILE, D_CHUNKS, 128), lambda i: (i, 0, 0)),
        out_shape=jax.ShapeDtypeStruct((T, D_CHUNKS, 128), F32),
        scratch_shapes=bufs + [pltpu.VMEM((PEER_TILE, PEER_PICKS * D_CHUNKS), F32)] * 2,
        compiler_params=_cparams(("arbitrary",)),
        name="peer_v",
    )(eid, hdn, gate, tab, spread)


def pack_table(tab):
    e = tab.shape[0]
    t = tab.astype(BF16).reshape(e, ROW_WORDS, 2, 128)
    t = jnp.transpose(t, (0, 1, 3, 2))
    return lax.bitcast_convert_type(t, jnp.int32).reshape(e * ROW_WORDS, 128)


def split_chunks(x):
    T = x.shape[0]
    hi = x.astype(BF16)
    lo = (x - hi.astype(F32)).astype(BF16)
    return jnp.concatenate([hi.reshape(T, D_CHUNKS, 128), lo.reshape(T, D_CHUNKS, 128)], axis=1)


def peer_consts():
    col = jnp.arange(PEER_PICKS * D_CHUNKS) // D_CHUNKS
    spread = (col[None, :] == jnp.arange(PEER_PICKS)[:, None])
    return spread.astype(BF16), jnp.transpose(spread).astype(F32)


def peer(x, wq, keys, u_packed, v_packed, spread, fold):
    eid, gate = peer_topk(x, wq, keys, pair_codes())
    hdn = peer_u(eid, split_chunks(x), u_packed, fold)
    out = peer_v(eid, hdn, gate, v_packed, spread)
    return out.reshape(x.shape[0], D_MODEL)


def _mla_kv_kernel(x_ref, w_ref, g_ref, cos_ref, sin_ref, c_ref, kpe_ref):
    kv = jnp.dot(x_ref[...].astype(BF16), w_ref[...], preferred_element_type=F32)
    c_ref[...] = _rms(kv[:, :KV_LORA], g_ref[...])
    kp = kv[:, KV_LORA:KV_LORA + QK_ROPE]
    sw = jnp.concatenate([kp[:, QK_ROPE // 2:], kp[:, :QK_ROPE // 2]], axis=1)
    kpe_ref[...] = kp * cos_ref[...] + sw * sin_ref[...]


def mla_kv(x, w, g, cos2, sin2):
    T = x.shape[0]
    return pl.pallas_call(
        _mla_kv_kernel,
        grid=(T // ROW_TILE,),
        in_specs=[_row_spec(ROW_TILE, D_MODEL), _full_spec(w.shape), _full_spec(g.shape),
                  _row_spec(ROW_TILE, QK_ROPE), _row_spec(ROW_TILE, QK_ROPE)],
        out_specs=[_row_spec(ROW_TILE, KV_LORA), _row_spec(ROW_TILE, QK_ROPE)],
        out_shape=[jax.ShapeDtypeStruct((T, KV_LORA), F32), jax.ShapeDtypeStruct((T, QK_ROPE), F32)],
        compiler_params=_cparams(("parallel",)),
        name="mla_kv",
    )(x, w, g, cos2, sin2)


def _mla_expand_kernel(c_ref, kpe_ref, wk_ref, wv_ref, k_ref, v_ref):
    cb = c_ref[...].astype(BF16)
    kn = jnp.dot(cb, wk_ref[...], preferred_element_type=F32)
    v_ref[...] = jnp.dot(cb, wv_ref[...], preferred_element_type=F32).astype(BF16)
    kp = kpe_ref[...]
    kpad = jnp.concatenate([kp, jnp.zeros_like(kp)], axis=1).astype(BF16)
    for h in range(MLA_HEADS):
        k_ref[:, h * HEAD_SLOT:h * HEAD_SLOT + QK_NOPE] = kn[:, h * QK_NOPE:(h + 1) * QK_NOPE].astype(BF16)
        k_ref[:, h * HEAD_SLOT + QK_NOPE:(h + 1) * HEAD_SLOT] = kpad


def mla_expand(c, kpe, wk, wv):
    R = c.shape[0]
    return pl.pallas_call(
        _mla_expand_kernel,
        grid=(R // ROW_TILE,),
        in_specs=[_row_spec(ROW_TILE, KV_LORA), _row_spec(ROW_TILE, QK_ROPE), _full_spec(wk.shape), _full_spec(wv.shape)],
        out_specs=[_row_spec(ROW_TILE, MLA_HEADS * HEAD_SLOT), _row_spec(ROW_TILE, MLA_HEADS * V_HEAD)],
        out_shape=[jax.ShapeDtypeStruct((R, MLA_HEADS * HEAD_SLOT), BF16), jax.ShapeDtypeStruct((R, MLA_HEADS * V_HEAD), BF16)],
        compiler_params=_cparams(("parallel",)),
        name="mla_expand",
    )(c, kpe, wk, wv)


def _mla_q_kernel(x_ref, wdq_ref, g_ref, wuq_ref, cos_ref, sin_ref, q_ref):
    cq = _rms(jnp.dot(x_ref[...].astype(BF16), wdq_ref[...], preferred_element_type=F32), g_ref[...])
    q = jnp.dot(cq.astype(BF16), wuq_ref[...], preferred_element_type=F32)
    lane = lax.broadcasted_iota(jnp.int32, (ROW_TILE, 128), 1)
    cos = cos_ref[...]
    sin = sin_ref[...]
    for h in range(MLA_HEADS):
        q_ref[:, h * HEAD_SLOT:h * HEAD_SLOT + QK_NOPE] = q[:, h * HEAD_SLOT:h * HEAD_SLOT + QK_NOPE].astype(BF16)
        seg = q[:, h * HEAD_SLOT + QK_NOPE:(h + 1) * HEAD_SLOT]
        sw = jnp.where(lane < QK_ROPE // 2, pltpu.roll(seg, 128 - QK_ROPE // 2, 1),
                       jnp.where(lane < QK_ROPE, pltpu.roll(seg, QK_ROPE // 2, 1), 0.0))
        q_ref[:, h * HEAD_SLOT + QK_NOPE:(h + 1) * HEAD_SLOT] = (seg * cos + sw * sin).astype(BF16)


def mla_q(x, wdq, g, wuq, cosq, sinq):
    T = x.shape[0]
    return pl.pallas_call(
        _mla_q_kernel,
        grid=(T // ROW_TILE,),
        in_specs=[_row_spec(ROW_TILE, D_MODEL), _full_spec(wdq.shape), _full_spec(g.shape), _full_spec(wuq.shape),
                  _row_spec(ROW_TILE, 128), _row_spec(ROW_TILE, 128)],
        out_specs=_row_spec(ROW_TILE, MLA_HEADS * HEAD_SLOT),
        out_shape=jax.ShapeDtypeStruct((T, MLA_HEADS * HEAD_SLOT), BF16),
        compiler_params=_cparams(("parallel",)),
        name="mla_q",
    )(x, wdq, g, wuq, cosq, sinq)


def _attn_prompt_kernel(q_ref, k_ref, v_ref, o_ref):
    qi = pl.program_id(2)
    q = q_ref[...]
    nt = (((1,), (1,)), ((), ()))

    def step(ki, carry, diagonal):
        m, l, acc = carry
        off = pl.multiple_of(ki * ATTN_TILE, ATTN_TILE)
        s = lax.dot_general(q, k_ref[pl.ds(off, ATTN_TILE), :], nt, preferred_element_type=F32) * ATTN_SCALE
        if diagonal:
            qc = lax.broadcasted_iota(jnp.int32, s.shape, 0) // CHUNK
            kc = lax.broadcasted_iota(jnp.int32, s.shape, 1) // CHUNK
            s = jnp.where(kc <= qc, s, -jnp.inf)
        m_new = jnp.maximum(m, jnp.max(s, axis=1, keepdims=True))
        alpha = jnp.exp(m - m_new)
        p = jnp.exp(s - m_new)
        l = alpha * l + jnp.sum(p, axis=1, keepdims=True)
        acc = alpha * acc + jnp.dot(p.astype(BF16), v_ref[pl.ds(off, ATTN_TILE), :], preferred_element_type=F32)
        return m_new, l, acc

    init = (jnp.full((ATTN_TILE, 1), -jnp.inf, F32), jnp.zeros((ATTN_TILE, 1), F32), jnp.zeros((ATTN_TILE, V_HEAD), F32))
    carry = lax.fori_loop(0, qi, lambda ki, c: step(ki, c, False), init)
    m, l, acc = step(qi, carry, True)
    o_ref[...] = (acc / l).astype(BF16)


def attn_prompt(q, k, v, batch, seq):
    nq = seq // ATTN_TILE
    return pl.pallas_call(
        _attn_prompt_kernel,
        grid=(batch, MLA_HEADS, nq),
        in_specs=[pl.BlockSpec((ATTN_TILE, HEAD_SLOT), lambda b, h, i: (b * nq + i, h)),
                  pl.BlockSpec((seq, HEAD_SLOT), lambda b, h, i: (b, h)),
                  pl.BlockSpec((seq, V_HEAD), lambda b, h, i: (b, h))],
        out_specs=pl.BlockSpec((ATTN_TILE, V_HEAD), lambda b, h, i: (b * nq + i, h)),
        out_shape=jax.ShapeDtypeStruct((batch * seq, MLA_HEADS * V_HEAD), BF16),
        compiler_params=_cparams(("parallel", "parallel", "arbitrary")),
        name="attn_prompt",
    )(q, k, v)


def _attn_sample_kernel(q_ref, k_ref, v_ref, o_ref):
    nt = (((1,), (1,)), ((), ()))
    s = lax.dot_general(q_ref[...], k_ref[...], nt, preferred_element_type=F32) * ATTN_SCALE
    p = jnp.exp(s - jnp.max(s, axis=1, keepdims=True))
    o = jnp.dot(p.astype(BF16), v_ref[...], preferred_element_type=F32)
    o_ref[...] = (o / jnp.sum(p, axis=1, keepdims=True)).astype(BF16)


def attn_sample(q, k, v, batch, q_len, kv_len, q_block0):
    return pl.pallas_call(
        _attn_sample_kernel,
        grid=(batch, MLA_HEADS),
        in_specs=[pl.BlockSpec((q_len, HEAD_SLOT), lambda b, h: (q_block0 + b, h)),
                  pl.BlockSpec((kv_len, HEAD_SLOT), lambda b, h: (b, h)),
                  pl.BlockSpec((kv_len, V_HEAD), lambda b, h: (b, h))],
        out_specs=pl.BlockSpec((q_len, V_HEAD), lambda b, h: (b, h)),
        out_shape=jax.ShapeDtypeStruct((batch * q_len, MLA_HEADS * V_HEAD), BF16),
        compiler_params=_cparams(("parallel", "parallel")),
        name="attn_sample",
    )(q, k, v)


def _proj_ln_kernel(x_ref, o_ref_in, w_ref, g_ref, b_ref, y_ref):
    mix = jnp.dot(o_ref_in[...], w_ref[...], preferred_element_type=F32)
    y_ref[...] = _ln(ALPHA * x_ref[...] + mix, g_ref[...], b_ref[...])


def proj_ln(x, o, w, g, b):
    T = x.shape[0]
    return pl.pallas_call(
        _proj_ln_kernel,
        grid=(T // ROW_TILE,),
        in_specs=[_row_spec(ROW_TILE, D_MODEL), _row_spec(ROW_TILE, o.shape[1]), _full_spec(w.shape),
                  _full_spec(g.shape), _full_spec(b.shape)],
        out_specs=_row_spec(ROW_TILE, D_MODEL),
        out_shape=jax.ShapeDtypeStruct((T, D_MODEL), F32),
        compiler_params=_cparams(("parallel",)),
        name="proj_ln",
    )(x, o, w, g, b)


def _rope_tables(pos):
    inv = 1.0 / (ROPE_BASE ** (jnp.arange(0, QK_ROPE, 2, dtype=F32) / QK_ROPE))
    ang = pos.astype(F32)[:, None] * inv[None, :]
    cos, sin = jnp.cos(ang), jnp.sin(ang)
    cos2 = jnp.concatenate([cos, cos], axis=1)
    sin2 = jnp.concatenate([-sin, sin], axis=1)
    pad = jnp.zeros_like(cos2)
    return cos2, sin2, jnp.concatenate([cos2, pad], axis=1), jnp.concatenate([sin2, pad], axis=1)


def _gm_chunk_mats(w_s, b_s, dec_seq):
    i = jnp.arange(GM_CHUNK)
    mask = (i[None, :] // CHUNK) <= (i[:, None] // CHUNK)
    full = jnp.where(mask[None], w_s, 0.0)
    rep = GM_CHUNK // dec_seq
    blk = w_s[:, :dec_seq, :dec_seq]
    same = (i[:, None] // dec_seq) == (i[None, :] // dec_seq)
    diag = jnp.where(same[None], jnp.tile(blk, (1, rep, rep)), 0.0)
    ws = jnp.stack([full, diag], axis=0).astype(BF16)
    bias = jnp.stack([b_s, jnp.tile(b_s[:, :dec_seq], (1, rep))], axis=0)
    bs = jnp.broadcast_to(bias[..., None], bias.shape + (GM_GROUP_DIM,)).astype(F32)
    return ws, bs


def _pad_q_weight(w_uq):
    w = w_uq.reshape(Q_LORA, MLA_HEADS, QK_NOPE + QK_ROPE)
    w = jnp.pad(w, ((0, 0), (0, 0), (0, HEAD_SLOT - QK_NOPE - QK_ROPE)))
    return w.reshape(Q_LORA, MLA_HEADS * HEAD_SLOT).astype(BF16)


def kernel(x_prompt, x_sample, cache_ckv, cache_kpe, ln1_g, ln1_b, ln2_g, ln2_b, gm_w_in, gm_b_in, gm_ln_g, gm_ln_b,
           gm_w_s, gm_b_s, gm_w_out, mla_w_dkv, mla_kv_norm_g, mla_w_ukv, mla_w_dq, mla_q_norm_g, mla_w_uq, mla_w_o,
           peer_w_q, peer_subkeys, peer_u, peer_v):
    B, S, _ = x_prompt.shape
    DB, DS, _ = x_sample.shape
    PAST = cache_ckv.shape[1]
    TP, TS = B * S, DB * DS
    T = TP + TS
    assert TP % ROW_TILE == 0 and TS % ROW_TILE == 0 and T % PEER_TILE == 0 and S % ATTN_TILE == 0
    assert GM_CHUNK % DS == 0 and DS <= CHUNK and PAST % CHUNK == 0 and (PAST + DS) % 16 == 0

    x = jnp.concatenate([x_prompt.reshape(TP, D_MODEL), x_sample.reshape(TS, D_MODEL)], axis=0)
    pos = jnp.concatenate([jnp.tile(jnp.arange(S, dtype=jnp.int32), B),
                           jnp.tile(PAST + jnp.arange(DS, dtype=jnp.int32), DB)])
    cos2, sin2, cosq, sinq = _rope_tables(pos)
    spread, fold = peer_consts()
    row = lambda a: a.reshape(1, -1)

    def peer_block(x, l):
        out = peer(x, peer_w_q[l].astype(BF16), peer_subkeys[l].astype(BF16),
                   pack_table(peer_u[l]), pack_table(peer_v[l]), spread, fold)
        return resid_ln(x, out, row(ln2_g[l]), row(ln2_b[l]))

    gm_rows = []
    for l in range(N_A):
        u, v = gm_in(x, gm_w_in[l].astype(BF16), row(gm_b_in[l]), row(gm_ln_g[l]), row(gm_ln_b[l]))
        gm_rows.append(v[TP:].reshape(DB, DS, GM_HALF))
        ws, bs = _gm_chunk_mats(gm_w_s[l], gm_b_s[l], DS)
        x = gm_mix(x, u, v, ws, bs, gm_w_out[l].astype(BF16), row(ln1_g[l]), row(ln1_b[l]), TP // ROW_TILE)
        x = peer_block(x, l)

    w_dkv = jnp.pad(mla_w_dkv, ((0, 0), (0, 128 - QK_ROPE))).astype(BF16)
    c_new, kpe_new = mla_kv(x, w_dkv, row(mla_kv_norm_g), cos2, sin2)
    w_ukv = mla_w_ukv.reshape(KV_LORA, MLA_HEADS, QK_NOPE + V_HEAD)
    w_uk = w_ukv[:, :, :QK_NOPE].reshape(KV_LORA, MLA_HEADS * QK_NOPE).astype(BF16)
    w_uv = w_ukv[:, :, QK_NOPE:].reshape(KV_LORA, MLA_HEADS * V_HEAD).astype(BF16)
    kp_cat, vp = mla_expand(c_new[:TP], kpe_new[:TP], w_uk, w_uv)
    KV = PAST + DS
    c_all = jnp.concatenate([cache_ckv, c_new[TP:].reshape(DB, DS, KV_LORA)], axis=1).reshape(DB * KV, KV_LORA)
    kpe_all = jnp.concatenate([cache_kpe, kpe_new[TP:].reshape(DB, DS, QK_ROPE)], axis=1).reshape(DB * KV, QK_ROPE)
    ks_cat, vs = mla_expand(c_all, kpe_all, w_uk, w_uv)

    for j in range(DEPTH - N_A):
        l = N_A + j
        q = mla_q(x, mla_w_dq[j].astype(BF16), row(mla_q_norm_g[j]), _pad_q_weight(mla_w_uq[j]), cosq, sinq)
        o = jnp.concatenate([attn_prompt(q, kp_cat, vp, B, S),
                             attn_sample(q, ks_cat, vs, DB, DS, KV, TP // DS)], axis=0)
        x = proj_ln(x, o, mla_w_o[j].astype(BF16), row(ln1_g[l]), row(ln1_b[l]))
        x = peer_block(x, l)

    return (x[:TP].reshape(B, S, D_MODEL), x[TP:].reshape(DB, DS, D_MODEL), jnp.stack(gm_rows, axis=0),
            c_new[:TP].reshape(B, S, KV_LORA), kpe_new[:TP].reshape(B, S, QK_ROPE),
            c_new[TP:].reshape(DB, DS, KV_LORA), kpe_new[TP:].reshape(DB, DS, QK_ROPE))
```

```python
import math

import jax
import jax.numpy as jnp
from jax import lax
from jax.experimental import pallas as pl
from jax.experimental.pallas import tpu as pltpu

F32 = jnp.float32
BF16 = jnp.bfloat16

D_MODEL = 1024
DEPTH = 4
N_A = DEPTH // 2
CHUNK = 64
ALPHA = (2.0 * DEPTH) ** 0.25
LN_EPS = 1e-5
RMS_EPS = 1e-6

GM_CHUNK = 128
GM_HALF = 2 * D_MODEL
GM_GROUPS = 8
GM_GROUP_DIM = GM_HALF // GM_GROUPS

MLA_HEADS = 8
QK_NOPE = 128
QK_ROPE = 64
V_HEAD = 128
KV_LORA = D_MODEL // 4
Q_LORA = 3 * D_MODEL // 8
ROPE_BASE = 10000.0
ATTN_SCALE = (QK_NOPE + QK_ROPE) ** -0.5
HEAD_SLOT = 256

PEER_HEADS = 8
PEER_NKEYS = 128
PEER_EXPERTS = PEER_NKEYS * PEER_NKEYS
PEER_DK = 256
PEER_TOPK = 16
PEER_PICKS = PEER_HEADS * PEER_TOPK
TOPK_SHIFT = PEER_TOPK.bit_length() - 1
PAIR_SENTINEL = 1 << 20
ROW_WORDS = 4
D_CHUNKS = D_MODEL // 128

ROW_TILE = 256
PEER_TILE = 128
PACK_TILE = 512
U_TOKENS_PER_TRIP = 16
V_TOKENS_PER_TRIP = 8
ATTN_TILE = 512
VMEM_LIMIT = 56 * 1024 * 1024


def _cparams(sem):
    return pltpu.CompilerParams(dimension_semantics=sem, vmem_limit_bytes=VMEM_LIMIT)


def _gelu(x):
    return 0.5 * x * (1.0 + lax.erf(x * (1.0 / math.sqrt(2.0))))


def _ln(x, g, b):
    mu = jnp.mean(x, axis=-1, keepdims=True)
    xc = x - mu
    var = jnp.mean(xc * xc, axis=-1, keepdims=True)
    return xc * lax.rsqrt(var + LN_EPS) * g + b


def _rms(x, g):
    return x * lax.rsqrt(jnp.mean(x * x, axis=-1, keepdims=True) + RMS_EPS) * g


def _row_spec(tile, width):
    return pl.BlockSpec((tile, width), lambda i: (i, 0))


def _full_spec(shape):
    nd = len(shape)
    return pl.BlockSpec(shape, lambda i: (0,) * nd)


def _gm_in_kernel(x_ref, w_ref, b_ref, lg_ref, lb_ref, u_ref, v_ref):
    z = jnp.dot(x_ref[...].astype(BF16), w_ref[...], preferred_element_type=F32) + b_ref[...]
    z = _gelu(z)
    u_ref[...] = z[:, :GM_HALF]
    v_ref[...] = _ln(z[:, GM_HALF:], lg_ref[...], lb_ref[...])


def gm_in(x, w, b, lg, lb):
    T = x.shape[0]
    return pl.pallas_call(
        _gm_in_kernel,
        grid=(T // ROW_TILE,),
        in_specs=[_row_spec(ROW_TILE, D_MODEL), _full_spec(w.shape), _full_spec(b.shape),
                  _full_spec(lg.shape), _full_spec(lb.shape)],
        out_specs=[_row_spec(ROW_TILE, GM_HALF), _row_spec(ROW_TILE, GM_HALF)],
        out_shape=[jax.ShapeDtypeStruct((T, GM_HALF), F32)] * 2,
        compiler_params=_cparams(("parallel",)),
        name="gm_in",
    )(x, w, b, lg, lb)


def _gm_mix_kernel(x_ref, u_ref, v_ref, ws_ref, bs_ref, wo_ref, g_ref, b_ref, o_ref, s_scr):
    for c in range(ROW_TILE // GM_CHUNK):
        r = slice(c * GM_CHUNK, (c + 1) * GM_CHUNK)
        for g in range(GM_GROUPS):
            cs = slice(g * GM_GROUP_DIM, (g + 1) * GM_GROUP_DIM)
            sv = jnp.dot(ws_ref[0, g], v_ref[r, cs].astype(BF16), preferred_element_type=F32) + bs_ref[0, g]
            s_scr[r, cs] = (u_ref[r, cs] * sv).astype(BF16)
    mix = jnp.dot(s_scr[...], wo_ref[...], preferred_element_type=F32)
    o_ref[...] = _ln(ALPHA * x_ref[...] + mix, g_ref[...], b_ref[...])


def gm_mix(x, u, v, ws, bs, wo, g, b, n_prompt_tiles):
    T = x.shape[0]
    sel = lambda i: (jnp.where(i >= n_prompt_tiles, 1, 0), 0, 0, 0)
    return pl.pallas_call(
        _gm_mix_kernel,
        grid=(T // ROW_TILE,),
        in_specs=[_row_spec(ROW_TILE, D_MODEL), _row_spec(ROW_TILE, GM_HALF), _row_spec(ROW_TILE, GM_HALF),
                  pl.BlockSpec((1,) + ws.shape[1:], sel), pl.BlockSpec((1,) + bs.shape[1:], sel),
                  _full_spec(wo.shape), _full_spec(g.shape), _full_spec(b.shape)],
        out_specs=_row_spec(ROW_TILE, D_MODEL),
        out_shape=jax.ShapeDtypeStruct((T, D_MODEL), F32),
        scratch_shapes=[pltpu.VMEM((ROW_TILE, GM_HALF), BF16)],
        compiler_params=_cparams(("parallel",)),
        name="gm_mix",
    )(x, u, v, ws, bs, wo, g, b)


def _extract_max(s, codes, sentinel):
    m = jnp.max(s, axis=0, keepdims=True)
    c = jnp.min(jnp.where(s == m, codes, sentinel), axis=0, keepdims=True)
    return m, c, jnp.where(codes == c, -jnp.inf, s)


def _topk_rows(s):
    rows = lax.broadcasted_iota(jnp.int32, s.shape, 0).astype(F32)
    vals, idxs = [], []
    for _ in range(PEER_TOPK):
        m, idx, s = _extract_max(s, rows, float(s.shape[0]))
        vals.append(m)
        idxs.append(idx)
    return jnp.concatenate(vals, axis=0), jnp.concatenate(idxs, axis=0).astype(jnp.int32)


_PAIR_GROUPS = ((None, 0, 16), (None, 1, 8), (0, None, 16), (1, None, 8), (2, None, 8), (3, None, 8), (4, None, 8))


def pair_codes():
    codes = []
    for b_fix, a_fix, n in _PAIR_GROUPS:
        for r in range(n):
            a, b = (a_fix, r) if b_fix is None else (r, b_fix)
            first = a_fix is not None or a >= 2
            ok = first and (a + 1) * (b + 1) <= PEER_TOPK
            codes.append(a * PEER_TOPK + b if ok else PAIR_SENTINEL)
    assert sorted(c for c in codes if c != PAIR_SENTINEL) == sorted(
        a * PEER_TOPK + b for a in range(PEER_TOPK) for b in range(PEER_TOPK) if (a + 1) * (b + 1) <= PEER_TOPK)
    return jnp.broadcast_to(jnp.asarray(codes, F32)[:, None], (len(codes), PEER_TILE))


def _topk_pairs(v1, i1, v2, i2, codes):
    parts = []
    for b_fix, a_fix, n in _PAIR_GROUPS:
        if b_fix is None:
            parts.append(v1[a_fix:a_fix + 1] + v2[0:n])
        else:
            parts.append(v1[0:n] + v2[b_fix:b_fix + 1])
    cand = jnp.where(codes < PAIR_SENTINEL, jnp.concatenate(parts, axis=0), -jnp.inf)
    krow = lax.broadcasted_iota(jnp.int32, v1.shape, 0)
    vals, k1, k2 = [], [], []
    for _ in range(PEER_TOPK):
        m, c, cand = _extract_max(cand, codes, float(PAIR_SENTINEL))
        vals.append(m)
        c = c.astype(jnp.int32)
        k1.append(jnp.sum(jnp.where(krow == (c >> TOPK_SHIFT), i1, 0), axis=0, keepdims=True))
        k2.append(jnp.sum(jnp.where(krow == (c & (PEER_TOPK - 1)), i2, 0), axis=0, keepdims=True))
    return jnp.concatenate(vals, axis=0), jnp.concatenate(k1, axis=0), jnp.concatenate(k2, axis=0)


def _peer_topk_kernel(x_ref, wq_ref, k_ref, codes_ref, eid_ref, gate_ref):
    q = jnp.dot(x_ref[...].astype(BF16), wq_ref[...], preferred_element_type=F32)
    half = PEER_DK // 2
    nt = (((1,), (1,)), ((), ()))
    codes = codes_ref[...]
    eids, gates = [], []
    for h in range(PEER_HEADS):
        q1 = q[:, h * PEER_DK:h * PEER_DK + half].astype(BF16)
        q2 = q[:, h * PEER_DK + half:(h + 1) * PEER_DK].astype(BF16)
        s1 = lax.dot_general(k_ref[0], q1, nt, preferred_element_type=F32)
        s2 = lax.dot_general(k_ref[1], q2, nt, preferred_element_type=F32)
        v1, i1 = _topk_rows(s1)
        v2, i2 = _topk_rows(s2)
        sc, e1, e2 = _topk_pairs(v1, i1, v2, i2, codes)
        e = jnp.exp(sc - sc[0:1])
        gates.append(e / jnp.sum(e, axis=0, keepdims=True))
        eids.append((e1 * PEER_NKEYS + e2) * ROW_WORDS)
    eid_ref[...] = jnp.transpose(jnp.concatenate(eids, axis=0))
    gate_ref[...] = jnp.transpose(jnp.concatenate(gates, axis=0))


def peer_topk(x, wq, keys, codes):
    T = x.shape[0]
    return pl.pallas_call(
        _peer_topk_kernel,
        grid=(T // PEER_TILE,),
        in_specs=[_row_spec(PEER_TILE, D_MODEL), _full_spec(wq.shape), _full_spec(keys.shape), _full_spec(codes.shape)],
        out_specs=[_row_spec(PEER_TILE, PEER_PICKS), _row_spec(PEER_TILE, PEER_PICKS)],
        out_shape=[jax.ShapeDtypeStruct((T, PEER_PICKS), jnp.int32), jax.ShapeDtypeStruct((T, PEER_PICKS), F32)],
        compiler_params=_cparams(("parallel",)),
        name="peer_topk",
    )(x, wq, keys, codes)


def _gather_rows(eid_ref, t, tab_ref, dst_ref):
    picks = eid_ref.at[t]
    for p in range(PEER_PICKS):
        row = pl.multiple_of(picks[p], ROW_WORDS)
        dst_ref[pl.ds(ROW_WORDS * p, ROW_WORDS), :] = tab_ref[pl.ds(row, ROW_WORDS), :]


def _chunk_mask():
    sub = lax.broadcasted_iota(jnp.int32, (D_CHUNKS, PEER_PICKS * D_CHUNKS), 0)
    col = lax.broadcasted_iota(jnp.int32, (D_CHUNKS, PEER_PICKS * D_CHUNKS), 1)
    return ((col & (D_CHUNKS - 1)) == sub).astype(F32)


def _for_tokens_pipelined(eid_ref, tab_ref, bufs, compute, tokens_per_trip):
    _gather_rows(eid_ref, 0, tab_ref, bufs[0])

    def body(i, carry):
        t0 = tokens_per_trip * i
        for k in range(tokens_per_trip):
            _gather_rows(eid_ref, jnp.minimum(t0 + k + 1, PEER_TILE - 1), tab_ref, bufs[(k + 1) % 2])
            compute(t0 + k, bufs[k % 2])
        return carry

    lax.fori_loop(0, PEER_TILE // tokens_per_trip, body, 0)


def _peer_u_kernel(eid_ref, x_ref, tab_ref, fold_ref, h_ref, hrep, *bufs):
    mask = _chunk_mask()
    nt = (((1,), (1,)), ((), ()))

    def compute(t, rows_ref):
        u = pltpu.bitcast(rows_ref[...], BF16)
        xr = x_ref[pl.ds(t, 1), :]
        x8 = jnp.concatenate([xr[:, j * 128:(j + 1) * 128] for j in range(D_CHUNKS)], axis=0)
        hi = x8.astype(BF16)
        lo = (x8 - hi.astype(F32)).astype(BF16)
        o = lax.dot_general(jnp.concatenate([hi, lo], axis=0), u, nt, preferred_element_type=F32)
        o8 = (o[0:D_CHUNKS] + o[D_CHUNKS:2 * D_CHUNKS]) * mask
        hrep[pl.ds(t, 1), :] = jnp.sum(o8, axis=0, keepdims=True)

    _for_tokens_pipelined(eid_ref, tab_ref, bufs, compute, U_TOKENS_PER_TRIP)
    h_ref[...] = jnp.dot(hrep[...], fold_ref[...], precision=lax.Precision.HIGHEST, preferred_element_type=F32)


def _peer_v_kernel(eid_ref, h_ref, gate_ref, x_ref, tab_ref, spread_ref, g_ref, b_ref, y_ref, a_hi, a_lo, mix, *bufs):
    a = gate_ref[...] * _gelu(h_ref[...])
    hi = a.astype(BF16)
    lo = (a - hi.astype(F32)).astype(BF16)
    a_hi[...] = jnp.dot(hi, spread_ref[...], preferred_element_type=F32)
    a_lo[...] = jnp.dot(lo, spread_ref[...], preferred_element_type=F32)
    mask = _chunk_mask()

    def compute(t, rows_ref):
        v = pltpu.bitcast(rows_ref[...], BF16)
        lhs = jnp.concatenate([(a_hi[pl.ds(t, 1), :] * mask).astype(BF16),
                               (a_lo[pl.ds(t, 1), :] * mask).astype(BF16)], axis=0)
        o = jnp.dot(lhs, v, preferred_element_type=F32)
        o8 = o[0:D_CHUNKS] + o[D_CHUNKS:2 * D_CHUNKS]
        mix[pl.ds(t, 1), :] = jnp.concatenate([o8[j:j + 1] for j in range(D_CHUNKS)], axis=1)

    _for_tokens_pipelined(eid_ref, tab_ref, bufs, compute, V_TOKENS_PER_TRIP)
    y_ref[...] = _ln(ALPHA * x_ref[...] + mix[...], g_ref[...], b_ref[...])


def _peer_specs():
    eid_spec = pl.BlockSpec((PEER_TILE, PEER_PICKS), lambda i: (i, 0), memory_space=pltpu.SMEM)
    tab_spec = pl.BlockSpec((PEER_EXPERTS * ROW_WORDS, 128), lambda i: (0, 0), pipeline_mode=pl.Buffered(1))
    bufs = [pltpu.VMEM((PEER_PICKS * ROW_WORDS, 128), jnp.int32)] * 2
    return eid_spec, tab_spec, bufs


def peer_u(eid, x, tab, fold):
    T = eid.shape[0]
    eid_spec, tab_spec, bufs = _peer_specs()
    return pl.pallas_call(
        _peer_u_kernel,
        grid=(T // PEER_TILE,),
        in_specs=[eid_spec, _row_spec(PEER_TILE, D_MODEL), tab_spec, _full_spec(fold.shape)],
        out_specs=_row_spec(PEER_TILE, PEER_PICKS),
        out_shape=jax.ShapeDtypeStruct((T, PEER_PICKS), F32),
        scratch_shapes=[pltpu.VMEM((PEER_TILE, PEER_PICKS * D_CHUNKS), F32)] + bufs,
        compiler_params=_cparams(("arbitrary",)),
        name="peer_u",
    )(eid, x, tab, fold)


def peer_v(eid, hdn, gate, x, tab, spread, g, b):
    T = eid.shape[0]
    eid_spec, tab_spec, bufs = _peer_specs()
    return pl.pallas_call(
        _peer_v_kernel,
        grid=(T // PEER_TILE,),
        in_specs=[eid_spec, _row_spec(PEER_TILE, PEER_PICKS), _row_spec(PEER_TILE, PEER_PICKS),
                  _row_spec(PEER_TILE, D_MODEL), tab_spec, _full_spec(spread.shape), _full_spec(g.shape), _full_spec(b.shape)],
        out_specs=_row_spec(PEER_TILE, D_MODEL),
        out_shape=jax.ShapeDtypeStruct((T, D_MODEL), F32),
        scratch_shapes=[pltpu.VMEM((PEER_TILE, PEER_PICKS * D_CHUNKS), F32)] * 2 + [pltpu.VMEM((PEER_TILE, D_MODEL), F32)] + bufs,
        compiler_params=_cparams(("arbitrary",)),
        name="peer_v",
    )(eid, hdn, gate, x, tab, spread, g, b)


def _pack_kernel(t_ref, o_ref):
    n = t_ref.shape[0]
    for s in range(ROW_WORDS):
        lo = t_ref[:, (2 * s) * 128:(2 * s + 1) * 128].astype(BF16).astype(F32)
        hi = t_ref[:, (2 * s + 1) * 128:(2 * s + 2) * 128].astype(BF16).astype(F32)
        word = pltpu.bitcast(hi, jnp.uint32) | (pltpu.bitcast(lo, jnp.uint32) >> 16)
        o_ref[pl.ds(s, n, stride=ROW_WORDS), :] = pltpu.bitcast(word, jnp.int32)


def pack_table(tab):
    e = tab.shape[0]
    return pl.pallas_call(
        _pack_kernel,
        grid=(e // PACK_TILE,),
        in_specs=[_row_spec(PACK_TILE, D_MODEL)],
        out_specs=_row_spec(PACK_TILE * ROW_WORDS, 128),
        out_shape=jax.ShapeDtypeStruct((e * ROW_WORDS, 128), jnp.int32),
        compiler_params=_cparams(("parallel",)),
        name="pack_table",
    )(tab)


def peer_consts():
    col = jnp.arange(PEER_PICKS * D_CHUNKS) // D_CHUNKS
    spread = (col[None, :] == jnp.arange(PEER_PICKS)[:, None])
    return spread.astype(BF16), jnp.transpose(spread).astype(F32)


def peer_block(x, wq, keys, u_packed, v_packed, spread, fold, g, b):
    eid, gate = peer_topk(x, wq, keys, pair_codes())
    hdn = peer_u(eid, x, u_packed, fold)
    return peer_v(eid, hdn, gate, x, v_packed, spread, g, b)


def _mla_kv_kernel(x_ref, w_ref, g_ref, cos_ref, sin_ref, c_ref, kpe_ref):
    kv = jnp.dot(x_ref[...].astype(BF16), w_ref[...], preferred_element_type=F32)
    c_ref[...] = _rms(kv[:, :KV_LORA], g_ref[...])
    kp = kv[:, KV_LORA:KV_LORA + QK_ROPE]
    sw = jnp.concatenate([kp[:, QK_ROPE // 2:], kp[:, :QK_ROPE // 2]], axis=1)
    kpe_ref[...] = kp * cos_ref[...] + sw * sin_ref[...]


def mla_kv(x, w, g, cos2, sin2):
    T = x.shape[0]
    return pl.pallas_call(
        _mla_kv_kernel,
        grid=(T // ROW_TILE,),
        in_specs=[_row_spec(ROW_TILE, D_MODEL), _full_spec(w.shape), _full_spec(g.shape),
                  _row_spec(ROW_TILE, QK_ROPE), _row_spec(ROW_TILE, QK_ROPE)],
        out_specs=[_row_spec(ROW_TILE, KV_LORA), _row_spec(ROW_TILE, QK_ROPE)],
        out_shape=[jax.ShapeDtypeStruct((T, KV_LORA), F32), jax.ShapeDtypeStruct((T, QK_ROPE), F32)],
        compiler_params=_cparams(("parallel",)),
        name="mla_kv",
    )(x, w, g, cos2, sin2)


def _mla_expand_kernel(c_ref, kpe_ref, wk_ref, wv_ref, k_ref, v_ref):
    cb = c_ref[...].astype(BF16)
    kn = jnp.dot(cb, wk_ref[...], preferred_element_type=F32)
    v_ref[...] = jnp.dot(cb, wv_ref[...], preferred_element_type=F32).astype(BF16)
    kp = kpe_ref[...]
    kpad = jnp.concatenate([kp, jnp.zeros_like(kp)], axis=1).astype(BF16)
    for h in range(MLA_HEADS):
        k_ref[:, h * HEAD_SLOT:h * HEAD_SLOT + QK_NOPE] = kn[:, h * QK_NOPE:(h + 1) * QK_NOPE].astype(BF16)
        k_ref[:, h * HEAD_SLOT + QK_NOPE:(h + 1) * HEAD_SLOT] = kpad


def mla_expand(c, kpe, wk, wv):
    R = c.shape[0]
    return pl.pallas_call(
        _mla_expand_kernel,
        grid=(R // ROW_TILE,),
        in_specs=[_row_spec(ROW_TILE, KV_LORA), _row_spec(ROW_TILE, QK_ROPE), _full_spec(wk.shape), _full_spec(wv.shape)],
        out_specs=[_row_spec(ROW_TILE, MLA_HEADS * HEAD_SLOT), _row_spec(ROW_TILE, MLA_HEADS * V_HEAD)],
        out_shape=[jax.ShapeDtypeStruct((R, MLA_HEADS * HEAD_SLOT), BF16), jax.ShapeDtypeStruct((R, MLA_HEADS * V_HEAD), BF16)],
        compiler_params=_cparams(("parallel",)),
        name="mla_expand",
    )(c, kpe, wk, wv)


def _mla_q_kernel(x_ref, wdq_ref, g_ref, wuq_ref, cos_ref, sin_ref, q_ref):
    cq = _rms(jnp.dot(x_ref[...].astype(BF16), wdq_ref[...], preferred_element_type=F32), g_ref[...])
    q = jnp.dot(cq.astype(BF16), wuq_ref[...], preferred_element_type=F32)
    q = q * (ATTN_SCALE * math.log2(math.e))
    lane = lax.broadcasted_iota(jnp.int32, (ROW_TILE, 128), 1)
    cos = cos_ref[...]
    sin = sin_ref[...]
    for h in range(MLA_HEADS):
        q_ref[:, h * HEAD_SLOT:h * HEAD_SLOT + QK_NOPE] = q[:, h * HEAD_SLOT:h * HEAD_SLOT + QK_NOPE].astype(BF16)
        seg = q[:, h * HEAD_SLOT + QK_NOPE:(h + 1) * HEAD_SLOT]
        sw = jnp.where(lane < QK_ROPE // 2, pltpu.roll(seg, 128 - QK_ROPE // 2, 1),
                       jnp.where(lane < QK_ROPE, pltpu.roll(seg, QK_ROPE // 2, 1), 0.0))
        q_ref[:, h * HEAD_SLOT + QK_NOPE:(h + 1) * HEAD_SLOT] = (seg * cos + sw * sin).astype(BF16)


def mla_q(x, wdq, g, wuq, cosq, sinq):
    T = x.shape[0]
    return pl.pallas_call(
        _mla_q_kernel,
        grid=(T // ROW_TILE,),
        in_specs=[_row_spec(ROW_TILE, D_MODEL), _full_spec(wdq.shape), _full_spec(g.shape), _full_spec(wuq.shape),
                  _row_spec(ROW_TILE, 128), _row_spec(ROW_TILE, 128)],
        out_specs=_row_spec(ROW_TILE, MLA_HEADS * HEAD_SLOT),
        out_shape=jax.ShapeDtypeStruct((T, MLA_HEADS * HEAD_SLOT), BF16),
        compiler_params=_cparams(("parallel",)),
        name="mla_q",
    )(x, wdq, g, wuq, cosq, sinq)


def _attn_prompt_kernel(q_ref, k_ref, v_ref, o_ref, s_even, s_odd):
    qi = pl.program_id(2)
    q = q_ref[...]
    nt = (((1,), (1,)), ((), ()))

    def scores(ki, dst):
        off = pl.multiple_of(ki * ATTN_TILE, ATTN_TILE)
        dst[...] = lax.dot_general(q, k_ref[pl.ds(off, ATTN_TILE), :], nt, preferred_element_type=F32)

    def update(ki, src, carry, diagonal):
        m, l, acc = carry
        off = pl.multiple_of(ki * ATTN_TILE, ATTN_TILE)
        s = src[...]
        if diagonal:
            qc = lax.broadcasted_iota(jnp.int32, s.shape, 0) // CHUNK
            kc = lax.broadcasted_iota(jnp.int32, s.shape, 1) // CHUNK
            s = jnp.where(kc <= qc, s, -jnp.inf)
        m_new = jnp.maximum(m, jnp.max(s, axis=1, keepdims=True))
        alpha = jnp.exp2(m - m_new)
        p = jnp.exp2(s - m_new)
        l = alpha * l + jnp.sum(p, axis=1, keepdims=True)
        acc = alpha * acc + jnp.dot(p.astype(BF16), v_ref[pl.ds(off, ATTN_TILE), :], preferred_element_type=F32)
        return m_new, l, acc

    init = (jnp.full((ATTN_TILE, 1), -jnp.inf, F32), jnp.zeros((ATTN_TILE, 1), F32), jnp.zeros((ATTN_TILE, V_HEAD), F32))
    scores(0, s_even)

    def pair(j, carry):
        scores(2 * j + 1, s_odd)
        carry = update(2 * j, s_even, carry, False)
        scores(jnp.minimum(2 * j + 2, qi), s_even)
        return update(2 * j + 1, s_odd, carry, False)

    carry = lax.fori_loop(0, qi // 2, pair, init)

    def last_even(carry):
        return update(qi, s_even, carry, True)

    def last_odd(carry):
        scores(qi, s_odd)
        carry = update(qi - 1, s_even, carry, False)
        return update(qi, s_odd, carry, True)

    m, l, acc = lax.cond(qi % 2 == 0, last_even, last_odd, carry)
    o_ref[...] = (acc / l).astype(BF16)


def attn_prompt(q, k, v, batch, seq):
    nq = seq // ATTN_TILE
    return pl.pallas_call(
        _attn_prompt_kernel,
        grid=(batch, MLA_HEADS, nq),
        in_specs=[pl.BlockSpec((ATTN_TILE, HEAD_SLOT), lambda b, h, i: (b * nq + i, h)),
                  pl.BlockSpec((seq, HEAD_SLOT), lambda b, h, i: (b, h)),
                  pl.BlockSpec((seq, V_HEAD), lambda b, h, i: (b, h))],
        out_specs=pl.BlockSpec((ATTN_TILE, V_HEAD), lambda b, h, i: (b * nq + i, h)),
        out_shape=jax.ShapeDtypeStruct((batch * seq, MLA_HEADS * V_HEAD), BF16),
        scratch_shapes=[pltpu.VMEM((ATTN_TILE, ATTN_TILE), F32)] * 2,
        compiler_params=_cparams(("parallel", "parallel", "arbitrary")),
        name="attn_prompt",
    )(q, k, v)


def _attn_sample_kernel(q_ref, k_ref, v_ref, o_ref):
    nt = (((1,), (1,)), ((), ()))
    s = lax.dot_general(q_ref[...], k_ref[...], nt, preferred_element_type=F32)
    p = jnp.exp2(s - jnp.max(s, axis=1, keepdims=True))
    o = jnp.dot(p.astype(BF16), v_ref[...], preferred_element_type=F32)
    o_ref[...] = (o / jnp.sum(p, axis=1, keepdims=True)).astype(BF16)


def attn_sample(q, k, v, batch, q_len, kv_len, q_block0):
    return pl.pallas_call(
        _attn_sample_kernel,
        grid=(batch, MLA_HEADS),
        in_specs=[pl.BlockSpec((q_len, HEAD_SLOT), lambda b, h: (q_block0 + b, h)),
                  pl.BlockSpec((kv_len, HEAD_SLOT), lambda b, h: (b, h)),
                  pl.BlockSpec((kv_len, V_HEAD), lambda b, h: (b, h))],
        out_specs=pl.BlockSpec((q_len, V_HEAD), lambda b, h: (b, h)),
        out_shape=jax.ShapeDtypeStruct((batch * q_len, MLA_HEADS * V_HEAD), BF16),
        compiler_params=_cparams(("parallel", "parallel")),
        name="attn_sample",
    )(q, k, v)


def _proj_ln_kernel(x_ref, o_ref_in, w_ref, g_ref, b_ref, y_ref):
    mix = jnp.dot(o_ref_in[...], w_ref[...], preferred_element_type=F32)
    y_ref[...] = _ln(ALPHA * x_ref[...] + mix, g_ref[...], b_ref[...])


def proj_ln(x, o, w, g, b):
    T = x.shape[0]
    return pl.pallas_call(
        _proj_ln_kernel,
        grid=(T // ROW_TILE,),
        in_specs=[_row_spec(ROW_TILE, D_MODEL), _row_spec(ROW_TILE, o.shape[1]), _full_spec(w.shape),
                  _full_spec(g.shape), _full_spec(b.shape)],
        out_specs=_row_spec(ROW_TILE, D_MODEL),
        out_shape=jax.ShapeDtypeStruct((T, D_MODEL), F32),
        compiler_params=_cparams(("parallel",)),
        name="proj_ln",
    )(x, o, w, g, b)


def _rope_tables(pos):
    inv = 1.0 / (ROPE_BASE ** (jnp.arange(0, QK_ROPE, 2, dtype=F32) / QK_ROPE))
    ang = pos.astype(F32)[:, None] * inv[None, :]
    cos, sin = jnp.cos(ang), jnp.sin(ang)
    cos2 = jnp.concatenate([cos, cos], axis=1)
    sin2 = jnp.concatenate([-sin, sin], axis=1)
    pad = jnp.zeros_like(cos2)
    return cos2, sin2, jnp.concatenate([cos2, pad], axis=1), jnp.concatenate([sin2, pad], axis=1)


def _gm_chunk_mats(w_s, b_s, dec_seq):
    i = jnp.arange(GM_CHUNK)
    mask = (i[None, :] // CHUNK) <= (i[:, None] // CHUNK)
    full = jnp.where(mask[None], w_s, 0.0)
    rep = GM_CHUNK // dec_seq
    blk = w_s[:, :dec_seq, :dec_seq]
    same = (i[:, None] // dec_seq) == (i[None, :] // dec_seq)
    diag = jnp.where(same[None], jnp.tile(blk, (1, rep, rep)), 0.0)
    ws = jnp.stack([full, diag], axis=0).astype(BF16)
    bias = jnp.stack([b_s, jnp.tile(b_s[:, :dec_seq], (1, rep))], axis=0)
    bs = jnp.broadcast_to(bias[..., None], bias.shape + (GM_GROUP_DIM,)).astype(F32)
    return ws, bs


def _pad_q_weight(w_uq):
    w = w_uq.reshape(Q_LORA, MLA_HEADS, QK_NOPE + QK_ROPE)
    w = jnp.pad(w, ((0, 0), (0, 0), (0, HEAD_SLOT - QK_NOPE - QK_ROPE)))
    return w.reshape(Q_LORA, MLA_HEADS * HEAD_SLOT).astype(BF16)


def kernel(x_prompt, x_sample, cache_ckv, cache_kpe, ln1_g, ln1_b, ln2_g, ln2_b, gm_w_in, gm_b_in, gm_ln_g, gm_ln_b,
           gm_w_s, gm_b_s, gm_w_out, mla_w_dkv, mla_kv_norm_g, mla_w_ukv, mla_w_dq, mla_q_norm_g, mla_w_uq, mla_w_o,
           peer_w_q, peer_subkeys, peer_u, peer_v):
    B, S, _ = x_prompt.shape
    DB, DS, _ = x_sample.shape
    PAST = cache_ckv.shape[1]
    TP, TS = B * S, DB * DS
    T = TP + TS
    assert TP % ROW_TILE == 0 and TS % ROW_TILE == 0 and T % PEER_TILE == 0 and S % ATTN_TILE == 0
    assert GM_CHUNK % DS == 0 and DS <= CHUNK and PAST % CHUNK == 0 and (PAST + DS) % 16 == 0

    x = jnp.concatenate([x_prompt.reshape(TP, D_MODEL), x_sample.reshape(TS, D_MODEL)], axis=0)
    pos = jnp.concatenate([jnp.tile(jnp.arange(S, dtype=jnp.int32), B),
                           jnp.tile(PAST + jnp.arange(DS, dtype=jnp.int32), DB)])
    cos2, sin2, cosq, sinq = _rope_tables(pos)
    spread, fold = peer_consts()
    row = lambda a: a.reshape(1, -1)

    def peer_layer(x, l):
        return peer_block(x, peer_w_q[l].astype(BF16), peer_subkeys[l].astype(BF16), pack_table(peer_u[l]),
                          pack_table(peer_v[l]), spread, fold, row(ln2_g[l]), row(ln2_b[l]))

    gm_rows = []
    for l in range(N_A):
        u, v = gm_in(x, gm_w_in[l].astype(BF16), row(gm_b_in[l]), row(gm_ln_g[l]), row(gm_ln_b[l]))
        gm_rows.append(v[TP:].reshape(DB, DS, GM_HALF))
        ws, bs = _gm_chunk_mats(gm_w_s[l], gm_b_s[l], DS)
        x = gm_mix(x, u, v, ws, bs, gm_w_out[l].astype(BF16), row(ln1_g[l]), row(ln1_b[l]), TP // ROW_TILE)
        x = peer_layer(x, l)

    w_dkv = jnp.pad(mla_w_dkv, ((0, 0), (0, 128 - QK_ROPE))).astype(BF16)
    c_new, kpe_new = mla_kv(x, w_dkv, row(mla_kv_norm_g), cos2, sin2)
    w_ukv = mla_w_ukv.reshape(KV_LORA, MLA_HEADS, QK_NOPE + V_HEAD)
    w_uk = w_ukv[:, :, :QK_NOPE].reshape(KV_LORA, MLA_HEADS * QK_NOPE).astype(BF16)
    w_uv = w_ukv[:, :, QK_NOPE:].reshape(KV_LORA, MLA_HEADS * V_HEAD).astype(BF16)
    kp_cat, vp = mla_expand(c_new[:TP], kpe_new[:TP], w_uk, w_uv)
    KV = PAST + DS
    c_all = jnp.concatenate([cache_ckv, c_new[TP:].reshape(DB, DS, KV_LORA)], axis=1).reshape(DB * KV, KV_LORA)
    kpe_all = jnp.concatenate([cache_kpe, kpe_new[TP:].reshape(DB, DS, QK_ROPE)], axis=1).reshape(DB * KV, QK_ROPE)
    ks_cat, vs = mla_expand(c_all, kpe_all, w_uk, w_uv)

    for j in range(DEPTH - N_A):
        l = N_A + j
        q = mla_q(x, mla_w_dq[j].astype(BF16), row(mla_q_norm_g[j]), _pad_q_weight(mla_w_uq[j]), cosq, sinq)
        o = jnp.concatenate([attn_prompt(q, kp_cat, vp, B, S),
                             attn_sample(q, ks_cat, vs, DB, DS, KV, TP // DS)], axis=0)
        x = proj_ln(x, o, mla_w_o[j].astype(BF16), row(ln1_g[l]), row(ln1_b[l]))
        x = peer_layer(x, l)

    return (x[:TP].reshape(B, S, D_MODEL), x[TP:].reshape(DB, DS, D_MODEL), jnp.stack(gm_rows, axis=0),
            c_new[:TP].reshape(B, S, KV_LORA), kpe_new[:TP].reshape(B, S, QK_ROPE),
            c_new[TP:].reshape(DB, DS, KV_LORA), kpe_new[TP:].reshape(DB, DS, QK_ROPE))
```

```python
import math

import jax
import jax.numpy as jnp
from jax import lax
from jax.experimental import pallas as pl
from jax.experimental.pallas import tpu as pltpu

F32 = jnp.float32
BF16 = jnp.bfloat16

D_MODEL = 1024
DEPTH = 4
N_A = DEPTH // 2
CHUNK = 64
ALPHA = (2.0 * DEPTH) ** 0.25
LN_EPS = 1e-5
RMS_EPS = 1e-6

GM_CHUNK = 128
GM_HALF = 2 * D_MODEL
GM_GROUPS = 8
GM_GROUP_DIM = GM_HALF // GM_GROUPS

MLA_HEADS = 8
QK_NOPE = 128
QK_ROPE = 64
V_HEAD = 128
KV_LORA = D_MODEL // 4
Q_LORA = 3 * D_MODEL // 8
ROPE_BASE = 10000.0
ATTN_SCALE = (QK_NOPE + QK_ROPE) ** -0.5
HEAD_SLOT = 256

PEER_HEADS = 8
PEER_NKEYS = 128
PEER_EXPERTS = PEER_NKEYS * PEER_NKEYS
PEER_DK = 256
PEER_TOPK = 16
PEER_PICKS = PEER_HEADS * PEER_TOPK
TOPK_SHIFT = PEER_TOPK.bit_length() - 1
PAIR_SENTINEL = 1 << 20
ROW_WORDS = 4
D_CHUNKS = D_MODEL // 128

ROW_TILE = 256
PEER_TILE = 128
PACK_TILE = 512
V_TOKENS_PER_TRIP = 8
ATTN_TILE = 512
VMEM_LIMIT = 56 * 1024 * 1024


def _cparams(sem):
    return pltpu.CompilerParams(dimension_semantics=sem, vmem_limit_bytes=VMEM_LIMIT)


def _gelu(x):
    return 0.5 * x * (1.0 + lax.erf(x * (1.0 / math.sqrt(2.0))))


def _ln(x, g, b):
    mu = jnp.mean(x, axis=-1, keepdims=True)
    xc = x - mu
    var = jnp.mean(xc * xc, axis=-1, keepdims=True)
    return xc * lax.rsqrt(var + LN_EPS) * g + b


def _rms(x, g):
    return x * lax.rsqrt(jnp.mean(x * x, axis=-1, keepdims=True) + RMS_EPS) * g


def _row_spec(tile, width):
    return pl.BlockSpec((tile, width), lambda i: (i, 0))


def _full_spec(shape):
    nd = len(shape)
    return pl.BlockSpec(shape, lambda i: (0,) * nd)


def _gm_in_kernel(x_ref, w_ref, b_ref, lg_ref, lb_ref, u_ref, v_ref):
    z = jnp.dot(x_ref[...].astype(BF16), w_ref[...], preferred_element_type=F32) + b_ref[...]
    z = _gelu(z)
    u_ref[...] = z[:, :GM_HALF]
    v_ref[...] = _ln(z[:, GM_HALF:], lg_ref[...], lb_ref[...])


def gm_in(x, w, b, lg, lb):
    T = x.shape[0]
    return pl.pallas_call(
        _gm_in_kernel,
        grid=(T // ROW_TILE,),
        in_specs=[_row_spec(ROW_TILE, D_MODEL), _full_spec(w.shape), _full_spec(b.shape),
                  _full_spec(lg.shape), _full_spec(lb.shape)],
        out_specs=[_row_spec(ROW_TILE, GM_HALF), _row_spec(ROW_TILE, GM_HALF)],
        out_shape=[jax.ShapeDtypeStruct((T, GM_HALF), F32)] * 2,
        compiler_params=_cparams(("parallel",)),
        name="gm_in",
    )(x, w, b, lg, lb)


def _gm_mix_kernel(x_ref, u_ref, v_ref, ws_ref, bs_ref, wo_ref, g_ref, b_ref, o_ref, s_scr):
    for c in range(ROW_TILE // GM_CHUNK):
        r = slice(c * GM_CHUNK, (c + 1) * GM_CHUNK)
        for g in range(GM_GROUPS):
            cs = slice(g * GM_GROUP_DIM, (g + 1) * GM_GROUP_DIM)
            sv = jnp.dot(ws_ref[0, g], v_ref[r, cs].astype(BF16), preferred_element_type=F32) + bs_ref[0, g]
            s_scr[r, cs] = (u_ref[r, cs] * sv).astype(BF16)
    mix = jnp.dot(s_scr[...], wo_ref[...], preferred_element_type=F32)
    o_ref[...] = _ln(ALPHA * x_ref[...] + mix, g_ref[...], b_ref[...])


def gm_mix(x, u, v, ws, bs, wo, g, b, n_prompt_tiles):
    T = x.shape[0]
    sel = lambda i: (jnp.where(i >= n_prompt_tiles, 1, 0), 0, 0, 0)
    return pl.pallas_call(
        _gm_mix_kernel,
        grid=(T // ROW_TILE,),
        in_specs=[_row_spec(ROW_TILE, D_MODEL), _row_spec(ROW_TILE, GM_HALF), _row_spec(ROW_TILE, GM_HALF),
                  pl.BlockSpec((1,) + ws.shape[1:], sel), pl.BlockSpec((1,) + bs.shape[1:], sel),
                  _full_spec(wo.shape), _full_spec(g.shape), _full_spec(b.shape)],
        out_specs=_row_spec(ROW_TILE, D_MODEL),
        out_shape=jax.ShapeDtypeStruct((T, D_MODEL), F32),
        scratch_shapes=[pltpu.VMEM((ROW_TILE, GM_HALF), BF16)],
        compiler_params=_cparams(("parallel",)),
        name="gm_mix",
    )(x, u, v, ws, bs, wo, g, b)


def _extract_max(s, codes, sentinel):
    m = jnp.max(s, axis=0, keepdims=True)
    c = jnp.min(jnp.where(s == m, codes, sentinel), axis=0, keepdims=True)
    return m, c, jnp.where(codes == c, -jnp.inf, s)


def _topk_rows(s):
    rows = lax.broadcasted_iota(jnp.int32, s.shape, 0).astype(F32)
    vals, idxs = [], []
    for _ in range(PEER_TOPK):
        m, idx, s = _extract_max(s, rows, float(s.shape[0]))
        vals.append(m)
        idxs.append(idx)
    return jnp.concatenate(vals, axis=0), jnp.concatenate(idxs, axis=0).astype(jnp.int32)


_PAIR_GROUPS = ((None, 0, 16), (None, 1, 8), (0, None, 16), (1, None, 8), (2, None, 8), (3, None, 8), (4, None, 8))


def pair_codes():
    codes = []
    for b_fix, a_fix, n in _PAIR_GROUPS:
        for r in range(n):
            a, b = (a_fix, r) if b_fix is None else (r, b_fix)
            first = a_fix is not None or a >= 2
            ok = first and (a + 1) * (b + 1) <= PEER_TOPK
            codes.append(a * PEER_TOPK + b if ok else PAIR_SENTINEL)
    assert sorted(c for c in codes if c != PAIR_SENTINEL) == sorted(
        a * PEER_TOPK + b for a in range(PEER_TOPK) for b in range(PEER_TOPK) if (a + 1) * (b + 1) <= PEER_TOPK)
    return jnp.broadcast_to(jnp.asarray(codes, F32)[:, None], (len(codes), PEER_TILE))


def _topk_pairs(v1, i1, v2, i2, codes):
    parts = []
    for b_fix, a_fix, n in _PAIR_GROUPS:
        if b_fix is None:
            parts.append(v1[a_fix:a_fix + 1] + v2[0:n])
        else:
            parts.append(v1[0:n] + v2[b_fix:b_fix + 1])
    cand = jnp.where(codes < PAIR_SENTINEL, jnp.concatenate(parts, axis=0), -jnp.inf)
    krow = lax.broadcasted_iota(jnp.int32, v1.shape, 0)
    vals, k1, k2 = [], [], []
    for _ in range(PEER_TOPK):
        m, c, cand = _extract_max(cand, codes, float(PAIR_SENTINEL))
        vals.append(m)
        c = c.astype(jnp.int32)
        k1.append(jnp.sum(jnp.where(krow == (c >> TOPK_SHIFT), i1, 0), axis=0, keepdims=True))
        k2.append(jnp.sum(jnp.where(krow == (c & (PEER_TOPK - 1)), i2, 0), axis=0, keepdims=True))
    return jnp.concatenate(vals, axis=0), jnp.concatenate(k1, axis=0), jnp.concatenate(k2, axis=0)


def _gather_rows(eid_ref, t, tab_ref, dst_ref):
    picks = eid_ref.at[t]
    for p in range(PEER_PICKS):
        row = pl.multiple_of(picks[p], ROW_WORDS)
        dst_ref[pl.ds(ROW_WORDS * p, ROW_WORDS), :] = tab_ref[pl.ds(row, ROW_WORDS), :]


def _chunk_mask():
    sub = lax.broadcasted_iota(jnp.int32, (D_CHUNKS, PEER_PICKS * D_CHUNKS), 0)
    col = lax.broadcasted_iota(jnp.int32, (D_CHUNKS, PEER_PICKS * D_CHUNKS), 1)
    return ((col & (D_CHUNKS - 1)) == sub).astype(F32)


def _for_tokens_pipelined(eid_ref, tab_ref, bufs, compute, tokens_per_trip):
    _gather_rows(eid_ref, 0, tab_ref, bufs[0])

    def body(i, carry):
        t0 = tokens_per_trip * i
        for k in range(tokens_per_trip):
            _gather_rows(eid_ref, jnp.minimum(t0 + k + 1, PEER_TILE - 1), tab_ref, bufs[(k + 1) % 2])
            compute(t0 + k, bufs[k % 2])
        return carry

    lax.fori_loop(0, PEER_TILE // tokens_per_trip, body, 0)


def _peer_route_u_kernel(xr_ref, xc_ref, wq_ref, k_ref, codes_ref, tab_ref, fold_ref, zeros_ref,
                         eid_ref, gate_ref, h_ref, q_scr, eid_t, gate_t, eid_vmem, eid_smem, hrep, sem, *bufs):
    i = pl.program_id(0)
    half = PEER_DK // 2
    nt = (((1,), (1,)), ((), ()))
    eid_to_smem = pltpu.make_async_copy(eid_vmem, eid_smem, sem)

    @pl.when(i == 0)
    def _():
        pltpu.sync_copy(zeros_ref, eid_smem)

    @pl.when(i > 0)
    def _():
        eid_to_smem.wait()

    q = jnp.dot(xr_ref[...].astype(BF16), wq_ref[...], preferred_element_type=F32)
    for h in range(PEER_HEADS):
        q_scr[h] = q[:, h * PEER_DK:(h + 1) * PEER_DK]
    codes = codes_ref[...]
    mask = _chunk_mask()

    def compute(t, rows_ref):
        u = pltpu.bitcast(rows_ref[...], BF16)
        xr = xc_ref[pl.ds(t, 1), :]
        x8 = jnp.concatenate([xr[:, j * 128:(j + 1) * 128] for j in range(D_CHUNKS)], axis=0)
        hi = x8.astype(BF16)
        lo = (x8 - hi.astype(F32)).astype(BF16)
        o = lax.dot_general(jnp.concatenate([hi, lo], axis=0), u, nt, preferred_element_type=F32)
        o8 = (o[0:D_CHUNKS] + o[D_CHUNKS:2 * D_CHUNKS]) * mask
        hrep[pl.ds(t, 1), :] = jnp.sum(o8, axis=0, keepdims=True)

    per_trip = PEER_TILE // PEER_HEADS
    _gather_rows(eid_smem, 0, tab_ref, bufs[0])

    def body(h, carry):
        qh = q_scr[h]
        s1 = lax.dot_general(k_ref[0], qh[:, :half].astype(BF16), nt, preferred_element_type=F32)
        s2 = lax.dot_general(k_ref[1], qh[:, half:].astype(BF16), nt, preferred_element_type=F32)
        v1, i1 = _topk_rows(s1)
        v2, i2 = _topk_rows(s2)
        sc, e1, e2 = _topk_pairs(v1, i1, v2, i2, codes)
        e = jnp.exp(sc - sc[0:1])
        gate_t[h] = e / jnp.sum(e, axis=0, keepdims=True)
        eid_t[h] = (e1 * PEER_NKEYS + e2) * ROW_WORDS
        t0 = per_trip * h
        for k in range(per_trip):
            _gather_rows(eid_smem, jnp.minimum(t0 + k + 1, PEER_TILE - 1), tab_ref, bufs[(k + 1) % 2])
            compute(t0 + k, bufs[k % 2])
        return carry

    lax.fori_loop(0, PEER_HEADS, body, 0)
    h_ref[...] = jnp.dot(hrep[...], fold_ref[...], precision=lax.Precision.HIGHEST, preferred_element_type=F32)
    eid = jnp.transpose(eid_t[...].reshape(PEER_PICKS, PEER_TILE))
    eid_ref[...] = eid
    eid_vmem[...] = eid
    gate_ref[...] = jnp.transpose(gate_t[...].reshape(PEER_PICKS, PEER_TILE))
    eid_to_smem.start()

    @pl.when(i == pl.num_programs(0) - 1)
    def _():
        eid_to_smem.wait()


def _peer_v_kernel(eid_ref, h_ref, gate_ref, x_ref, tab_ref, spread_ref, g_ref, b_ref, y_ref, a_hi, a_lo, mix, *bufs):
    a = gate_ref[...] * _gelu(h_ref[...])
    hi = a.astype(BF16)
    lo = (a - hi.astype(F32)).astype(BF16)
    a_hi[...] = jnp.dot(hi, spread_ref[...], preferred_element_type=F32)
    a_lo[...] = jnp.dot(lo, spread_ref[...], preferred_element_type=F32)
    mask = _chunk_mask()

    def compute(t, rows_ref):
        v = pltpu.bitcast(rows_ref[...], BF16)
        lhs = jnp.concatenate([(a_hi[pl.ds(t, 1), :] * mask).astype(BF16),
                               (a_lo[pl.ds(t, 1), :] * mask).astype(BF16)], axis=0)
        o = jnp.dot(lhs, v, preferred_element_type=F32)
        o8 = o[0:D_CHUNKS] + o[D_CHUNKS:2 * D_CHUNKS]
        mix[pl.ds(t, 1), :] = jnp.concatenate([o8[j:j + 1] for j in range(D_CHUNKS)], axis=1)

    _for_tokens_pipelined(eid_ref, tab_ref, bufs, compute, V_TOKENS_PER_TRIP)
    y_ref[...] = _ln(ALPHA * x_ref[...] + mix[...], g_ref[...], b_ref[...])


def _peer_specs():
    eid_spec = pl.BlockSpec((PEER_TILE, PEER_PICKS), lambda i: (i, 0), memory_space=pltpu.SMEM)
    tab_spec = pl.BlockSpec((PEER_EXPERTS * ROW_WORDS, 128), lambda i: (0, 0), pipeline_mode=pl.Buffered(1))
    bufs = [pltpu.VMEM((PEER_PICKS * ROW_WORDS, 128), jnp.int32)] * 2
    return eid_spec, tab_spec, bufs


def peer_route_u(x, wq, keys, codes, tab, fold):
    T = x.shape[0]
    n = T // PEER_TILE
    _, tab_spec, bufs = _peer_specs()
    cur = lambda i: (jnp.minimum(i, n - 1), 0)
    prev = lambda i: (jnp.maximum(i - 1, 0), 0)
    zeros = jnp.zeros((PEER_TILE, PEER_PICKS), jnp.int32)
    tile = (PEER_TILE, PEER_PICKS)
    return pl.pallas_call(
        _peer_route_u_kernel,
        grid=(n + 1,),
        in_specs=[pl.BlockSpec((PEER_TILE, D_MODEL), cur), pl.BlockSpec((PEER_TILE, D_MODEL), prev),
                  _full_spec(wq.shape), _full_spec(keys.shape), _full_spec(codes.shape), tab_spec,
                  _full_spec(fold.shape), _full_spec(zeros.shape)],
        out_specs=[pl.BlockSpec(tile, cur), pl.BlockSpec(tile, cur), pl.BlockSpec(tile, prev)],
        out_shape=[jax.ShapeDtypeStruct((T, PEER_PICKS), jnp.int32), jax.ShapeDtypeStruct((T, PEER_PICKS), F32),
                   jax.ShapeDtypeStruct((T, PEER_PICKS), F32)],
        scratch_shapes=[pltpu.VMEM((PEER_HEADS, PEER_TILE, PEER_DK), F32),
                        pltpu.VMEM((PEER_HEADS, PEER_TOPK, PEER_TILE), jnp.int32),
                        pltpu.VMEM((PEER_HEADS, PEER_TOPK, PEER_TILE), F32),
                        pltpu.VMEM(tile, jnp.int32), pltpu.SMEM(tile, jnp.int32),
                        pltpu.VMEM((PEER_TILE, PEER_PICKS * D_CHUNKS), F32),
                        pltpu.SemaphoreType.DMA(())] + bufs,
        compiler_params=_cparams(("arbitrary",)),
        name="peer_route_u",
    )(x, x, wq, keys, codes, tab, fold, zeros)


def peer_v(eid, hdn, gate, x, tab, spread, g, b):
    T = eid.shape[0]
    eid_spec, tab_spec, bufs = _peer_specs()
    return pl.pallas_call(
        _peer_v_kernel,
        grid=(T // PEER_TILE,),
        in_specs=[eid_spec, _row_spec(PEER_TILE, PEER_PICKS), _row_spec(PEER_TILE, PEER_PICKS),
                  _row_spec(PEER_TILE, D_MODEL), tab_spec, _full_spec(spread.shape), _full_spec(g.shape), _full_spec(b.shape)],
        out_specs=_row_spec(PEER_TILE, D_MODEL),
        out_shape=jax.ShapeDtypeStruct((T, D_MODEL), F32),
        scratch_shapes=[pltpu.VMEM((PEER_TILE, PEER_PICKS * D_CHUNKS), F32)] * 2 + [pltpu.VMEM((PEER_TILE, D_MODEL), F32)] + bufs,
        compiler_params=_cparams(("arbitrary",)),
        name="peer_v",
    )(eid, hdn, gate, x, tab, spread, g, b)


def _pack_kernel(t_ref, o_ref):
    n = t_ref.shape[1]
    for s in range(ROW_WORDS):
        lo = t_ref[0, :, (2 * s) * 128:(2 * s + 1) * 128].astype(BF16).astype(F32)
        hi = t_ref[0, :, (2 * s + 1) * 128:(2 * s + 2) * 128].astype(BF16).astype(F32)
        word = pltpu.bitcast(hi, jnp.uint32) | (pltpu.bitcast(lo, jnp.uint32) >> 16)
        o_ref[pl.ds(s, n, stride=ROW_WORDS), :] = pltpu.bitcast(word, jnp.int32)


def pack_table(tabs, layer):
    e = tabs.shape[1]
    return pl.pallas_call(
        _pack_kernel,
        grid=(e // PACK_TILE,),
        in_specs=[pl.BlockSpec((1, PACK_TILE, D_MODEL), lambda i: (layer, i, 0))],
        out_specs=_row_spec(PACK_TILE * ROW_WORDS, 128),
        out_shape=jax.ShapeDtypeStruct((e * ROW_WORDS, 128), jnp.int32),
        compiler_params=_cparams(("parallel",)),
        name="pack_table",
    )(tabs)


def peer_consts():
    col = jnp.arange(PEER_PICKS * D_CHUNKS) // D_CHUNKS
    spread = (col[None, :] == jnp.arange(PEER_PICKS)[:, None])
    return spread.astype(BF16), jnp.transpose(spread).astype(F32)


def peer_block(x, wq, keys, u_packed, v_packed, spread, fold, g, b):
    eid, gate, hdn = peer_route_u(x, wq, keys, pair_codes(), u_packed, fold)
    return peer_v(eid, hdn, gate, x, v_packed, spread, g, b)


def _mla_kv_kernel(x_ref, w_ref, g_ref, cos_ref, sin_ref, c_ref, kpe_ref):
    kv = jnp.dot(x_ref[...].astype(BF16), w_ref[...], preferred_element_type=F32)
    c_ref[...] = _rms(kv[:, :KV_LORA], g_ref[...])
    kp = kv[:, KV_LORA:KV_LORA + QK_ROPE]
    sw = jnp.concatenate([kp[:, QK_ROPE // 2:], kp[:, :QK_ROPE // 2]], axis=1)
    kpe_ref[...] = kp * cos_ref[...] + sw * sin_ref[...]


def mla_kv(x, w, g, cos2, sin2):
    T = x.shape[0]
    return pl.pallas_call(
        _mla_kv_kernel,
        grid=(T // ROW_TILE,),
        in_specs=[_row_spec(ROW_TILE, D_MODEL), _full_spec(w.shape), _full_spec(g.shape),
                  _row_spec(ROW_TILE, QK_ROPE), _row_spec(ROW_TILE, QK_ROPE)],
        out_specs=[_row_spec(ROW_TILE, KV_LORA), _row_spec(ROW_TILE, QK_ROPE)],
        out_shape=[jax.ShapeDtypeStruct((T, KV_LORA), F32), jax.ShapeDtypeStruct((T, QK_ROPE), F32)],
        compiler_params=_cparams(("parallel",)),
        name="mla_kv",
    )(x, w, g, cos2, sin2)


def _mla_expand_kernel(c_ref, kpe_ref, wk_ref, wv_ref, k_ref, v_ref):
    cb = c_ref[...].astype(BF16)
    kn = jnp.dot(cb, wk_ref[...], preferred_element_type=F32)
    v_ref[...] = jnp.dot(cb, wv_ref[...], preferred_element_type=F32).astype(BF16)
    kp = kpe_ref[...]
    kpad = jnp.concatenate([kp, jnp.zeros_like(kp)], axis=1).astype(BF16)
    for h in range(MLA_HEADS):
        k_ref[:, h * HEAD_SLOT:h * HEAD_SLOT + QK_NOPE] = kn[:, h * QK_NOPE:(h + 1) * QK_NOPE].astype(BF16)
        k_ref[:, h * HEAD_SLOT + QK_NOPE:(h + 1) * HEAD_SLOT] = kpad


def mla_expand(c, kpe, wk, wv):
    R = c.shape[0]
    return pl.pallas_call(
        _mla_expand_kernel,
        grid=(R // ROW_TILE,),
        in_specs=[_row_spec(ROW_TILE, KV_LORA), _row_spec(ROW_TILE, QK_ROPE), _full_spec(wk.shape), _full_spec(wv.shape)],
        out_specs=[_row_spec(ROW_TILE, MLA_HEADS * HEAD_SLOT), _row_spec(ROW_TILE, MLA_HEADS * V_HEAD)],
        out_shape=[jax.ShapeDtypeStruct((R, MLA_HEADS * HEAD_SLOT), BF16), jax.ShapeDtypeStruct((R, MLA_HEADS * V_HEAD), BF16)],
        compiler_params=_cparams(("parallel",)),
        name="mla_expand",
    )(c, kpe, wk, wv)


def _mla_q_kernel(x_ref, wdq_ref, g_ref, wuq_ref, cos_ref, sin_ref, q_ref):
    cq = _rms(jnp.dot(x_ref[...].astype(BF16), wdq_ref[...], preferred_element_type=F32), g_ref[...])
    q = jnp.dot(cq.astype(BF16), wuq_ref[...], preferred_element_type=F32)
    q = q * (ATTN_SCALE * math.log2(math.e))
    lane = lax.broadcasted_iota(jnp.int32, (ROW_TILE, 128), 1)
    cos = cos_ref[...]
    sin = sin_ref[...]
    for h in range(MLA_HEADS):
        q_ref[:, h * HEAD_SLOT:h * HEAD_SLOT + QK_NOPE] = q[:, h * HEAD_SLOT:h * HEAD_SLOT + QK_NOPE].astype(BF16)
        seg = q[:, h * HEAD_SLOT + QK_NOPE:(h + 1) * HEAD_SLOT]
        sw = jnp.where(lane < QK_ROPE // 2, pltpu.roll(seg, 128 - QK_ROPE // 2, 1),
                       jnp.where(lane < QK_ROPE, pltpu.roll(seg, QK_ROPE // 2, 1), 0.0))
        q_ref[:, h * HEAD_SLOT + QK_NOPE:(h + 1) * HEAD_SLOT] = (seg * cos + sw * sin).astype(BF16)


def mla_q(x, wdq, g, wuq, cosq, sinq):
    T = x.shape[0]
    return pl.pallas_call(
        _mla_q_kernel,
        grid=(T // ROW_TILE,),
        in_specs=[_row_spec(ROW_TILE, D_MODEL), _full_spec(wdq.shape), _full_spec(g.shape), _full_spec(wuq.shape),
                  _row_spec(ROW_TILE, 128), _row_spec(ROW_TILE, 128)],
        out_specs=_row_spec(ROW_TILE, MLA_HEADS * HEAD_SLOT),
        out_shape=jax.ShapeDtypeStruct((T, MLA_HEADS * HEAD_SLOT), BF16),
        compiler_params=_cparams(("parallel",)),
        name="mla_q",
    )(x, wdq, g, wuq, cosq, sinq)


def _attn_prompt_kernel(q_ref, k_ref, v_ref, o_ref, s_even, s_odd):
    qi = pl.program_id(2)
    q = q_ref[...]
    nt = (((1,), (1,)), ((), ()))

    def scores(ki, dst):
        off = pl.multiple_of(ki * ATTN_TILE, ATTN_TILE)
        dst[...] = lax.dot_general(q, k_ref[pl.ds(off, ATTN_TILE), :], nt, preferred_element_type=F32)

    def update(ki, src, carry, diagonal):
        m, l, acc = carry
        off = pl.multiple_of(ki * ATTN_TILE, ATTN_TILE)
        s = src[...]
        if diagonal:
            qc = lax.broadcasted_iota(jnp.int32, s.shape, 0) // CHUNK
            kc = lax.broadcasted_iota(jnp.int32, s.shape, 1) // CHUNK
            s = jnp.where(kc <= qc, s, -jnp.inf)
        m_new = jnp.maximum(m, jnp.max(s, axis=1, keepdims=True))
        alpha = jnp.exp2(m - m_new)
        p = jnp.exp2(s - m_new)
        l = alpha * l + jnp.sum(p, axis=1, keepdims=True)
        acc = alpha * acc + jnp.dot(p.astype(BF16), v_ref[pl.ds(off, ATTN_TILE), :], preferred_element_type=F32)
        return m_new, l, acc

    init = (jnp.full((ATTN_TILE, 1), -jnp.inf, F32), jnp.zeros((ATTN_TILE, 1), F32), jnp.zeros((ATTN_TILE, V_HEAD), F32))
    scores(0, s_even)

    def pair(j, carry):
        scores(2 * j + 1, s_odd)
        carry = update(2 * j, s_even, carry, False)
        scores(jnp.minimum(2 * j + 2, qi), s_even)
        return update(2 * j + 1, s_odd, carry, False)

    carry = lax.fori_loop(0, qi // 2, pair, init)

    def last_even(carry):
        return update(qi, s_even, carry, True)

    def last_odd(carry):
        scores(qi, s_odd)
        carry = update(qi - 1, s_even, carry, False)
        return update(qi, s_odd, carry, True)

    m, l, acc = lax.cond(qi % 2 == 0, last_even, last_odd, carry)
    o_ref[...] = (acc / l).astype(BF16)


def attn_prompt(q, k, v, batch, seq):
    nq = seq // ATTN_TILE
    return pl.pallas_call(
        _attn_prompt_kernel,
        grid=(batch, MLA_HEADS, nq),
        in_specs=[pl.BlockSpec((ATTN_TILE, HEAD_SLOT), lambda b, h, i: (b * nq + i, h)),
                  pl.BlockSpec((seq, HEAD_SLOT), lambda b, h, i: (b, h)),
                  pl.BlockSpec((seq, V_HEAD), lambda b, h, i: (b, h))],
        out_specs=pl.BlockSpec((ATTN_TILE, V_HEAD), lambda b, h, i: (b * nq + i, h)),
        out_shape=jax.ShapeDtypeStruct((batch * seq, MLA_HEADS * V_HEAD), BF16),
        scratch_shapes=[pltpu.VMEM((ATTN_TILE, ATTN_TILE), F32)] * 2,
        compiler_params=_cparams(("parallel", "parallel", "arbitrary")),
        name="attn_prompt",
    )(q, k, v)


def _attn_sample_kernel(q_ref, k_ref, v_ref, o_ref):
    nt = (((1,), (1,)), ((), ()))
    s = lax.dot_general(q_ref[...], k_ref[...], nt, preferred_element_type=F32)
    p = jnp.exp2(s - jnp.max(s, axis=1, keepdims=True))
    o = jnp.dot(p.astype(BF16), v_ref[...], preferred_element_type=F32)
    o_ref[...] = (o / jnp.sum(p, axis=1, keepdims=True)).astype(BF16)


def attn_sample(q, k, v, batch, q_len, kv_len, q_block0):
    return pl.pallas_call(
        _attn_sample_kernel,
        grid=(batch, MLA_HEADS),
        in_specs=[pl.BlockSpec((q_len, HEAD_SLOT), lambda b, h: (q_block0 + b, h)),
                  pl.BlockSpec((kv_len, HEAD_SLOT), lambda b, h: (b, h)),
                  pl.BlockSpec((kv_len, V_HEAD), lambda b, h: (b, h))],
        out_specs=pl.BlockSpec((q_len, V_HEAD), lambda b, h: (b, h)),
        out_shape=jax.ShapeDtypeStruct((batch * q_len, MLA_HEADS * V_HEAD), BF16),
        compiler_params=_cparams(("parallel", "parallel")),
        name="attn_sample",
    )(q, k, v)


def _proj_ln_kernel(x_ref, o_ref_in, w_ref, g_ref, b_ref, y_ref):
    mix = jnp.dot(o_ref_in[...], w_ref[...], preferred_element_type=F32)
    y_ref[...] = _ln(ALPHA * x_ref[...] + mix, g_ref[...], b_ref[...])


def proj_ln(x, o, w, g, b):
    T = x.shape[0]
    return pl.pallas_call(
        _proj_ln_kernel,
        grid=(T // ROW_TILE,),
        in_specs=[_row_spec(ROW_TILE, D_MODEL), _row_spec(ROW_TILE, o.shape[1]), _full_spec(w.shape),
                  _full_spec(g.shape), _full_spec(b.shape)],
        out_specs=_row_spec(ROW_TILE, D_MODEL),
        out_shape=jax.ShapeDtypeStruct((T, D_MODEL), F32),
        compiler_params=_cparams(("parallel",)),
        name="proj_ln",
    )(x, o, w, g, b)


def _rope_tables(pos):
    inv = 1.0 / (ROPE_BASE ** (jnp.arange(0, QK_ROPE, 2, dtype=F32) / QK_ROPE))
    ang = pos.astype(F32)[:, None] * inv[None, :]
    cos, sin = jnp.cos(ang), jnp.sin(ang)
    cos2 = jnp.concatenate([cos, cos], axis=1)
    sin2 = jnp.concatenate([-sin, sin], axis=1)
    pad = jnp.zeros_like(cos2)
    return cos2, sin2, jnp.concatenate([cos2, pad], axis=1), jnp.concatenate([sin2, pad], axis=1)


def _gm_chunk_mats(w_s, b_s, dec_seq):
    i = jnp.arange(GM_CHUNK)
    mask = (i[None, :] // CHUNK) <= (i[:, None] // CHUNK)
    full = jnp.where(mask[None], w_s, 0.0)
    rep = GM_CHUNK // dec_seq
    blk = w_s[:, :dec_seq, :dec_seq]
    same = (i[:, None] // dec_seq) == (i[None, :] // dec_seq)
    diag = jnp.where(same[None], jnp.tile(blk, (1, rep, rep)), 0.0)
    ws = jnp.stack([full, diag], axis=0).astype(BF16)
    bias = jnp.stack([b_s, jnp.tile(b_s[:, :dec_seq], (1, rep))], axis=0)
    bs = jnp.broadcast_to(bias[..., None], bias.shape + (GM_GROUP_DIM,)).astype(F32)
    return ws, bs


def _pad_q_weight(w_uq):
    w = w_uq.reshape(Q_LORA, MLA_HEADS, QK_NOPE + QK_ROPE)
    w = jnp.pad(w, ((0, 0), (0, 0), (0, HEAD_SLOT - QK_NOPE - QK_ROPE)))
    return w.reshape(Q_LORA, MLA_HEADS * HEAD_SLOT).astype(BF16)


def kernel(x_prompt, x_sample, cache_ckv, cache_kpe, ln1_g, ln1_b, ln2_g, ln2_b, gm_w_in, gm_b_in, gm_ln_g, gm_ln_b,
           gm_w_s, gm_b_s, gm_w_out, mla_w_dkv, mla_kv_norm_g, mla_w_ukv, mla_w_dq, mla_q_norm_g, mla_w_uq, mla_w_o,
           peer_w_q, peer_subkeys, peer_u, peer_v):
    B, S, _ = x_prompt.shape
    DB, DS, _ = x_sample.shape
    PAST = cache_ckv.shape[1]
    TP, TS = B * S, DB * DS
    T = TP + TS
    assert TP % ROW_TILE == 0 and TS % ROW_TILE == 0 and T % PEER_TILE == 0 and S % ATTN_TILE == 0
    assert GM_CHUNK % DS == 0 and DS <= CHUNK and PAST % CHUNK == 0 and (PAST + DS) % 16 == 0

    x = jnp.concatenate([x_prompt.reshape(TP, D_MODEL), x_sample.reshape(TS, D_MODEL)], axis=0)
    pos = jnp.concatenate([jnp.tile(jnp.arange(S, dtype=jnp.int32), B),
                           jnp.tile(PAST + jnp.arange(DS, dtype=jnp.int32), DB)])
    cos2, sin2, cosq, sinq = _rope_tables(pos)
    spread, fold = peer_consts()
    row = lambda a: a.reshape(1, -1)

    def peer_layer(x, l):
        return peer_block(x, peer_w_q[l].astype(BF16), peer_subkeys[l].astype(BF16), pack_table(peer_u, l),
                          pack_table(peer_v, l), spread, fold, row(ln2_g[l]), row(ln2_b[l]))

    gm_rows = []
    for l in range(N_A):
        u, v = gm_in(x, gm_w_in[l].astype(BF16), row(gm_b_in[l]), row(gm_ln_g[l]), row(gm_ln_b[l]))
        gm_rows.append(v[TP:].reshape(DB, DS, GM_HALF))
        ws, bs = _gm_chunk_mats(gm_w_s[l], gm_b_s[l], DS)
        x = gm_mix(x, u, v, ws, bs, gm_w_out[l].astype(BF16), row(ln1_g[l]), row(ln1_b[l]), TP // ROW_TILE)
        x = peer_layer(x, l)

    w_dkv = jnp.pad(mla_w_dkv, ((0, 0), (0, 128 - QK_ROPE))).astype(BF16)
    c_new, kpe_new = mla_kv(x, w_dkv, row(mla_kv_norm_g), cos2, sin2)
    w_ukv = mla_w_ukv.reshape(KV_LORA, MLA_HEADS, QK_NOPE + V_HEAD)
    w_uk = w_ukv[:, :, :QK_NOPE].reshape(KV_LORA, MLA_HEADS * QK_NOPE).astype(BF16)
    w_uv = w_ukv[:, :, QK_NOPE:].reshape(KV_LORA, MLA_HEADS * V_HEAD).astype(BF16)
    kp_cat, vp = mla_expand(c_new[:TP], kpe_new[:TP], w_uk, w_uv)
    KV = PAST + DS
    c_all = jnp.concatenate([cache_ckv, c_new[TP:].reshape(DB, DS, KV_LORA)], axis=1).reshape(DB * KV, KV_LORA)
    kpe_all = jnp.concatenate([cache_kpe, kpe_new[TP:].reshape(DB, DS, QK_ROPE)], axis=1).reshape(DB * KV, QK_ROPE)
    ks_cat, vs = mla_expand(c_all, kpe_all, w_uk, w_uv)

    for j in range(DEPTH - N_A):
        l = N_A + j
        q = mla_q(x, mla_w_dq[j].astype(BF16), row(mla_q_norm_g[j]), _pad_q_weight(mla_w_uq[j]), cosq, sinq)
        o = jnp.concatenate([attn_prompt(q, kp_cat, vp, B, S),
                             attn_sample(q, ks_cat, vs, DB, DS, KV, TP // DS)], axis=0)
        x = proj_ln(x, o, mla_w_o[j].astype(BF16), row(ln1_g[l]), row(ln1_b[l]))
        x = peer_layer(x, l)

    return (x[:TP].reshape(B, S, D_MODEL), x[TP:].reshape(DB, DS, D_MODEL), jnp.stack(gm_rows, axis=0),
            c_new[:TP].reshape(B, S, KV_LORA), kpe_new[:TP].reshape(B, S, QK_ROPE),
            c_new[TP:].reshape(DB, DS, KV_LORA), kpe_new[TP:].reshape(DB, DS, QK_ROPE))
```

```python
import functools
import math

import jax
import jax.numpy as jnp
from jax import lax
from jax.experimental import pallas as pl
from jax.experimental.pallas import tpu as pltpu
from jax.experimental.pallas import tpu_sc as plsc

F32 = jnp.float32
BF16 = jnp.bfloat16

D_MODEL = 1024
DEPTH = 4
N_A = DEPTH // 2
CHUNK = 64
ALPHA = (2.0 * DEPTH) ** 0.25
LN_EPS = 1e-5
RMS_EPS = 1e-6

GM_CHUNK = 128
GM_HALF = 2 * D_MODEL
GM_GROUPS = 8
GM_GROUP_DIM = GM_HALF // GM_GROUPS

MLA_HEADS = 8
QK_NOPE = 128
QK_ROPE = 64
V_HEAD = 128
KV_LORA = D_MODEL // 4
Q_LORA = 3 * D_MODEL // 8
ROPE_BASE = 10000.0
ATTN_SCALE = (QK_NOPE + QK_ROPE) ** -0.5
HEAD_SLOT = 256

PEER_HEADS = 8
PEER_NKEYS = 128
PEER_EXPERTS = PEER_NKEYS * PEER_NKEYS
PEER_DK = 256
PEER_TOPK = 16
PEER_PICKS = PEER_HEADS * PEER_TOPK
TOPK_SHIFT = PEER_TOPK.bit_length() - 1
PAIR_SENTINEL = 1 << 20
ROW_WORDS = 4
D_CHUNKS = D_MODEL // 128

ROW_TILE = 256
PEER_TILE = 128
PACK_TILE = 512
SC_TOKENS = 8192
SC_SUBCORES = 32
SC_LANES = 16
SC_ROWS = 16
V_TOKENS_PER_TRIP = 8
ATTN_TILE = 512
VMEM_LIMIT = 56 * 1024 * 1024


def _cparams(sem):
    return pltpu.CompilerParams(dimension_semantics=sem, vmem_limit_bytes=VMEM_LIMIT)


def _gelu(x):
    return 0.5 * x * (1.0 + lax.erf(x * (1.0 / math.sqrt(2.0))))


def _ln(x, g, b):
    mu = jnp.mean(x, axis=-1, keepdims=True)
    xc = x - mu
    var = jnp.mean(xc * xc, axis=-1, keepdims=True)
    return xc * lax.rsqrt(var + LN_EPS) * g + b


def _rms(x, g):
    return x * lax.rsqrt(jnp.mean(x * x, axis=-1, keepdims=True) + RMS_EPS) * g


def _row_spec(tile, width):
    return pl.BlockSpec((tile, width), lambda i: (i, 0))


def _full_spec(shape):
    nd = len(shape)
    return pl.BlockSpec(shape, lambda i: (0,) * nd)


def _gm_in_kernel(x_ref, w_ref, b_ref, lg_ref, lb_ref, u_ref, v_ref):
    z = jnp.dot(x_ref[...].astype(BF16), w_ref[...], preferred_element_type=F32) + b_ref[...]
    z = _gelu(z)
    u_ref[...] = z[:, :GM_HALF]
    v_ref[...] = _ln(z[:, GM_HALF:], lg_ref[...], lb_ref[...])


def gm_in(x, w, b, lg, lb):
    T = x.shape[0]
    return pl.pallas_call(
        _gm_in_kernel,
        grid=(T // ROW_TILE,),
        in_specs=[_row_spec(ROW_TILE, D_MODEL), _full_spec(w.shape), _full_spec(b.shape),
                  _full_spec(lg.shape), _full_spec(lb.shape)],
        out_specs=[_row_spec(ROW_TILE, GM_HALF), _row_spec(ROW_TILE, GM_HALF)],
        out_shape=[jax.ShapeDtypeStruct((T, GM_HALF), F32)] * 2,
        compiler_params=_cparams(("parallel",)),
        name="gm_in",
    )(x, w, b, lg, lb)


def _gm_mix_kernel(x_ref, u_ref, v_ref, ws_ref, bs_ref, wo_ref, g_ref, b_ref, o_ref, s_scr):
    for c in range(ROW_TILE // GM_CHUNK):
        r = slice(c * GM_CHUNK, (c + 1) * GM_CHUNK)
        for g in range(GM_GROUPS):
            cs = slice(g * GM_GROUP_DIM, (g + 1) * GM_GROUP_DIM)
            sv = jnp.dot(ws_ref[0, g], v_ref[r, cs].astype(BF16), preferred_element_type=F32) + bs_ref[0, g]
            s_scr[r, cs] = (u_ref[r, cs] * sv).astype(BF16)
    mix = jnp.dot(s_scr[...], wo_ref[...], preferred_element_type=F32)
    o_ref[...] = _ln(ALPHA * x_ref[...] + mix, g_ref[...], b_ref[...])


def gm_mix(x, u, v, ws, bs, wo, g, b, n_prompt_tiles):
    T = x.shape[0]
    sel = lambda i: (jnp.where(i >= n_prompt_tiles, 1, 0), 0, 0, 0)
    return pl.pallas_call(
        _gm_mix_kernel,
        grid=(T // ROW_TILE,),
        in_specs=[_row_spec(ROW_TILE, D_MODEL), _row_spec(ROW_TILE, GM_HALF), _row_spec(ROW_TILE, GM_HALF),
                  pl.BlockSpec((1,) + ws.shape[1:], sel), pl.BlockSpec((1,) + bs.shape[1:], sel),
                  _full_spec(wo.shape), _full_spec(g.shape), _full_spec(b.shape)],
        out_specs=_row_spec(ROW_TILE, D_MODEL),
        out_shape=jax.ShapeDtypeStruct((T, D_MODEL), F32),
        scratch_shapes=[pltpu.VMEM((ROW_TILE, GM_HALF), BF16)],
        compiler_params=_cparams(("parallel",)),
        name="gm_mix",
    )(x, u, v, ws, bs, wo, g, b)


def _extract_max(s, codes, sentinel):
    m = jnp.max(s, axis=0, keepdims=True)
    c = jnp.min(jnp.where(s == m, codes, sentinel), axis=0, keepdims=True)
    return m, c, jnp.where(codes == c, -jnp.inf, s)


def _topk_rows(s):
    rows = lax.broadcasted_iota(jnp.int32, s.shape, 0).astype(F32)
    vals, idxs = [], []
    for _ in range(PEER_TOPK):
        m, idx, s = _extract_max(s, rows, float(s.shape[0]))
        vals.append(m)
        idxs.append(idx)
    return jnp.concatenate(vals, axis=0), jnp.concatenate(idxs, axis=0).astype(jnp.int32)


_PAIR_GROUPS = ((None, 0, 16), (None, 1, 8), (0, None, 16), (1, None, 8), (2, None, 8), (3, None, 8), (4, None, 8))


def pair_codes():
    codes = []
    for b_fix, a_fix, n in _PAIR_GROUPS:
        for r in range(n):
            a, b = (a_fix, r) if b_fix is None else (r, b_fix)
            first = a_fix is not None or a >= 2
            ok = first and (a + 1) * (b + 1) <= PEER_TOPK
            codes.append(a * PEER_TOPK + b if ok else PAIR_SENTINEL)
    assert sorted(c for c in codes if c != PAIR_SENTINEL) == sorted(
        a * PEER_TOPK + b for a in range(PEER_TOPK) for b in range(PEER_TOPK) if (a + 1) * (b + 1) <= PEER_TOPK)
    return jnp.broadcast_to(jnp.asarray(codes, F32)[:, None], (len(codes), PEER_TILE))


def _topk_pairs(v1, i1, v2, i2, codes):
    parts = []
    for b_fix, a_fix, n in _PAIR_GROUPS:
        if b_fix is None:
            parts.append(v1[a_fix:a_fix + 1] + v2[0:n])
        else:
            parts.append(v1[0:n] + v2[b_fix:b_fix + 1])
    cand = jnp.where(codes < PAIR_SENTINEL, jnp.concatenate(parts, axis=0), -jnp.inf)
    krow = lax.broadcasted_iota(jnp.int32, v1.shape, 0)
    vals, k1, k2 = [], [], []
    for _ in range(PEER_TOPK):
        m, c, cand = _extract_max(cand, codes, float(PAIR_SENTINEL))
        vals.append(m)
        c = c.astype(jnp.int32)
        k1.append(jnp.sum(jnp.where(krow == (c >> TOPK_SHIFT), i1, 0), axis=0, keepdims=True))
        k2.append(jnp.sum(jnp.where(krow == (c & (PEER_TOPK - 1)), i2, 0), axis=0, keepdims=True))
    return jnp.concatenate(vals, axis=0), jnp.concatenate(k1, axis=0), jnp.concatenate(k2, axis=0)


def _gather_rows(eid_ref, t, tab_ref, dst_ref):
    picks = eid_ref.at[t]
    for p in range(PEER_PICKS):
        row = pl.multiple_of(picks[p], ROW_WORDS)
        dst_ref[pl.ds(ROW_WORDS * p, ROW_WORDS), :] = tab_ref[pl.ds(row, ROW_WORDS), :]


def _chunk_mask():
    sub = lax.broadcasted_iota(jnp.int32, (D_CHUNKS, PEER_PICKS * D_CHUNKS), 0)
    col = lax.broadcasted_iota(jnp.int32, (D_CHUNKS, PEER_PICKS * D_CHUNKS), 1)
    return ((col & (D_CHUNKS - 1)) == sub).astype(F32)


def _for_tokens_pipelined(eid_ref, tab_ref, bufs, compute, tokens_per_trip):
    _gather_rows(eid_ref, 0, tab_ref, bufs[0])

    def body(i, carry):
        t0 = tokens_per_trip * i
        for k in range(tokens_per_trip):
            _gather_rows(eid_ref, jnp.minimum(t0 + k + 1, PEER_TILE - 1), tab_ref, bufs[(k + 1) % 2])
            compute(t0 + k, bufs[k % 2])
        return carry

    lax.fori_loop(0, PEER_TILE // tokens_per_trip, body, 0)


def _peer_route_u_kernel(xr_ref, xc_ref, wq_ref, k_ref, codes_ref, tab_ref, fold_ref, zeros_ref,
                         eid_ref, gate_ref, h_ref, q_scr, eid_t, gate_t, eid_vmem, eid_smem, hrep, sem, *bufs):
    i = pl.program_id(0)
    half = PEER_DK // 2
    nt = (((1,), (1,)), ((), ()))
    eid_to_smem = pltpu.make_async_copy(eid_vmem, eid_smem, sem)

    @pl.when(i == 0)
    def _():
        pltpu.sync_copy(zeros_ref, eid_smem)

    @pl.when(i > 0)
    def _():
        eid_to_smem.wait()

    q = jnp.dot(xr_ref[...].astype(BF16), wq_ref[...], preferred_element_type=F32)
    for h in range(PEER_HEADS):
        q_scr[h] = q[:, h * PEER_DK:(h + 1) * PEER_DK]
    codes = codes_ref[...]
    mask = _chunk_mask()

    def compute(t, rows_ref):
        u = pltpu.bitcast(rows_ref[...], BF16)
        xr = xc_ref[pl.ds(t, 1), :]
        x8 = jnp.concatenate([xr[:, j * 128:(j + 1) * 128] for j in range(D_CHUNKS)], axis=0)
        hi = x8.astype(BF16)
        lo = (x8 - hi.astype(F32)).astype(BF16)
        o = lax.dot_general(jnp.concatenate([hi, lo], axis=0), u, nt, preferred_element_type=F32)
        o8 = (o[0:D_CHUNKS] + o[D_CHUNKS:2 * D_CHUNKS]) * mask
        hrep[pl.ds(t, 1), :] = jnp.sum(o8, axis=0, keepdims=True)

    per_trip = PEER_TILE // PEER_HEADS
    _gather_rows(eid_smem, 0, tab_ref, bufs[0])

    def body(h, carry):
        qh = q_scr[h]
        s1 = lax.dot_general(k_ref[0], qh[:, :half].astype(BF16), nt, preferred_element_type=F32)
        s2 = lax.dot_general(k_ref[1], qh[:, half:].astype(BF16), nt, preferred_element_type=F32)
        v1, i1 = _topk_rows(s1)
        v2, i2 = _topk_rows(s2)
        sc, e1, e2 = _topk_pairs(v1, i1, v2, i2, codes)
        e = jnp.exp(sc - sc[0:1])
        gate_t[h] = e / jnp.sum(e, axis=0, keepdims=True)
        eid_t[h] = (e1 * PEER_NKEYS + e2) * ROW_WORDS
        t0 = per_trip * h
        for k in range(per_trip):
            _gather_rows(eid_smem, jnp.minimum(t0 + k + 1, PEER_TILE - 1), tab_ref, bufs[(k + 1) % 2])
            compute(t0 + k, bufs[k % 2])
        return carry

    lax.fori_loop(0, PEER_HEADS, body, 0)
    h_ref[...] = jnp.dot(hrep[...], fold_ref[...], precision=lax.Precision.HIGHEST, preferred_element_type=F32)
    eid = jnp.transpose(eid_t[...].reshape(PEER_PICKS, PEER_TILE))
    eid_ref[...] = eid
    eid_vmem[...] = eid
    gate_ref[...] = jnp.transpose(gate_t[...].reshape(PEER_PICKS, PEER_TILE))
    eid_to_smem.start()

    @pl.when(i == pl.num_programs(0) - 1)
    def _():
        eid_to_smem.wait()


def _peer_v_kernel(eid_ref, h_ref, gate_ref, x_ref, tab_ref, spread_ref, g_ref, b_ref, y_ref, a_hi, a_lo, mix, *bufs):
    a = gate_ref[...] * _gelu(h_ref[...])
    hi = a.astype(BF16)
    lo = (a - hi.astype(F32)).astype(BF16)
    a_hi[...] = jnp.dot(hi, spread_ref[...], preferred_element_type=F32)
    a_lo[...] = jnp.dot(lo, spread_ref[...], preferred_element_type=F32)
    mask = _chunk_mask()

    def compute(t, rows_ref):
        v = pltpu.bitcast(rows_ref[...], BF16)
        lhs = jnp.concatenate([(a_hi[pl.ds(t, 1), :] * mask).astype(BF16),
                               (a_lo[pl.ds(t, 1), :] * mask).astype(BF16)], axis=0)
        o = jnp.dot(lhs, v, preferred_element_type=F32)
        o8 = o[0:D_CHUNKS] + o[D_CHUNKS:2 * D_CHUNKS]
        mix[pl.ds(t, 1), :] = jnp.concatenate([o8[j:j + 1] for j in range(D_CHUNKS)], axis=1)

    _for_tokens_pipelined(eid_ref, tab_ref, bufs, compute, V_TOKENS_PER_TRIP)
    y_ref[...] = _ln(ALPHA * x_ref[...] + mix[...], g_ref[...], b_ref[...])


def _peer_specs():
    eid_spec = pl.BlockSpec((PEER_TILE, PEER_PICKS), lambda i: (i, 0), memory_space=pltpu.SMEM)
    tab_spec = pl.BlockSpec((PEER_EXPERTS * ROW_WORDS, 128), lambda i: (0, 0), pipeline_mode=pl.Buffered(1))
    bufs = [pltpu.VMEM((PEER_PICKS * ROW_WORDS, 128), jnp.int32)] * 2
    return eid_spec, tab_spec, bufs


def peer_route_u(x, wq, keys, codes, tab, fold):
    T = x.shape[0]
    n = T // PEER_TILE
    _, tab_spec, bufs = _peer_specs()
    cur = lambda i: (jnp.minimum(i, n - 1), 0)
    prev = lambda i: (jnp.maximum(i - 1, 0), 0)
    zeros = jnp.zeros((PEER_TILE, PEER_PICKS), jnp.int32)
    tile = (PEER_TILE, PEER_PICKS)
    return pl.pallas_call(
        _peer_route_u_kernel,
        grid=(n + 1,),
        in_specs=[pl.BlockSpec((PEER_TILE, D_MODEL), cur), pl.BlockSpec((PEER_TILE, D_MODEL), prev),
                  _full_spec(wq.shape), _full_spec(keys.shape), _full_spec(codes.shape), tab_spec,
                  _full_spec(fold.shape), _full_spec(zeros.shape)],
        out_specs=[pl.BlockSpec(tile, cur), pl.BlockSpec(tile, cur), pl.BlockSpec(tile, prev)],
        out_shape=[jax.ShapeDtypeStruct((T, PEER_PICKS), jnp.int32), jax.ShapeDtypeStruct((T, PEER_PICKS), F32),
                   jax.ShapeDtypeStruct((T, PEER_PICKS), F32)],
        scratch_shapes=[pltpu.VMEM((PEER_HEADS, PEER_TILE, PEER_DK), F32),
                        pltpu.VMEM((PEER_HEADS, PEER_TOPK, PEER_TILE), jnp.int32),
                        pltpu.VMEM((PEER_HEADS, PEER_TOPK, PEER_TILE), F32),
                        pltpu.VMEM(tile, jnp.int32), pltpu.SMEM(tile, jnp.int32),
                        pltpu.VMEM((PEER_TILE, PEER_PICKS * D_CHUNKS), F32),
                        pltpu.SemaphoreType.DMA(())] + bufs,
        compiler_params=_cparams(("arbitrary",)),
        name="peer_route_u",
    )(x, x, wq, keys, codes, tab, fold, zeros)


def peer_v(eid, hdn, gate, x, tab, spread, g, b, n_tokens):
    T = eid.shape[0]
    eid_spec, tab_spec, bufs = _peer_specs()
    return pl.pallas_call(
        _peer_v_kernel,
        grid=(n_tokens // PEER_TILE,),
        in_specs=[eid_spec, _row_spec(PEER_TILE, PEER_PICKS), _row_spec(PEER_TILE, PEER_PICKS),
                  _row_spec(PEER_TILE, D_MODEL), tab_spec, _full_spec(spread.shape), _full_spec(g.shape), _full_spec(b.shape)],
        out_specs=_row_spec(PEER_TILE, D_MODEL),
        out_shape=jax.ShapeDtypeStruct((T, D_MODEL), F32),
        scratch_shapes=[pltpu.VMEM((PEER_TILE, PEER_PICKS * D_CHUNKS), F32)] * 2 + [pltpu.VMEM((PEER_TILE, D_MODEL), F32)] + bufs,
        compiler_params=_cparams(("arbitrary",)),
        name="peer_v",
    )(eid, hdn, gate, x, tab, spread, g, b)


def _sc_gate_kernel(eid_ref, h_ref, gate_ref, a_ref, e_ref, *, expert0):
    a_ref[...] = gate_ref[...] * _gelu(h_ref[...])
    e_ref[...] = eid_ref[...] // ROW_WORDS + expert0


def sc_gate(eid, hdn, gate, row0, expert0):
    n = (eid.shape[0] - row0) // PEER_TILE
    tile = (PEER_TILE, PEER_PICKS)
    spec = pl.BlockSpec(tile, lambda i: (row0 // PEER_TILE + i, 0))
    return pl.pallas_call(
        functools.partial(_sc_gate_kernel, expert0=expert0),
        grid=(n,),
        in_specs=[spec, spec, spec],
        out_specs=[_row_spec(*tile), _row_spec(*tile)],
        out_shape=[jax.ShapeDtypeStruct((n * PEER_TILE, PEER_PICKS), F32),
                   jax.ShapeDtypeStruct((n * PEER_TILE, PEER_PICKS), jnp.int32)],
        compiler_params=_cparams(("parallel",)),
        name="sc_gate",
    )(eid, hdn, gate)


def sc_value_mix(experts, a_flat, tabs):
    ns = experts.shape[0]
    per = ns // SC_SUBCORES
    n_chunks = PEER_PICKS // SC_ROWS
    half = D_MODEL // 2
    mesh = plsc.VectorSubcoreMesh(core_axis_name="c", subcore_axis_name="s")

    @functools.partial(
        pl.kernel, out_type=jax.ShapeDtypeStruct((ns, D_MODEL), F32), mesh=mesh,
        scratch_types=[pltpu.VMEM((per, PEER_PICKS), jnp.int32), pltpu.VMEM((per * PEER_PICKS,), F32),
                       pltpu.VMEM((SC_ROWS, D_MODEL), F32), pltpu.VMEM((SC_ROWS, D_MODEL), F32),
                       pltpu.VMEM((D_MODEL,), F32), pltpu.SemaphoreType.DMA, pltpu.SemaphoreType.DMA],
        compiler_params=pltpu.CompilerParams(needs_layout_passes=False))
    def kern(e_hbm, a_hbm, tab_hbm, o_hbm, idx_v, a_v, rows0, rows1, acc_v, sem0, sem1):
        t0 = (lax.axis_index("c") * (SC_SUBCORES // 2) + lax.axis_index("s")) * per
        pltpu.sync_copy(e_hbm.at[pl.ds(t0, per)], idx_v)
        pltpu.sync_copy(a_hbm.at[pl.ds(t0 * PEER_PICKS, per * PEER_PICKS)], a_v)
        bufs = ((rows0, sem0), (rows1, sem1))

        def gather(g, slot):
            rows, sem = bufs[slot]
            idx = idx_v.at[g // n_chunks, pl.ds((g % n_chunks) * SC_ROWS, SC_ROWS)]
            return pltpu.make_async_copy(tab_hbm.at[idx], rows, sem)

        def consume(g, slot):
            k = g // n_chunks
            c = g % n_chunks
            rows, _ = bufs[slot]

            @pl.when(c == 0)
            def _():
                for d in range(D_MODEL // SC_LANES):
                    acc_v[pl.ds(d * SC_LANES, SC_LANES)] = jnp.zeros((SC_LANES,), F32)

            for base in (0, half):
                accs = tuple(acc_v[pl.ds(base + d * SC_LANES, SC_LANES)] for d in range(half // SC_LANES))

                def pick(p, accs, base=base):
                    lane_idx = jnp.zeros((SC_LANES,), jnp.int32) + (k * PEER_PICKS + c * SC_ROWS + p)
                    ap = plsc.load_gather(a_v, [lane_idx])
                    return tuple(acc + ap * rows[p, pl.ds(base + d * SC_LANES, SC_LANES)]
                                 for d, acc in enumerate(accs))

                accs = lax.fori_loop(0, SC_ROWS, pick, accs)
                for d, acc in enumerate(accs):
                    acc_v[pl.ds(base + d * SC_LANES, SC_LANES)] = acc

            @pl.when(c == n_chunks - 1)
            def _():
                pltpu.sync_copy(acc_v, o_hbm.at[t0 + k])

        total = per * n_chunks
        gather(0, 0).start()

        @pl.loop(0, total // 2)
        def _(j):
            g = 2 * j
            gather(g + 1, 1).start()
            gather(g, 0).wait()
            consume(g, 0)

            @pl.when(g + 2 < total)
            def _():
                gather(g + 2, 0).start()

            gather(g + 1, 1).wait()
            consume(g + 1, 1)

    return kern(experts, a_flat, tabs)


def _resid_ln_rows_kernel(x_ref, r_ref, g_ref, b_ref, y_in_ref, y_ref):
    del y_in_ref
    y_ref[...] = _ln(ALPHA * x_ref[...] + r_ref[...], g_ref[...], b_ref[...])


def resid_ln_rows(x, r, g, b, y, row0):
    n = r.shape[0] // ROW_TILE
    at = lambda i: (row0 // ROW_TILE + i, 0)
    return pl.pallas_call(
        _resid_ln_rows_kernel,
        grid=(n,),
        in_specs=[pl.BlockSpec((ROW_TILE, D_MODEL), at), _row_spec(ROW_TILE, D_MODEL), _full_spec(g.shape),
                  _full_spec(b.shape), pl.BlockSpec(memory_space=pl.ANY)],
        out_specs=pl.BlockSpec((ROW_TILE, D_MODEL), at),
        out_shape=jax.ShapeDtypeStruct(y.shape, y.dtype),
        input_output_aliases={4: 0},
        compiler_params=_cparams(("parallel",)),
        name="resid_ln_rows",
    )(x, r, g, b, y)


def _pack_kernel(t_ref, o_ref):
    n = t_ref.shape[1]
    for s in range(ROW_WORDS):
        lo = t_ref[0, :, (2 * s) * 128:(2 * s + 1) * 128].astype(BF16).astype(F32)
        hi = t_ref[0, :, (2 * s + 1) * 128:(2 * s + 2) * 128].astype(BF16).astype(F32)
        word = pltpu.bitcast(hi, jnp.uint32) | (pltpu.bitcast(lo, jnp.uint32) >> 16)
        o_ref[pl.ds(s, n, stride=ROW_WORDS), :] = pltpu.bitcast(word, jnp.int32)


def pack_table(tabs, layer):
    e = tabs.shape[1]
    return pl.pallas_call(
        _pack_kernel,
        grid=(e // PACK_TILE,),
        in_specs=[pl.BlockSpec((1, PACK_TILE, D_MODEL), lambda i: (layer, i, 0))],
        out_specs=_row_spec(PACK_TILE * ROW_WORDS, 128),
        out_shape=jax.ShapeDtypeStruct((e * ROW_WORDS, 128), jnp.int32),
        compiler_params=_cparams(("parallel",)),
        name="pack_table",
    )(tabs)


def peer_consts():
    col = jnp.arange(PEER_PICKS * D_CHUNKS) // D_CHUNKS
    spread = (col[None, :] == jnp.arange(PEER_PICKS)[:, None])
    return spread.astype(BF16), jnp.transpose(spread).astype(F32)


def peer_block(x, wq, keys, u_packed, v_packed, v_tabs, layer, spread, fold, g, b):
    T = x.shape[0]
    eid, gate, hdn = peer_route_u(x, wq, keys, pair_codes(), u_packed, fold)
    a_sc, e_sc = sc_gate(eid, hdn, gate, T - SC_TOKENS, layer * PEER_EXPERTS)
    mix_sc = sc_value_mix(e_sc, a_sc.reshape(-1), v_tabs.reshape(-1, D_MODEL))
    y = peer_v(eid, hdn, gate, x, v_packed, spread, g, b, T - SC_TOKENS)
    return resid_ln_rows(x, mix_sc, g, b, y, T - SC_TOKENS)


def _mla_kv_kernel(x_ref, w_ref, g_ref, cos_ref, sin_ref, c_ref, kpe_ref):
    kv = jnp.dot(x_ref[...].astype(BF16), w_ref[...], preferred_element_type=F32)
    c_ref[...] = _rms(kv[:, :KV_LORA], g_ref[...])
    kp = kv[:, KV_LORA:KV_LORA + QK_ROPE]
    sw = jnp.concatenate([kp[:, QK_ROPE // 2:], kp[:, :QK_ROPE // 2]], axis=1)
    kpe_ref[...] = kp * cos_ref[...] + sw * sin_ref[...]


def mla_kv(x, w, g, cos2, sin2):
    T = x.shape[0]
    return pl.pallas_call(
        _mla_kv_kernel,
        grid=(T // ROW_TILE,),
        in_specs=[_row_spec(ROW_TILE, D_MODEL), _full_spec(w.shape), _full_spec(g.shape),
                  _row_spec(ROW_TILE, QK_ROPE), _row_spec(ROW_TILE, QK_ROPE)],
        out_specs=[_row_spec(ROW_TILE, KV_LORA), _row_spec(ROW_TILE, QK_ROPE)],
        out_shape=[jax.ShapeDtypeStruct((T, KV_LORA), F32), jax.ShapeDtypeStruct((T, QK_ROPE), F32)],
        compiler_params=_cparams(("parallel",)),
        name="mla_kv",
    )(x, w, g, cos2, sin2)


def _mla_expand_kernel(c_ref, kpe_ref, wk_ref, wv_ref, k_ref, v_ref):
    cb = c_ref[...].astype(BF16)
    kn = jnp.dot(cb, wk_ref[...], preferred_element_type=F32)
    v_ref[...] = jnp.dot(cb, wv_ref[...], preferred_element_type=F32).astype(BF16)
    kp = kpe_ref[...]
    kpad = jnp.concatenate([kp, jnp.zeros_like(kp)], axis=1).astype(BF16)
    for h in range(MLA_HEADS):
        k_ref[:, h * HEAD_SLOT:h * HEAD_SLOT + QK_NOPE] = kn[:, h * QK_NOPE:(h + 1) * QK_NOPE].astype(BF16)
        k_ref[:, h * HEAD_SLOT + QK_NOPE:(h + 1) * HEAD_SLOT] = kpad


def mla_expand(c, kpe, wk, wv):
    R = c.shape[0]
    return pl.pallas_call(
        _mla_expand_kernel,
        grid=(R // ROW_TILE,),
        in_specs=[_row_spec(ROW_TILE, KV_LORA), _row_spec(ROW_TILE, QK_ROPE), _full_spec(wk.shape), _full_spec(wv.shape)],
        out_specs=[_row_spec(ROW_TILE, MLA_HEADS * HEAD_SLOT), _row_spec(ROW_TILE, MLA_HEADS * V_HEAD)],
        out_shape=[jax.ShapeDtypeStruct((R, MLA_HEADS * HEAD_SLOT), BF16), jax.ShapeDtypeStruct((R, MLA_HEADS * V_HEAD), BF16)],
        compiler_params=_cparams(("parallel",)),
        name="mla_expand",
    )(c, kpe, wk, wv)


def _mla_q_kernel(x_ref, wdq_ref, g_ref, wuq_ref, cos_ref, sin_ref, q_ref):
    cq = _rms(jnp.dot(x_ref[...].astype(BF16), wdq_ref[...], preferred_element_type=F32), g_ref[...])
    q = jnp.dot(cq.astype(BF16), wuq_ref[...], preferred_element_type=F32)
    q = q * (ATTN_SCALE * math.log2(math.e))
    lane = lax.broadcasted_iota(jnp.int32, (ROW_TILE, 128), 1)
    cos = cos_ref[...]
    sin = sin_ref[...]
    for h in range(MLA_HEADS):
        q_ref[:, h * HEAD_SLOT:h * HEAD_SLOT + QK_NOPE] = q[:, h * HEAD_SLOT:h * HEAD_SLOT + QK_NOPE].astype(BF16)
        seg = q[:, h * HEAD_SLOT + QK_NOPE:(h + 1) * HEAD_SLOT]
        sw = jnp.where(lane < QK_ROPE // 2, pltpu.roll(seg, 128 - QK_ROPE // 2, 1),
                       jnp.where(lane < QK_ROPE, pltpu.roll(seg, QK_ROPE // 2, 1), 0.0))
        q_ref[:, h * HEAD_SLOT + QK_NOPE:(h + 1) * HEAD_SLOT] = (seg * cos + sw * sin).astype(BF16)


def mla_q(x, wdq, g, wuq, cosq, sinq):
    T = x.shape[0]
    return pl.pallas_call(
        _mla_q_kernel,
        grid=(T // ROW_TILE,),
        in_specs=[_row_spec(ROW_TILE, D_MODEL), _full_spec(wdq.shape), _full_spec(g.shape), _full_spec(wuq.shape),
                  _row_spec(ROW_TILE, 128), _row_spec(ROW_TILE, 128)],
        out_specs=_row_spec(ROW_TILE, MLA_HEADS * HEAD_SLOT),
        out_shape=jax.ShapeDtypeStruct((T, MLA_HEADS * HEAD_SLOT), BF16),
        compiler_params=_cparams(("parallel",)),
        name="mla_q",
    )(x, wdq, g, wuq, cosq, sinq)


def _attn_prompt_kernel(q_ref, k_ref, v_ref, o_ref, s_even, s_odd):
    qi = pl.program_id(2)
    q = q_ref[...]
    nt = (((1,), (1,)), ((), ()))

    def scores(ki, dst):
        off = pl.multiple_of(ki * ATTN_TILE, ATTN_TILE)
        dst[...] = lax.dot_general(q, k_ref[pl.ds(off, ATTN_TILE), :], nt, preferred_element_type=F32)

    def update(ki, src, carry, diagonal):
        m, l, acc = carry
        off = pl.multiple_of(ki * ATTN_TILE, ATTN_TILE)
        s = src[...]
        if diagonal:
            qc = lax.broadcasted_iota(jnp.int32, s.shape, 0) // CHUNK
            kc = lax.broadcasted_iota(jnp.int32, s.shape, 1) // CHUNK
            s = jnp.where(kc <= qc, s, -jnp.inf)
        m_new = jnp.maximum(m, jnp.max(s, axis=1, keepdims=True))
        alpha = jnp.exp2(m - m_new)
        p = jnp.exp2(s - m_new)
        l = alpha * l + jnp.sum(p, axis=1, keepdims=True)
        acc = alpha * acc + jnp.dot(p.astype(BF16), v_ref[pl.ds(off, ATTN_TILE), :], preferred_element_type=F32)
        return m_new, l, acc

    init = (jnp.full((ATTN_TILE, 1), -jnp.inf, F32), jnp.zeros((ATTN_TILE, 1), F32), jnp.zeros((ATTN_TILE, V_HEAD), F32))
    scores(0, s_even)

    def pair(j, carry):
        scores(2 * j + 1, s_odd)
        carry = update(2 * j, s_even, carry, False)
        scores(jnp.minimum(2 * j + 2, qi), s_even)
        return update(2 * j + 1, s_odd, carry, False)

    carry = lax.fori_loop(0, qi // 2, pair, init)

    def last_even(carry):
        return update(qi, s_even, carry, True)

    def last_odd(carry):
        scores(qi, s_odd)
        carry = update(qi - 1, s_even, carry, False)
        return update(qi, s_odd, carry, True)

    m, l, acc = lax.cond(qi % 2 == 0, last_even, last_odd, carry)
    o_ref[...] = (acc / l).astype(BF16)


def attn_prompt(q, k, v, batch, seq):
    nq = seq // ATTN_TILE
    return pl.pallas_call(
        _attn_prompt_kernel,
        grid=(batch, MLA_HEADS, nq),
        in_specs=[pl.BlockSpec((ATTN_TILE, HEAD_SLOT), lambda b, h, i: (b * nq + i, h)),
                  pl.BlockSpec((seq, HEAD_SLOT), lambda b, h, i: (b, h)),
                  pl.BlockSpec((seq, V_HEAD), lambda b, h, i: (b, h))],
        out_specs=pl.BlockSpec((ATTN_TILE, V_HEAD), lambda b, h, i: (b * nq + i, h)),
        out_shape=jax.ShapeDtypeStruct((batch * seq, MLA_HEADS * V_HEAD), BF16),
        scratch_shapes=[pltpu.VMEM((ATTN_TILE, ATTN_TILE), F32)] * 2,
        compiler_params=_cparams(("parallel", "parallel", "arbitrary")),
        name="attn_prompt",
    )(q, k, v)


def _attn_sample_kernel(q_ref, k_ref, v_ref, o_ref):
    nt = (((1,), (1,)), ((), ()))
    s = lax.dot_general(q_ref[...], k_ref[...], nt, preferred_element_type=F32)
    p = jnp.exp2(s - jnp.max(s, axis=1, keepdims=True))
    o = jnp.dot(p.astype(BF16), v_ref[...], preferred_element_type=F32)
    o_ref[...] = (o / jnp.sum(p, axis=1, keepdims=True)).astype(BF16)


def attn_sample(q, k, v, batch, q_len, kv_len, q_block0):
    return pl.pallas_call(
        _attn_sample_kernel,
        grid=(batch, MLA_HEADS),
        in_specs=[pl.BlockSpec((q_len, HEAD_SLOT), lambda b, h: (q_block0 + b, h)),
                  pl.BlockSpec((kv_len, HEAD_SLOT), lambda b, h: (b, h)),
                  pl.BlockSpec((kv_len, V_HEAD), lambda b, h: (b, h))],
        out_specs=pl.BlockSpec((q_len, V_HEAD), lambda b, h: (b, h)),
        out_shape=jax.ShapeDtypeStruct((batch * q_len, MLA_HEADS * V_HEAD), BF16),
        compiler_params=_cparams(("parallel", "parallel")),
        name="attn_sample",
    )(q, k, v)


def _proj_ln_kernel(x_ref, o_ref_in, w_ref, g_ref, b_ref, y_ref):
    mix = jnp.dot(o_ref_in[...], w_ref[...], preferred_element_type=F32)
    y_ref[...] = _ln(ALPHA * x_ref[...] + mix, g_ref[...], b_ref[...])


def proj_ln(x, o, w, g, b):
    T = x.shape[0]
    return pl.pallas_call(
        _proj_ln_kernel,
        grid=(T // ROW_TILE,),
        in_specs=[_row_spec(ROW_TILE, D_MODEL), _row_spec(ROW_TILE, o.shape[1]), _full_spec(w.shape),
                  _full_spec(g.shape), _full_spec(b.shape)],
        out_specs=_row_spec(ROW_TILE, D_MODEL),
        out_shape=jax.ShapeDtypeStruct((T, D_MODEL), F32),
        compiler_params=_cparams(("parallel",)),
        name="proj_ln",
    )(x, o, w, g, b)


def _rope_tables(pos):
    inv = 1.0 / (ROPE_BASE ** (jnp.arange(0, QK_ROPE, 2, dtype=F32) / QK_ROPE))
    ang = pos.astype(F32)[:, None] * inv[None, :]
    cos, sin = jnp.cos(ang), jnp.sin(ang)
    cos2 = jnp.concatenate([cos, cos], axis=1)
    sin2 = jnp.concatenate([-sin, sin], axis=1)
    pad = jnp.zeros_like(cos2)
    return cos2, sin2, jnp.concatenate([cos2, pad], axis=1), jnp.concatenate([sin2, pad], axis=1)


def _gm_chunk_mats(w_s, b_s, dec_seq):
    i = jnp.arange(GM_CHUNK)
    mask = (i[None, :] // CHUNK) <= (i[:, None] // CHUNK)
    full = jnp.where(mask[None], w_s, 0.0)
    rep = GM_CHUNK // dec_seq
    blk = w_s[:, :dec_seq, :dec_seq]
    same = (i[:, None] // dec_seq) == (i[None, :] // dec_seq)
    diag = jnp.where(same[None], jnp.tile(blk, (1, rep, rep)), 0.0)
    ws = jnp.stack([full, diag], axis=0).astype(BF16)
    bias = jnp.stack([b_s, jnp.tile(b_s[:, :dec_seq], (1, rep))], axis=0)
    bs = jnp.broadcast_to(bias[..., None], bias.shape + (GM_GROUP_DIM,)).astype(F32)
    return ws, bs


def _pad_q_weight(w_uq):
    w = w_uq.reshape(Q_LORA, MLA_HEADS, QK_NOPE + QK_ROPE)
    w = jnp.pad(w, ((0, 0), (0, 0), (0, HEAD_SLOT - QK_NOPE - QK_ROPE)))
    return w.reshape(Q_LORA, MLA_HEADS * HEAD_SLOT).astype(BF16)


def kernel(x_prompt, x_sample, cache_ckv, cache_kpe, ln1_g, ln1_b, ln2_g, ln2_b, gm_w_in, gm_b_in, gm_ln_g, gm_ln_b,
           gm_w_s, gm_b_s, gm_w_out, mla_w_dkv, mla_kv_norm_g, mla_w_ukv, mla_w_dq, mla_q_norm_g, mla_w_uq, mla_w_o,
           peer_w_q, peer_subkeys, peer_u, peer_v):
    B, S, _ = x_prompt.shape
    DB, DS, _ = x_sample.shape
    PAST = cache_ckv.shape[1]
    TP, TS = B * S, DB * DS
    T = TP + TS
    assert TP % ROW_TILE == 0 and TS % ROW_TILE == 0 and T % PEER_TILE == 0 and S % ATTN_TILE == 0
    assert SC_TOKENS % (SC_SUBCORES * 2) == 0 and SC_TOKENS % ROW_TILE == 0 and (T - SC_TOKENS) % ROW_TILE == 0
    assert GM_CHUNK % DS == 0 and DS <= CHUNK and PAST % CHUNK == 0 and (PAST + DS) % 16 == 0

    x = jnp.concatenate([x_prompt.reshape(TP, D_MODEL), x_sample.reshape(TS, D_MODEL)], axis=0)
    pos = jnp.concatenate([jnp.tile(jnp.arange(S, dtype=jnp.int32), B),
                           jnp.tile(PAST + jnp.arange(DS, dtype=jnp.int32), DB)])
    cos2, sin2, cosq, sinq = _rope_tables(pos)
    spread, fold = peer_consts()
    row = lambda a: a.reshape(1, -1)

    def peer_layer(x, l):
        return peer_block(x, peer_w_q[l].astype(BF16), peer_subkeys[l].astype(BF16), pack_table(peer_u, l),
                          pack_table(peer_v, l), peer_v, l, spread, fold, row(ln2_g[l]), row(ln2_b[l]))

    gm_rows = []
    for l in range(N_A):
        u, v = gm_in(x, gm_w_in[l].astype(BF16), row(gm_b_in[l]), row(gm_ln_g[l]), row(gm_ln_b[l]))
        gm_rows.append(v[TP:].reshape(DB, DS, GM_HALF))
        ws, bs = _gm_chunk_mats(gm_w_s[l], gm_b_s[l], DS)
        x = gm_mix(x, u, v, ws, bs, gm_w_out[l].astype(BF16), row(ln1_g[l]), row(ln1_b[l]), TP // ROW_TILE)
        x = peer_layer(x, l)

    w_dkv = jnp.pad(mla_w_dkv, ((0, 0), (0, 128 - QK_ROPE))).astype(BF16)
    c_new, kpe_new = mla_kv(x, w_dkv, row(mla_kv_norm_g), cos2, sin2)
    w_ukv = mla_w_ukv.reshape(KV_LORA, MLA_HEADS, QK_NOPE + V_HEAD)
    w_uk = w_ukv[:, :, :QK_NOPE].reshape(KV_LORA, MLA_HEADS * QK_NOPE).astype(BF16)
    w_uv = w_ukv[:, :, QK_NOPE:].reshape(KV_LORA, MLA_HEADS * V_HEAD).astype(BF16)
    kp_cat, vp = mla_expand(c_new[:TP], kpe_new[:TP], w_uk, w_uv)
    KV = PAST + DS
    c_all = jnp.concatenate([cache_ckv, c_new[TP:].reshape(DB, DS, KV_LORA)], axis=1).reshape(DB * KV, KV_LORA)
    kpe_all = jnp.concatenate([cache_kpe, kpe_new[TP:].reshape(DB, DS, QK_ROPE)], axis=1).reshape(DB * KV, QK_ROPE)
    ks_cat, vs = mla_expand(c_all, kpe_all, w_uk, w_uv)

    for j in range(DEPTH - N_A):
        l = N_A + j
        q = mla_q(x, mla_w_dq[j].astype(BF16), row(mla_q_norm_g[j]), _pad_q_weight(mla_w_uq[j]), cosq, sinq)
        o = jnp.concatenate([attn_prompt(q, kp_cat, vp, B, S),
                             attn_sample(q, ks_cat, vs, DB, DS, KV, TP // DS)], axis=0)
        x = proj_ln(x, o, mla_w_o[j].astype(BF16), row(ln1_g[l]), row(ln1_b[l]))
        x = peer_layer(x, l)

    return (x[:TP].reshape(B, S, D_MODEL), x[TP:].reshape(DB, DS, D_MODEL), jnp.stack(gm_rows, axis=0),
            c_new[:TP].reshape(B, S, KV_LORA), kpe_new[:TP].reshape(B, S, QK_ROPE),
            c_new[TP:].reshape(DB, DS, KV_LORA), kpe_new[TP:].reshape(DB, DS, QK_ROPE))
```

```python
import functools
import math

import jax
import jax.numpy as jnp
from jax import lax
from jax.experimental import pallas as pl
from jax.experimental.pallas import tpu as pltpu
from jax.experimental.pallas import tpu_sc as plsc

F32 = jnp.float32
BF16 = jnp.bfloat16

D_MODEL = 1024
DEPTH = 4
N_A = DEPTH // 2
CHUNK = 64
ALPHA = (2.0 * DEPTH) ** 0.25
LN_EPS = 1e-5
RMS_EPS = 1e-6

GM_CHUNK = 128
GM_HALF = 2 * D_MODEL
GM_GROUPS = 8
GM_GROUP_DIM = GM_HALF // GM_GROUPS

MLA_HEADS = 8
QK_NOPE = 128
QK_ROPE = 64
V_HEAD = 128
KV_LORA = D_MODEL // 4
Q_LORA = 3 * D_MODEL // 8
ROPE_BASE = 10000.0
ATTN_SCALE = (QK_NOPE + QK_ROPE) ** -0.5
HEAD_SLOT = 256

PEER_HEADS = 8
PEER_NKEYS = 128
PEER_EXPERTS = PEER_NKEYS * PEER_NKEYS
PEER_DK = 256
PEER_TOPK = 16
PEER_PICKS = PEER_HEADS * PEER_TOPK
TOPK_SHIFT = PEER_TOPK.bit_length() - 1
PAIR_SENTINEL = 1 << 20
ROW_WORDS = 4
D_CHUNKS = D_MODEL // 128

ROW_TILE = 256
PEER_TILE = 128
PACK_TILE = 512
SC_U_TOKENS = 12288
SC_V_TOKENS = 9216
SC_BATCH = 32
SC_SUBCORES = 32
SC_LANES = 16
SC_ROWS = 16
V_TOKENS_PER_TRIP = 8
ATTN_TILE = 512
VMEM_LIMIT = 56 * 1024 * 1024


def _cparams(sem):
    return pltpu.CompilerParams(dimension_semantics=sem, vmem_limit_bytes=VMEM_LIMIT)


def _gelu(x):
    return 0.5 * x * (1.0 + lax.erf(x * (1.0 / math.sqrt(2.0))))


def _ln(x, g, b):
    mu = jnp.mean(x, axis=-1, keepdims=True)
    xc = x - mu
    var = jnp.mean(xc * xc, axis=-1, keepdims=True)
    return xc * lax.rsqrt(var + LN_EPS) * g + b


def _rms(x, g):
    return x * lax.rsqrt(jnp.mean(x * x, axis=-1, keepdims=True) + RMS_EPS) * g


def _row_spec(tile, width):
    return pl.BlockSpec((tile, width), lambda i: (i, 0))


def _full_spec(shape):
    nd = len(shape)
    return pl.BlockSpec(shape, lambda i: (0,) * nd)


def _gm_in_kernel(x_ref, w_ref, b_ref, lg_ref, lb_ref, u_ref, v_ref):
    z = jnp.dot(x_ref[...].astype(BF16), w_ref[...], preferred_element_type=F32) + b_ref[...]
    z = _gelu(z)
    u_ref[...] = z[:, :GM_HALF]
    v_ref[...] = _ln(z[:, GM_HALF:], lg_ref[...], lb_ref[...])


def gm_in(x, w, b, lg, lb):
    T = x.shape[0]
    return pl.pallas_call(
        _gm_in_kernel,
        grid=(T // ROW_TILE,),
        in_specs=[_row_spec(ROW_TILE, D_MODEL), _full_spec(w.shape), _full_spec(b.shape),
                  _full_spec(lg.shape), _full_spec(lb.shape)],
        out_specs=[_row_spec(ROW_TILE, GM_HALF), _row_spec(ROW_TILE, GM_HALF)],
        out_shape=[jax.ShapeDtypeStruct((T, GM_HALF), F32)] * 2,
        compiler_params=_cparams(("parallel",)),
        name="gm_in",
    )(x, w, b, lg, lb)


def _gm_mix_kernel(x_ref, u_ref, v_ref, ws_ref, bs_ref, wo_ref, g_ref, b_ref, o_ref, s_scr):
    for c in range(ROW_TILE // GM_CHUNK):
        r = slice(c * GM_CHUNK, (c + 1) * GM_CHUNK)
        for g in range(GM_GROUPS):
            cs = slice(g * GM_GROUP_DIM, (g + 1) * GM_GROUP_DIM)
            sv = jnp.dot(ws_ref[0, g], v_ref[r, cs].astype(BF16), preferred_element_type=F32) + bs_ref[0, g]
            s_scr[r, cs] = (u_ref[r, cs] * sv).astype(BF16)
    mix = jnp.dot(s_scr[...], wo_ref[...], preferred_element_type=F32)
    o_ref[...] = _ln(ALPHA * x_ref[...] + mix, g_ref[...], b_ref[...])


def gm_mix(x, u, v, ws, bs, wo, g, b, n_prompt_tiles):
    T = x.shape[0]
    sel = lambda i: (jnp.where(i >= n_prompt_tiles, 1, 0), 0, 0, 0)
    return pl.pallas_call(
        _gm_mix_kernel,
        grid=(T // ROW_TILE,),
        in_specs=[_row_spec(ROW_TILE, D_MODEL), _row_spec(ROW_TILE, GM_HALF), _row_spec(ROW_TILE, GM_HALF),
                  pl.BlockSpec((1,) + ws.shape[1:], sel), pl.BlockSpec((1,) + bs.shape[1:], sel),
                  _full_spec(wo.shape), _full_spec(g.shape), _full_spec(b.shape)],
        out_specs=_row_spec(ROW_TILE, D_MODEL),
        out_shape=jax.ShapeDtypeStruct((T, D_MODEL), F32),
        scratch_shapes=[pltpu.VMEM((ROW_TILE, GM_HALF), BF16)],
        compiler_params=_cparams(("parallel",)),
        name="gm_mix",
    )(x, u, v, ws, bs, wo, g, b)


def _extract_max(s, codes, sentinel):
    m = jnp.max(s, axis=0, keepdims=True)
    c = jnp.min(jnp.where(s == m, codes, sentinel), axis=0, keepdims=True)
    return m, c, jnp.where(codes == c, -jnp.inf, s)


def _topk_rows(s):
    rows = lax.broadcasted_iota(jnp.int32, s.shape, 0).astype(F32)
    vals, idxs = [], []
    for _ in range(PEER_TOPK):
        m, idx, s = _extract_max(s, rows, float(s.shape[0]))
        vals.append(m)
        idxs.append(idx)
    return jnp.concatenate(vals, axis=0), jnp.concatenate(idxs, axis=0).astype(jnp.int32)


_PAIR_GROUPS = ((None, 0, 16), (None, 1, 8), (0, None, 16), (1, None, 8), (2, None, 8), (3, None, 8), (4, None, 8))


def pair_codes():
    codes = []
    for b_fix, a_fix, n in _PAIR_GROUPS:
        for r in range(n):
            a, b = (a_fix, r) if b_fix is None else (r, b_fix)
            first = a_fix is not None or a >= 2
            ok = first and (a + 1) * (b + 1) <= PEER_TOPK
            codes.append(a * PEER_TOPK + b if ok else PAIR_SENTINEL)
    assert sorted(c for c in codes if c != PAIR_SENTINEL) == sorted(
        a * PEER_TOPK + b for a in range(PEER_TOPK) for b in range(PEER_TOPK) if (a + 1) * (b + 1) <= PEER_TOPK)
    return jnp.broadcast_to(jnp.asarray(codes, F32)[:, None], (len(codes), PEER_TILE))


def _topk_pairs(v1, i1, v2, i2, codes):
    parts = []
    for b_fix, a_fix, n in _PAIR_GROUPS:
        if b_fix is None:
            parts.append(v1[a_fix:a_fix + 1] + v2[0:n])
        else:
            parts.append(v1[0:n] + v2[b_fix:b_fix + 1])
    cand = jnp.where(codes < PAIR_SENTINEL, jnp.concatenate(parts, axis=0), -jnp.inf)
    krow = lax.broadcasted_iota(jnp.int32, v1.shape, 0)
    vals, k1, k2 = [], [], []
    for _ in range(PEER_TOPK):
        m, c, cand = _extract_max(cand, codes, float(PAIR_SENTINEL))
        vals.append(m)
        c = c.astype(jnp.int32)
        k1.append(jnp.sum(jnp.where(krow == (c >> TOPK_SHIFT), i1, 0), axis=0, keepdims=True))
        k2.append(jnp.sum(jnp.where(krow == (c & (PEER_TOPK - 1)), i2, 0), axis=0, keepdims=True))
    return jnp.concatenate(vals, axis=0), jnp.concatenate(k1, axis=0), jnp.concatenate(k2, axis=0)


def _peer_route_kernel(x_ref, wq_ref, k_ref, codes_ref, eid_ref, gate_ref, expert_ref, *, expert0):
    q = jnp.dot(x_ref[...].astype(BF16), wq_ref[...], preferred_element_type=F32)
    half = PEER_DK // 2
    nt = (((1,), (1,)), ((), ()))
    codes = codes_ref[...]
    experts, gates = [], []
    for h in range(PEER_HEADS):
        q1 = q[:, h * PEER_DK:h * PEER_DK + half].astype(BF16)
        q2 = q[:, h * PEER_DK + half:(h + 1) * PEER_DK].astype(BF16)
        s1 = lax.dot_general(k_ref[0], q1, nt, preferred_element_type=F32)
        s2 = lax.dot_general(k_ref[1], q2, nt, preferred_element_type=F32)
        v1, i1 = _topk_rows(s1)
        v2, i2 = _topk_rows(s2)
        sc, e1, e2 = _topk_pairs(v1, i1, v2, i2, codes)
        e = jnp.exp(sc - sc[0:1])
        gates.append(e / jnp.sum(e, axis=0, keepdims=True))
        experts.append(e1 * PEER_NKEYS + e2)
    expert = jnp.transpose(jnp.concatenate(experts, axis=0))
    eid_ref[...] = expert * ROW_WORDS
    expert_ref[...] = expert + expert0
    gate_ref[...] = jnp.transpose(jnp.concatenate(gates, axis=0))


def peer_route(x, wq, keys, codes, row0, expert0):
    n = (x.shape[0] - row0) // PEER_TILE
    tile = (PEER_TILE, PEER_PICKS)
    return pl.pallas_call(
        functools.partial(_peer_route_kernel, expert0=expert0),
        grid=(n,),
        in_specs=[pl.BlockSpec((PEER_TILE, D_MODEL), lambda i: (row0 // PEER_TILE + i, 0)), _full_spec(wq.shape),
                  _full_spec(keys.shape), _full_spec(codes.shape)],
        out_specs=[_row_spec(*tile), _row_spec(*tile), _row_spec(*tile)],
        out_shape=[jax.ShapeDtypeStruct((n * PEER_TILE, PEER_PICKS), jnp.int32),
                   jax.ShapeDtypeStruct((n * PEER_TILE, PEER_PICKS), F32),
                   jax.ShapeDtypeStruct((n * PEER_TILE, PEER_PICKS), jnp.int32)],
        compiler_params=_cparams(("parallel",)),
        name="peer_route",
    )(x, wq, keys, codes)


def _gather_rows(eid_ref, t, tab_ref, dst_ref):
    picks = eid_ref.at[t]
    for p in range(PEER_PICKS):
        row = pl.multiple_of(picks[p], ROW_WORDS)
        dst_ref[pl.ds(ROW_WORDS * p, ROW_WORDS), :] = tab_ref[pl.ds(row, ROW_WORDS), :]


def _chunk_mask():
    sub = lax.broadcasted_iota(jnp.int32, (D_CHUNKS, PEER_PICKS * D_CHUNKS), 0)
    col = lax.broadcasted_iota(jnp.int32, (D_CHUNKS, PEER_PICKS * D_CHUNKS), 1)
    return ((col & (D_CHUNKS - 1)) == sub).astype(F32)


def _for_tokens_pipelined(eid_ref, tab_ref, bufs, compute, tokens_per_trip):
    _gather_rows(eid_ref, 0, tab_ref, bufs[0])

    def body(i, carry):
        t0 = tokens_per_trip * i
        for k in range(tokens_per_trip):
            _gather_rows(eid_ref, jnp.minimum(t0 + k + 1, PEER_TILE - 1), tab_ref, bufs[(k + 1) % 2])
            compute(t0 + k, bufs[k % 2])
        return carry

    lax.fori_loop(0, PEER_TILE // tokens_per_trip, body, 0)


def _peer_route_u_kernel(xr_ref, xc_ref, wq_ref, k_ref, codes_ref, tab_ref, fold_ref, zeros_ref,
                         eid_ref, gate_ref, h_ref, q_scr, eid_t, gate_t, eid_vmem, eid_smem, hrep, sem, *bufs):
    i = pl.program_id(0)
    half = PEER_DK // 2
    nt = (((1,), (1,)), ((), ()))
    eid_to_smem = pltpu.make_async_copy(eid_vmem, eid_smem, sem)

    @pl.when(i == 0)
    def _():
        pltpu.sync_copy(zeros_ref, eid_smem)

    @pl.when(i > 0)
    def _():
        eid_to_smem.wait()

    q = jnp.dot(xr_ref[...].astype(BF16), wq_ref[...], preferred_element_type=F32)
    for h in range(PEER_HEADS):
        q_scr[h] = q[:, h * PEER_DK:(h + 1) * PEER_DK]
    codes = codes_ref[...]
    mask = _chunk_mask()

    def compute(t, rows_ref):
        u = pltpu.bitcast(rows_ref[...], BF16)
        xr = xc_ref[pl.ds(t, 1), :]
        x8 = jnp.concatenate([xr[:, j * 128:(j + 1) * 128] for j in range(D_CHUNKS)], axis=0)
        hi = x8.astype(BF16)
        lo = (x8 - hi.astype(F32)).astype(BF16)
        o = lax.dot_general(jnp.concatenate([hi, lo], axis=0), u, nt, preferred_element_type=F32)
        o8 = (o[0:D_CHUNKS] + o[D_CHUNKS:2 * D_CHUNKS]) * mask
        hrep[pl.ds(t, 1), :] = jnp.sum(o8, axis=0, keepdims=True)

    per_trip = PEER_TILE // PEER_HEADS
    _gather_rows(eid_smem, 0, tab_ref, bufs[0])

    def body(h, carry):
        qh = q_scr[h]
        s1 = lax.dot_general(k_ref[0], qh[:, :half].astype(BF16), nt, preferred_element_type=F32)
        s2 = lax.dot_general(k_ref[1], qh[:, half:].astype(BF16), nt, preferred_element_type=F32)
        v1, i1 = _topk_rows(s1)
        v2, i2 = _topk_rows(s2)
        sc, e1, e2 = _topk_pairs(v1, i1, v2, i2, codes)
        e = jnp.exp(sc - sc[0:1])
        gate_t[h] = e / jnp.sum(e, axis=0, keepdims=True)
        eid_t[h] = (e1 * PEER_NKEYS + e2) * ROW_WORDS
        t0 = per_trip * h
        for k in range(per_trip):
            _gather_rows(eid_smem, jnp.minimum(t0 + k + 1, PEER_TILE - 1), tab_ref, bufs[(k + 1) % 2])
            compute(t0 + k, bufs[k % 2])
        return carry

    lax.fori_loop(0, PEER_HEADS, body, 0)
    h_ref[...] = jnp.dot(hrep[...], fold_ref[...], precision=lax.Precision.HIGHEST, preferred_element_type=F32)
    eid = jnp.transpose(eid_t[...].reshape(PEER_PICKS, PEER_TILE))
    eid_ref[...] = eid
    eid_vmem[...] = eid
    gate_ref[...] = jnp.transpose(gate_t[...].reshape(PEER_PICKS, PEER_TILE))
    eid_to_smem.start()

    @pl.when(i == pl.num_programs(0) - 1)
    def _():
        eid_to_smem.wait()


def _peer_v_kernel(eid_ref, h_ref, gate_ref, x_ref, tab_ref, spread_ref, g_ref, b_ref, y_ref, a_hi, a_lo, mix, *bufs):
    a = gate_ref[...] * _gelu(h_ref[...])
    hi = a.astype(BF16)
    lo = (a - hi.astype(F32)).astype(BF16)
    a_hi[...] = jnp.dot(hi, spread_ref[...], preferred_element_type=F32)
    a_lo[...] = jnp.dot(lo, spread_ref[...], preferred_element_type=F32)
    mask = _chunk_mask()

    def compute(t, rows_ref):
        v = pltpu.bitcast(rows_ref[...], BF16)
        lhs = jnp.concatenate([(a_hi[pl.ds(t, 1), :] * mask).astype(BF16),
                               (a_lo[pl.ds(t, 1), :] * mask).astype(BF16)], axis=0)
        o = jnp.dot(lhs, v, preferred_element_type=F32)
        o8 = o[0:D_CHUNKS] + o[D_CHUNKS:2 * D_CHUNKS]
        mix[pl.ds(t, 1), :] = jnp.concatenate([o8[j:j + 1] for j in range(D_CHUNKS)], axis=1)

    _for_tokens_pipelined(eid_ref, tab_ref, bufs, compute, V_TOKENS_PER_TRIP)
    y_ref[...] = _ln(ALPHA * x_ref[...] + mix[...], g_ref[...], b_ref[...])


def _peer_specs():
    eid_spec = pl.BlockSpec((PEER_TILE, PEER_PICKS), lambda i: (i, 0), memory_space=pltpu.SMEM)
    tab_spec = pl.BlockSpec((PEER_EXPERTS * ROW_WORDS, 128), lambda i: (0, 0), pipeline_mode=pl.Buffered(1))
    bufs = [pltpu.VMEM((PEER_PICKS * ROW_WORDS, 128), jnp.int32)] * 2
    return eid_spec, tab_spec, bufs


def peer_route_u(x, wq, keys, codes, tab, fold, n_tokens):
    T = n_tokens
    n = T // PEER_TILE
    _, tab_spec, bufs = _peer_specs()
    cur = lambda i: (jnp.minimum(i, n - 1), 0)
    prev = lambda i: (jnp.maximum(i - 1, 0), 0)
    zeros = jnp.zeros((PEER_TILE, PEER_PICKS), jnp.int32)
    tile = (PEER_TILE, PEER_PICKS)
    return pl.pallas_call(
        _peer_route_u_kernel,
        grid=(n + 1,),
        in_specs=[pl.BlockSpec((PEER_TILE, D_MODEL), cur), pl.BlockSpec((PEER_TILE, D_MODEL), prev),
                  _full_spec(wq.shape), _full_spec(keys.shape), _full_spec(codes.shape), tab_spec,
                  _full_spec(fold.shape), _full_spec(zeros.shape)],
        out_specs=[pl.BlockSpec(tile, cur), pl.BlockSpec(tile, cur), pl.BlockSpec(tile, prev)],
        out_shape=[jax.ShapeDtypeStruct((T, PEER_PICKS), jnp.int32), jax.ShapeDtypeStruct((T, PEER_PICKS), F32),
                   jax.ShapeDtypeStruct((T, PEER_PICKS), F32)],
        scratch_shapes=[pltpu.VMEM((PEER_HEADS, PEER_TILE, PEER_DK), F32),
                        pltpu.VMEM((PEER_HEADS, PEER_TOPK, PEER_TILE), jnp.int32),
                        pltpu.VMEM((PEER_HEADS, PEER_TOPK, PEER_TILE), F32),
                        pltpu.VMEM(tile, jnp.int32), pltpu.SMEM(tile, jnp.int32),
                        pltpu.VMEM((PEER_TILE, PEER_PICKS * D_CHUNKS), F32),
                        pltpu.SemaphoreType.DMA(())] + bufs,
        compiler_params=_cparams(("arbitrary",)),
        name="peer_route_u",
    )(x, x, wq, keys, codes, tab, fold, zeros)


def peer_v(eid, hdn, gate, x, tab, spread, g, b, n_tokens):
    T = eid.shape[0]
    eid_spec, tab_spec, bufs = _peer_specs()
    return pl.pallas_call(
        _peer_v_kernel,
        grid=(n_tokens // PEER_TILE,),
        in_specs=[eid_spec, _row_spec(PEER_TILE, PEER_PICKS), _row_spec(PEER_TILE, PEER_PICKS),
                  _row_spec(PEER_TILE, D_MODEL), tab_spec, _full_spec(spread.shape), _full_spec(g.shape), _full_spec(b.shape)],
        out_specs=_row_spec(PEER_TILE, D_MODEL),
        out_shape=jax.ShapeDtypeStruct((T, D_MODEL), F32),
        scratch_shapes=[pltpu.VMEM((PEER_TILE, PEER_PICKS * D_CHUNKS), F32)] * 2 + [pltpu.VMEM((PEER_TILE, D_MODEL), F32)] + bufs,
        compiler_params=_cparams(("arbitrary",)),
        name="peer_v",
    )(eid, hdn, gate, x, tab, spread, g, b)


def _sc_gate_kernel(eid_ref, h_ref, gate_ref, a_ref, e_ref, *, expert0):
    a_ref[...] = gate_ref[...] * _gelu(h_ref[...])
    e_ref[...] = eid_ref[...] // ROW_WORDS + expert0


def sc_gate(eid, hdn, gate, row0, expert0):
    n = (eid.shape[0] - row0) // PEER_TILE
    tile = (PEER_TILE, PEER_PICKS)
    spec = pl.BlockSpec(tile, lambda i: (row0 // PEER_TILE + i, 0))
    return pl.pallas_call(
        functools.partial(_sc_gate_kernel, expert0=expert0),
        grid=(n,),
        in_specs=[spec, spec, spec],
        out_specs=[_row_spec(*tile), _row_spec(*tile)],
        out_shape=[jax.ShapeDtypeStruct((n * PEER_TILE, PEER_PICKS), F32),
                   jax.ShapeDtypeStruct((n * PEER_TILE, PEER_PICKS), jnp.int32)],
        compiler_params=_cparams(("parallel",)),
        name="sc_gate",
    )(eid, hdn, gate)


def _sc_pipeline(idx_v, tab_hbm, bufs, consume):
    n_chunks = PEER_PICKS // SC_ROWS

    def gather(g, slot):
        rows, sem = bufs[slot]
        idx = idx_v.at[g // n_chunks, pl.ds((g % n_chunks) * SC_ROWS, SC_ROWS)]
        return pltpu.make_async_copy(tab_hbm.at[idx], rows, sem)

    total = SC_BATCH * n_chunks
    gather(0, 0).start()

    @pl.loop(0, total // 2)
    def _(j):
        g = 2 * j
        gather(g + 1, 1).start()
        gather(g, 0).wait()
        consume(g, bufs[0][0])

        @pl.when(g + 2 < total)
        def _():
            gather(g + 2, 0).start()

        gather(g + 1, 1).wait()
        consume(g + 1, bufs[1][0])


def _sc_call(body, out_type, stage_shape, stage_dtype, result_shape):
    mesh = plsc.VectorSubcoreMesh(core_axis_name="c", subcore_axis_name="s")
    return pl.kernel(
        body, out_type=out_type, mesh=mesh,
        scratch_types=[pltpu.VMEM((SC_BATCH, PEER_PICKS), jnp.int32), pltpu.VMEM(stage_shape, stage_dtype),
                       pltpu.VMEM((SC_ROWS, D_MODEL), F32), pltpu.VMEM((SC_ROWS, D_MODEL), F32),
                       pltpu.VMEM(result_shape, F32), pltpu.SemaphoreType.DMA, pltpu.SemaphoreType.DMA],
        compiler_params=pltpu.CompilerParams(needs_layout_passes=False))


def _sc_first_token(n_rows):
    return (lax.axis_index("c") * (SC_SUBCORES // 2) + lax.axis_index("s")) * (n_rows // SC_SUBCORES)


def sc_value_mix(experts, a, tabs):
    ns = experts.shape[0]
    n_chunks = PEER_PICKS // SC_ROWS
    half = D_MODEL // 2

    def body(e_hbm, a_hbm, tab_hbm, o_hbm, idx_v, a_v, rows0, rows1, out_v, sem0, sem1):
        t0 = _sc_first_token(ns)

        @pl.loop(0, ns // SC_SUBCORES // SC_BATCH)
        def _(bi):
            tb = t0 + bi * SC_BATCH
            pltpu.sync_copy(e_hbm.at[pl.ds(tb, SC_BATCH)], idx_v)
            pltpu.sync_copy(a_hbm.at[pl.ds(tb * PEER_PICKS, SC_BATCH * PEER_PICKS)], a_v)

            def consume(g, rows):
                k = g // n_chunks
                c = g % n_chunks
                for base in (0, half):
                    accs = tuple(jnp.where(c == 0, 0.0, out_v[k, pl.ds(base + d * SC_LANES, SC_LANES)])
                                 for d in range(half // SC_LANES))

                    def pick(p, accs, base=base):
                        lane_idx = jnp.zeros((SC_LANES,), jnp.int32) + (k * PEER_PICKS + c * SC_ROWS + p)
                        ap = plsc.load_gather(a_v, [lane_idx])
                        return tuple(acc + ap * rows[p, pl.ds(base + d * SC_LANES, SC_LANES)]
                                     for d, acc in enumerate(accs))

                    accs = lax.fori_loop(0, SC_ROWS, pick, accs)
                    for d, acc in enumerate(accs):
                        out_v[k, pl.ds(base + d * SC_LANES, SC_LANES)] = acc

            _sc_pipeline(idx_v, tab_hbm, ((rows0, sem0), (rows1, sem1)), consume)
            pltpu.sync_copy(out_v, o_hbm.at[pl.ds(tb, SC_BATCH)])

    call = _sc_call(body, jax.ShapeDtypeStruct((ns, D_MODEL), F32), (SC_BATCH * PEER_PICKS,), F32, (SC_BATCH, D_MODEL))
    return call(experts, a.reshape(-1), tabs)


def sc_expert_dots(experts, x, row0, tabs):
    ns = experts.shape[0]
    n_chunks = PEER_PICKS // SC_ROWS
    half = D_MODEL // 2
    n_partial = 4

    def body(e_hbm, x_hbm, tab_hbm, o_hbm, idx_v, x_v, rows0, rows1, out_v, sem0, sem1):
        t0 = _sc_first_token(ns)
        lanes = lax.iota(jnp.int32, SC_LANES)

        @pl.loop(0, ns // SC_SUBCORES // SC_BATCH)
        def _(bi):
            tb = t0 + bi * SC_BATCH
            pltpu.sync_copy(e_hbm.at[pl.ds(tb, SC_BATCH)], idx_v)
            pltpu.sync_copy(x_hbm.at[pl.ds(row0 + tb, SC_BATCH)], x_v)

            def consume(g, rows):
                k = g // n_chunks
                c = g % n_chunks
                dots = jnp.zeros((SC_LANES,), F32)
                for base in (0, half):
                    xs = tuple(x_v[k, pl.ds(base + d * SC_LANES, SC_LANES)] for d in range(half // SC_LANES))

                    def pick(p, dots, base=base, xs=xs):
                        parts = [jnp.zeros((SC_LANES,), F32)] * n_partial
                        for d, xd in enumerate(xs):
                            parts[d % n_partial] = parts[d % n_partial] + xd * rows[p, pl.ds(base + d * SC_LANES, SC_LANES)]
                        s = jnp.sum((parts[0] + parts[1]) + (parts[2] + parts[3]))
                        return jnp.where(lanes == p, dots + s, dots)

                    dots = lax.fori_loop(0, SC_ROWS, pick, dots)
                out_v[pl.ds(k * PEER_PICKS + c * SC_ROWS, SC_ROWS)] = dots

            _sc_pipeline(idx_v, tab_hbm, ((rows0, sem0), (rows1, sem1)), consume)
            pltpu.sync_copy(out_v, o_hbm.at[pl.ds(tb * PEER_PICKS, SC_BATCH * PEER_PICKS)])

    call = _sc_call(body, jax.ShapeDtypeStruct((ns * PEER_PICKS,), F32), (SC_BATCH, D_MODEL), F32, (SC_BATCH * PEER_PICKS,))
    return call(experts, x, tabs).reshape(ns, PEER_PICKS)


def _resid_ln_rows_kernel(x_ref, r_ref, g_ref, b_ref, y_in_ref, y_ref):
    del y_in_ref
    y_ref[...] = _ln(ALPHA * x_ref[...] + r_ref[...], g_ref[...], b_ref[...])


def resid_ln_rows(x, r, g, b, y, row0):
    n = r.shape[0] // ROW_TILE
    at = lambda i: (row0 // ROW_TILE + i, 0)
    return pl.pallas_call(
        _resid_ln_rows_kernel,
        grid=(n,),
        in_specs=[pl.BlockSpec((ROW_TILE, D_MODEL), at), _row_spec(ROW_TILE, D_MODEL), _full_spec(g.shape),
                  _full_spec(b.shape), pl.BlockSpec(memory_space=pl.ANY)],
        out_specs=pl.BlockSpec((ROW_TILE, D_MODEL), at),
        out_shape=jax.ShapeDtypeStruct(y.shape, y.dtype),
        input_output_aliases={4: 0},
        compiler_params=_cparams(("parallel",)),
        name="resid_ln_rows",
    )(x, r, g, b, y)


def _pack_kernel(t_ref, o_ref):
    n = t_ref.shape[1]
    for s in range(ROW_WORDS):
        lo = t_ref[0, :, (2 * s) * 128:(2 * s + 1) * 128].astype(BF16).astype(F32)
        hi = t_ref[0, :, (2 * s + 1) * 128:(2 * s + 2) * 128].astype(BF16).astype(F32)
        word = pltpu.bitcast(hi, jnp.uint32) | (pltpu.bitcast(lo, jnp.uint32) >> 16)
        o_ref[pl.ds(s, n, stride=ROW_WORDS), :] = pltpu.bitcast(word, jnp.int32)


def pack_table(tabs, layer):
    e = tabs.shape[1]
    return pl.pallas_call(
        _pack_kernel,
        grid=(e // PACK_TILE,),
        in_specs=[pl.BlockSpec((1, PACK_TILE, D_MODEL), lambda i: (layer, i, 0))],
        out_specs=_row_spec(PACK_TILE * ROW_WORDS, 128),
        out_shape=jax.ShapeDtypeStruct((e * ROW_WORDS, 128), jnp.int32),
        compiler_params=_cparams(("parallel",)),
        name="pack_table",
    )(tabs)


def peer_consts():
    col = jnp.arange(PEER_PICKS * D_CHUNKS) // D_CHUNKS
    spread = (col[None, :] == jnp.arange(PEER_PICKS)[:, None])
    return spread.astype(BF16), jnp.transpose(spread).astype(F32)


def peer_block(x, wq, keys, u_packed, v_packed, u_tabs, v_tabs, layer, spread, fold, g, b):
    T = x.shape[0]
    codes = pair_codes()
    expert0 = layer * PEER_EXPERTS
    eid_s, gate_s, expert_s = peer_route(x, wq, keys, codes, T - SC_U_TOKENS, expert0)
    hdn_s = sc_expert_dots(expert_s, x, T - SC_U_TOKENS, u_tabs.reshape(-1, D_MODEL))
    eid, gate, hdn = peer_route_u(x, wq, keys, codes, u_packed, fold, T - SC_U_TOKENS)
    eid, gate, hdn = (jnp.concatenate(p, axis=0) for p in ((eid, eid_s), (gate, gate_s), (hdn, hdn_s)))
    a_sc, e_sc = sc_gate(eid, hdn, gate, T - SC_V_TOKENS, expert0)
    mix_sc = sc_value_mix(e_sc, a_sc, v_tabs.reshape(-1, D_MODEL))
    y = peer_v(eid, hdn, gate, x, v_packed, spread, g, b, T - SC_V_TOKENS)
    return resid_ln_rows(x, mix_sc, g, b, y, T - SC_V_TOKENS)


def _mla_kv_kernel(x_ref, w_ref, g_ref, cos_ref, sin_ref, c_ref, kpe_ref):
    kv = jnp.dot(x_ref[...].astype(BF16), w_ref[...], preferred_element_type=F32)
    c_ref[...] = _rms(kv[:, :KV_LORA], g_ref[...])
    kp = kv[:, KV_LORA:KV_LORA + QK_ROPE]
    sw = jnp.concatenate([kp[:, QK_ROPE // 2:], kp[:, :QK_ROPE // 2]], axis=1)
    kpe_ref[...] = kp * cos_ref[...] + sw * sin_ref[...]


def mla_kv(x, w, g, cos2, sin2):
    T = x.shape[0]
    return pl.pallas_call(
        _mla_kv_kernel,
        grid=(T // ROW_TILE,),
        in_specs=[_row_spec(ROW_TILE, D_MODEL), _full_spec(w.shape), _full_spec(g.shape),
                  _row_spec(ROW_TILE, QK_ROPE), _row_spec(ROW_TILE, QK_ROPE)],
        out_specs=[_row_spec(ROW_TILE, KV_LORA), _row_spec(ROW_TILE, QK_ROPE)],
        out_shape=[jax.ShapeDtypeStruct((T, KV_LORA), F32), jax.ShapeDtypeStruct((T, QK_ROPE), F32)],
        compiler_params=_cparams(("parallel",)),
        name="mla_kv",
    )(x, w, g, cos2, sin2)


def _mla_expand_kernel(c_ref, kpe_ref, wk_ref, wv_ref, k_ref, v_ref):
    cb = c_ref[...].astype(BF16)
    kn = jnp.dot(cb, wk_ref[...], preferred_element_type=F32)
    v_ref[...] = jnp.dot(cb, wv_ref[...], preferred_element_type=F32).astype(BF16)
    kp = kpe_ref[...]
    kpad = jnp.concatenate([kp, jnp.zeros_like(kp)], axis=1).astype(BF16)
    for h in range(MLA_HEADS):
        k_ref[:, h * HEAD_SLOT:h * HEAD_SLOT + QK_NOPE] = kn[:, h * QK_NOPE:(h + 1) * QK_NOPE].astype(BF16)
        k_ref[:, h * HEAD_SLOT + QK_NOPE:(h + 1) * HEAD_SLOT] = kpad


def mla_expand(c, kpe, wk, wv):
    R = c.shape[0]
    return pl.pallas_call(
        _mla_expand_kernel,
        grid=(R // ROW_TILE,),
        in_specs=[_row_spec(ROW_TILE, KV_LORA), _row_spec(ROW_TILE, QK_ROPE), _full_spec(wk.shape), _full_spec(wv.shape)],
        out_specs=[_row_spec(ROW_TILE, MLA_HEADS * HEAD_SLOT), _row_spec(ROW_TILE, MLA_HEADS * V_HEAD)],
        out_shape=[jax.ShapeDtypeStruct((R, MLA_HEADS * HEAD_SLOT), BF16), jax.ShapeDtypeStruct((R, MLA_HEADS * V_HEAD), BF16)],
        compiler_params=_cparams(("parallel",)),
        name="mla_expand",
    )(c, kpe, wk, wv)


def _mla_q_kernel(x_ref, wdq_ref, g_ref, wuq_ref, cos_ref, sin_ref, q_ref):
    cq = _rms(jnp.dot(x_ref[...].astype(BF16), wdq_ref[...], preferred_element_type=F32), g_ref[...])
    q = jnp.dot(cq.astype(BF16), wuq_ref[...], preferred_element_type=F32)
    q = q * (ATTN_SCALE * math.log2(math.e))
    lane = lax.broadcasted_iota(jnp.int32, (ROW_TILE, 128), 1)
    cos = cos_ref[...]
    sin = sin_ref[...]
    for h in range(MLA_HEADS):
        q_ref[:, h * HEAD_SLOT:h * HEAD_SLOT + QK_NOPE] = q[:, h * HEAD_SLOT:h * HEAD_SLOT + QK_NOPE].astype(BF16)
        seg = q[:, h * HEAD_SLOT + QK_NOPE:(h + 1) * HEAD_SLOT]
        sw = jnp.where(lane < QK_ROPE // 2, pltpu.roll(seg, 128 - QK_ROPE // 2, 1),
                       jnp.where(lane < QK_ROPE, pltpu.roll(seg, QK_ROPE // 2, 1), 0.0))
        q_ref[:, h * HEAD_SLOT + QK_NOPE:(h + 1) * HEAD_SLOT] = (seg * cos + sw * sin).astype(BF16)


def mla_q(x, wdq, g, wuq, cosq, sinq):
    T = x.shape[0]
    return pl.pallas_call(
        _mla_q_kernel,
        grid=(T // ROW_TILE,),
        in_specs=[_row_spec(ROW_TILE, D_MODEL), _full_spec(wdq.shape), _full_spec(g.shape), _full_spec(wuq.shape),
                  _row_spec(ROW_TILE, 128), _row_spec(ROW_TILE, 128)],
        out_specs=_row_spec(ROW_TILE, MLA_HEADS * HEAD_SLOT),
        out_shape=jax.ShapeDtypeStruct((T, MLA_HEADS * HEAD_SLOT), BF16),
        compiler_params=_cparams(("parallel",)),
        name="mla_q",
    )(x, wdq, g, wuq, cosq, sinq)


def _attn_prompt_kernel(q_ref, k_ref, v_ref, o_ref, s_even, s_odd):
    qi = pl.program_id(2)
    q = q_ref[...]
    nt = (((1,), (1,)), ((), ()))

    def scores(ki, dst):
        off = pl.multiple_of(ki * ATTN_TILE, ATTN_TILE)
        dst[...] = lax.dot_general(q, k_ref[pl.ds(off, ATTN_TILE), :], nt, preferred_element_type=F32)

    def update(ki, src, carry, diagonal):
        m, l, acc = carry
        off = pl.multiple_of(ki * ATTN_TILE, ATTN_TILE)
        s = src[...]
        if diagonal:
            qc = lax.broadcasted_iota(jnp.int32, s.shape, 0) // CHUNK
            kc = lax.broadcasted_iota(jnp.int32, s.shape, 1) // CHUNK
            s = jnp.where(kc <= qc, s, -jnp.inf)
        m_new = jnp.maximum(m, jnp.max(s, axis=1, keepdims=True))
        alpha = jnp.exp2(m - m_new)
        p = jnp.exp2(s - m_new)
        l = alpha * l + jnp.sum(p, axis=1, keepdims=True)
        acc = alpha * acc + jnp.dot(p.astype(BF16), v_ref[pl.ds(off, ATTN_TILE), :], preferred_element_type=F32)
        return m_new, l, acc

    init = (jnp.full((ATTN_TILE, 1), -jnp.inf, F32), jnp.zeros((ATTN_TILE, 1), F32), jnp.zeros((ATTN_TILE, V_HEAD), F32))
    scores(0, s_even)

    def pair(j, carry):
        scores(2 * j + 1, s_odd)
        carry = update(2 * j, s_even, carry, False)
        scores(jnp.minimum(2 * j + 2, qi), s_even)
        return update(2 * j + 1, s_odd, carry, False)

    carry = lax.fori_loop(0, qi // 2, pair, init)

    def last_even(carry):
        return update(qi, s_even, carry, True)

    def last_odd(carry):
        scores(qi, s_odd)
        carry = update(qi - 1, s_even, carry, False)
        return update(qi, s_odd, carry, True)

    m, l, acc = lax.cond(qi % 2 == 0, last_even, last_odd, carry)
    o_ref[...] = (acc / l).astype(BF16)


def attn_prompt(q, k, v, batch, seq):
    nq = seq // ATTN_TILE
    return pl.pallas_call(
        _attn_prompt_kernel,
        grid=(batch, MLA_HEADS, nq),
        in_specs=[pl.BlockSpec((ATTN_TILE, HEAD_SLOT), lambda b, h, i: (b * nq + i, h)),
                  pl.BlockSpec((seq, HEAD_SLOT), lambda b, h, i: (b, h)),
                  pl.BlockSpec((seq, V_HEAD), lambda b, h, i: (b, h))],
        out_specs=pl.BlockSpec((ATTN_TILE, V_HEAD), lambda b, h, i: (b * nq + i, h)),
        out_shape=jax.ShapeDtypeStruct((batch * seq, MLA_HEADS * V_HEAD), BF16),
        scratch_shapes=[pltpu.VMEM((ATTN_TILE, ATTN_TILE), F32)] * 2,
        compiler_params=_cparams(("parallel", "parallel", "arbitrary")),
        name="attn_prompt",
    )(q, k, v)


def _attn_sample_kernel(q_ref, k_ref, v_ref, o_ref):
    nt = (((1,), (1,)), ((), ()))
    s = lax.dot_general(q_ref[...], k_ref[...], nt, preferred_element_type=F32)
    p = jnp.exp2(s - jnp.max(s, axis=1, keepdims=True))
    o = jnp.dot(p.astype(BF16), v_ref[...], preferred_element_type=F32)
    o_ref[...] = (o / jnp.sum(p, axis=1, keepdims=True)).astype(BF16)


def attn_sample(q, k, v, batch, q_len, kv_len, q_block0):
    return pl.pallas_call(
        _attn_sample_kernel,
        grid=(batch, MLA_HEADS),
        in_specs=[pl.BlockSpec((q_len, HEAD_SLOT), lambda b, h: (q_block0 + b, h)),
                  pl.BlockSpec((kv_len, HEAD_SLOT), lambda b, h: (b, h)),
                  pl.BlockSpec((kv_len, V_HEAD), lambda b, h: (b, h))],
        out_specs=pl.BlockSpec((q_len, V_HEAD), lambda b, h: (b, h)),
        out_shape=jax.ShapeDtypeStruct((batch * q_len, MLA_HEADS * V_HEAD), BF16),
        compiler_params=_cparams(("parallel", "parallel")),
        name="attn_sample",
    )(q, k, v)


def _proj_ln_kernel(x_ref, o_ref_in, w_ref, g_ref, b_ref, y_ref):
    mix = jnp.dot(o_ref_in[...], w_ref[...], preferred_element_type=F32)
    y_ref[...] = _ln(ALPHA * x_ref[...] + mix, g_ref[...], b_ref[...])


def proj_ln(x, o, w, g, b):
    T = x.shape[0]
    return pl.pallas_call(
        _proj_ln_kernel,
        grid=(T // ROW_TILE,),
        in_specs=[_row_spec(ROW_TILE, D_MODEL), _row_spec(ROW_TILE, o.shape[1]), _full_spec(w.shape),
                  _full_spec(g.shape), _full_spec(b.shape)],
        out_specs=_row_spec(ROW_TILE, D_MODEL),
        out_shape=jax.ShapeDtypeStruct((T, D_MODEL), F32),
        compiler_params=_cparams(("parallel",)),
        name="proj_ln",
    )(x, o, w, g, b)


def _rope_tables(pos):
    inv = 1.0 / (ROPE_BASE ** (jnp.arange(0, QK_ROPE, 2, dtype=F32) / QK_ROPE))
    ang = pos.astype(F32)[:, None] * inv[None, :]
    cos, sin = jnp.cos(ang), jnp.sin(ang)
    cos2 = jnp.concatenate([cos, cos], axis=1)
    sin2 = jnp.concatenate([-sin, sin], axis=1)
    pad = jnp.zeros_like(cos2)
    return cos2, sin2, jnp.concatenate([cos2, pad], axis=1), jnp.concatenate([sin2, pad], axis=1)


def _gm_chunk_mats(w_s, b_s, dec_seq):
    i = jnp.arange(GM_CHUNK)
    mask = (i[None, :] // CHUNK) <= (i[:, None] // CHUNK)
    full = jnp.where(mask[None], w_s, 0.0)
    rep = GM_CHUNK // dec_seq
    blk = w_s[:, :dec_seq, :dec_seq]
    same = (i[:, None] // dec_seq) == (i[None, :] // dec_seq)
    diag = jnp.where(same[None], jnp.tile(blk, (1, rep, rep)), 0.0)
    ws = jnp.stack([full, diag], axis=0).astype(BF16)
    bias = jnp.stack([b_s, jnp.tile(b_s[:, :dec_seq], (1, rep))], axis=0)
    bs = jnp.broadcast_to(bias[..., None], bias.shape + (GM_GROUP_DIM,)).astype(F32)
    return ws, bs


def _pad_q_weight(w_uq):
    w = w_uq.reshape(Q_LORA, MLA_HEADS, QK_NOPE + QK_ROPE)
    w = jnp.pad(w, ((0, 0), (0, 0), (0, HEAD_SLOT - QK_NOPE - QK_ROPE)))
    return w.reshape(Q_LORA, MLA_HEADS * HEAD_SLOT).astype(BF16)


def kernel(x_prompt, x_sample, cache_ckv, cache_kpe, ln1_g, ln1_b, ln2_g, ln2_b, gm_w_in, gm_b_in, gm_ln_g, gm_ln_b,
           gm_w_s, gm_b_s, gm_w_out, mla_w_dkv, mla_kv_norm_g, mla_w_ukv, mla_w_dq, mla_q_norm_g, mla_w_uq, mla_w_o,
           peer_w_q, peer_subkeys, peer_u, peer_v):
    B, S, _ = x_prompt.shape
    DB, DS, _ = x_sample.shape
    PAST = cache_ckv.shape[1]
    TP, TS = B * S, DB * DS
    T = TP + TS
    assert TP % ROW_TILE == 0 and TS % ROW_TILE == 0 and T % PEER_TILE == 0 and S % ATTN_TILE == 0
    for n_sc in (SC_U_TOKENS, SC_V_TOKENS):
        assert n_sc % (SC_SUBCORES * SC_BATCH) == 0 and n_sc % ROW_TILE == 0 and (T - n_sc) % ROW_TILE == 0
    assert GM_CHUNK % DS == 0 and DS <= CHUNK and PAST % CHUNK == 0 and (PAST + DS) % 16 == 0

    x = jnp.concatenate([x_prompt.reshape(TP, D_MODEL), x_sample.reshape(TS, D_MODEL)], axis=0)
    pos = jnp.concatenate([jnp.tile(jnp.arange(S, dtype=jnp.int32), B),
                           jnp.tile(PAST + jnp.arange(DS, dtype=jnp.int32), DB)])
    cos2, sin2, cosq, sinq = _rope_tables(pos)
    spread, fold = peer_consts()
    row = lambda a: a.reshape(1, -1)

    def peer_layer(x, l):
        return peer_block(x, peer_w_q[l].astype(BF16), peer_subkeys[l].astype(BF16), pack_table(peer_u, l),
                          pack_table(peer_v, l), peer_u, peer_v, l, spread, fold, row(ln2_g[l]), row(ln2_b[l]))

    gm_rows = []
    for l in range(N_A):
        u, v = gm_in(x, gm_w_in[l].astype(BF16), row(gm_b_in[l]), row(gm_ln_g[l]), row(gm_ln_b[l]))
        gm_rows.append(v[TP:].reshape(DB, DS, GM_HALF))
        ws, bs = _gm_chunk_mats(gm_w_s[l], gm_b_s[l], DS)
        x = gm_mix(x, u, v, ws, bs, gm_w_out[l].astype(BF16), row(ln1_g[l]), row(ln1_b[l]), TP // ROW_TILE)
        x = peer_layer(x, l)

    w_dkv = jnp.pad(mla_w_dkv, ((0, 0), (0, 128 - QK_ROPE))).astype(BF16)
    c_new, kpe_new = mla_kv(x, w_dkv, row(mla_kv_norm_g), cos2, sin2)
    w_ukv = mla_w_ukv.reshape(KV_LORA, MLA_HEADS, QK_NOPE + V_HEAD)
    w_uk = w_ukv[:, :, :QK_NOPE].reshape(KV_LORA, MLA_HEADS * QK_NOPE).astype(BF16)
    w_uv = w_ukv[:, :, QK_NOPE:].reshape(KV_LORA, MLA_HEADS * V_HEAD).astype(BF16)
    kp_cat, vp = mla_expand(c_new[:TP], kpe_new[:TP], w_uk, w_uv)
    KV = PAST + DS
    c_all = jnp.concatenate([cache_ckv, c_new[TP:].reshape(DB, DS, KV_LORA)], axis=1).reshape(DB * KV, KV_LORA)
    kpe_all = jnp.concatenate([cache_kpe, kpe_new[TP:].reshape(DB, DS, QK_ROPE)], axis=1).reshape(DB * KV, QK_ROPE)
    ks_cat, vs = mla_expand(c_all, kpe_all, w_uk, w_uv)

    for j in range(DEPTH - N_A):
        l = N_A + j
        q = mla_q(x, mla_w_dq[j].astype(BF16), row(mla_q_norm_g[j]), _pad_q_weight(mla_w_uq[j]), cosq, sinq)
        o = jnp.concatenate([attn_prompt(q, kp_cat, vp, B, S),
                             attn_sample(q, ks_cat, vs, DB, DS, KV, TP // DS)], axis=0)
        x = proj_ln(x, o, mla_w_o[j].astype(BF16), row(ln1_g[l]), row(ln1_b[l]))
        x = peer_layer(x, l)

    return (x[:TP].reshape(B, S, D_MODEL), x[TP:].reshape(DB, DS, D_MODEL), jnp.stack(gm_rows, axis=0),
            c_new[:TP].reshape(B, S, KV_LORA), kpe_new[:TP].reshape(B, S, QK_ROPE),
            c_new[TP:].reshape(DB, DS, KV_LORA), kpe_new[TP:].reshape(DB, DS, QK_ROPE))
```

```python
import functools
import math

import jax
import jax.numpy as jnp
from jax import lax
from jax.experimental import pallas as pl
from jax.experimental.pallas import tpu as pltpu
from jax.experimental.pallas import tpu_sc as plsc

F32 = jnp.float32
BF16 = jnp.bfloat16

D_MODEL = 1024
DEPTH = 4
N_A = DEPTH // 2
CHUNK = 64
ALPHA = (2.0 * DEPTH) ** 0.25
LN_EPS = 1e-5
RMS_EPS = 1e-6

GM_CHUNK = 128
GM_HALF = 2 * D_MODEL
GM_GROUPS = 8
GM_GROUP_DIM = GM_HALF // GM_GROUPS

MLA_HEADS = 8
QK_NOPE = 128
QK_ROPE = 64
V_HEAD = 128
KV_LORA = D_MODEL // 4
Q_LORA = 3 * D_MODEL // 8
ROPE_BASE = 10000.0
ATTN_SCALE = (QK_NOPE + QK_ROPE) ** -0.5
HEAD_SLOT = 256

PEER_HEADS = 8
PEER_NKEYS = 128
PEER_EXPERTS = PEER_NKEYS * PEER_NKEYS
PEER_DK = 256
PEER_TOPK = 16
PEER_PICKS = PEER_HEADS * PEER_TOPK
TOPK_SHIFT = PEER_TOPK.bit_length() - 1
PAIR_SENTINEL = 1 << 20
ROW_WORDS = 4
D_CHUNKS = D_MODEL // 128

ROW_TILE = 256
PEER_TILE = 128
PACK_TILE = 512
SC_U_TOKENS = 12288
SC_V_TOKENS = 12288
SC_BATCH = 32
PACKED_WORDS = D_MODEL // 2
SC_SUBCORES = 32
SC_LANES = 16
SC_ROWS = 16
V_TOKENS_PER_TRIP = 8
ATTN_TILE = 512
VMEM_LIMIT = 56 * 1024 * 1024


def _cparams(sem):
    return pltpu.CompilerParams(dimension_semantics=sem, vmem_limit_bytes=VMEM_LIMIT)


def _gelu(x):
    return 0.5 * x * (1.0 + lax.erf(x * (1.0 / math.sqrt(2.0))))


def _ln(x, g, b):
    mu = jnp.mean(x, axis=-1, keepdims=True)
    xc = x - mu
    var = jnp.mean(xc * xc, axis=-1, keepdims=True)
    return xc * lax.rsqrt(var + LN_EPS) * g + b


def _rms(x, g):
    return x * lax.rsqrt(jnp.mean(x * x, axis=-1, keepdims=True) + RMS_EPS) * g


def _row_spec(tile, width):
    return pl.BlockSpec((tile, width), lambda i: (i, 0))


def _full_spec(shape):
    nd = len(shape)
    return pl.BlockSpec(shape, lambda i: (0,) * nd)


def _gm_in_kernel(x_ref, w_ref, b_ref, lg_ref, lb_ref, u_ref, v_ref):
    z = jnp.dot(x_ref[...].astype(BF16), w_ref[...], preferred_element_type=F32) + b_ref[...]
    z = _gelu(z)
    u_ref[...] = z[:, :GM_HALF]
    v_ref[...] = _ln(z[:, GM_HALF:], lg_ref[...], lb_ref[...])


def gm_in(x, w, b, lg, lb):
    T = x.shape[0]
    return pl.pallas_call(
        _gm_in_kernel,
        grid=(T // ROW_TILE,),
        in_specs=[_row_spec(ROW_TILE, D_MODEL), _full_spec(w.shape), _full_spec(b.shape),
                  _full_spec(lg.shape), _full_spec(lb.shape)],
        out_specs=[_row_spec(ROW_TILE, GM_HALF), _row_spec(ROW_TILE, GM_HALF)],
        out_shape=[jax.ShapeDtypeStruct((T, GM_HALF), F32)] * 2,
        compiler_params=_cparams(("parallel",)),
        name="gm_in",
    )(x, w, b, lg, lb)


def _gm_mix_kernel(x_ref, u_ref, v_ref, ws_ref, bs_ref, wo_ref, g_ref, b_ref, o_ref, s_scr):
    for c in range(ROW_TILE // GM_CHUNK):
        r = slice(c * GM_CHUNK, (c + 1) * GM_CHUNK)
        for g in range(GM_GROUPS):
            cs = slice(g * GM_GROUP_DIM, (g + 1) * GM_GROUP_DIM)
            sv = jnp.dot(ws_ref[0, g], v_ref[r, cs].astype(BF16), preferred_element_type=F32) + bs_ref[0, g]
            s_scr[r, cs] = (u_ref[r, cs] * sv).astype(BF16)
    mix = jnp.dot(s_scr[...], wo_ref[...], preferred_element_type=F32)
    o_ref[...] = _ln(ALPHA * x_ref[...] + mix, g_ref[...], b_ref[...])


def gm_mix(x, u, v, ws, bs, wo, g, b, n_prompt_tiles):
    T = x.shape[0]
    sel = lambda i: (jnp.where(i >= n_prompt_tiles, 1, 0), 0, 0, 0)
    return pl.pallas_call(
        _gm_mix_kernel,
        grid=(T // ROW_TILE,),
        in_specs=[_row_spec(ROW_TILE, D_MODEL), _row_spec(ROW_TILE, GM_HALF), _row_spec(ROW_TILE, GM_HALF),
                  pl.BlockSpec((1,) + ws.shape[1:], sel), pl.BlockSpec((1,) + bs.shape[1:], sel),
                  _full_spec(wo.shape), _full_spec(g.shape), _full_spec(b.shape)],
        out_specs=_row_spec(ROW_TILE, D_MODEL),
        out_shape=jax.ShapeDtypeStruct((T, D_MODEL), F32),
        scratch_shapes=[pltpu.VMEM((ROW_TILE, GM_HALF), BF16)],
        compiler_params=_cparams(("parallel",)),
        name="gm_mix",
    )(x, u, v, ws, bs, wo, g, b)


def _extract_max(s, codes, sentinel):
    m = jnp.max(s, axis=0, keepdims=True)
    c = jnp.min(jnp.where(s == m, codes, sentinel), axis=0, keepdims=True)
    return m, c, jnp.where(codes == c, -jnp.inf, s)


def _topk_rows(s):
    rows = lax.broadcasted_iota(jnp.int32, s.shape, 0).astype(F32)
    vals, idxs = [], []
    for _ in range(PEER_TOPK):
        m, idx, s = _extract_max(s, rows, float(s.shape[0]))
        vals.append(m)
        idxs.append(idx)
    return jnp.concatenate(vals, axis=0), jnp.concatenate(idxs, axis=0).astype(jnp.int32)


_PAIR_GROUPS = ((None, 0, 16), (None, 1, 8), (0, None, 16), (1, None, 8), (2, None, 8), (3, None, 8), (4, None, 8))


def pair_codes():
    codes = []
    for b_fix, a_fix, n in _PAIR_GROUPS:
        for r in range(n):
            a, b = (a_fix, r) if b_fix is None else (r, b_fix)
            first = a_fix is not None or a >= 2
            ok = first and (a + 1) * (b + 1) <= PEER_TOPK
            codes.append(a * PEER_TOPK + b if ok else PAIR_SENTINEL)
    assert sorted(c for c in codes if c != PAIR_SENTINEL) == sorted(
        a * PEER_TOPK + b for a in range(PEER_TOPK) for b in range(PEER_TOPK) if (a + 1) * (b + 1) <= PEER_TOPK)
    return jnp.broadcast_to(jnp.asarray(codes, F32)[:, None], (len(codes), PEER_TILE))


def _topk_pairs(v1, i1, v2, i2, codes):
    parts = []
    for b_fix, a_fix, n in _PAIR_GROUPS:
        if b_fix is None:
            parts.append(v1[a_fix:a_fix + 1] + v2[0:n])
        else:
            parts.append(v1[0:n] + v2[b_fix:b_fix + 1])
    cand = jnp.where(codes < PAIR_SENTINEL, jnp.concatenate(parts, axis=0), -jnp.inf)
    krow = lax.broadcasted_iota(jnp.int32, v1.shape, 0)
    vals, k1, k2 = [], [], []
    for _ in range(PEER_TOPK):
        m, c, cand = _extract_max(cand, codes, float(PAIR_SENTINEL))
        vals.append(m)
        c = c.astype(jnp.int32)
        k1.append(jnp.sum(jnp.where(krow == (c >> TOPK_SHIFT), i1, 0), axis=0, keepdims=True))
        k2.append(jnp.sum(jnp.where(krow == (c & (PEER_TOPK - 1)), i2, 0), axis=0, keepdims=True))
    return jnp.concatenate(vals, axis=0), jnp.concatenate(k1, axis=0), jnp.concatenate(k2, axis=0)


def _peer_route_kernel(x_ref, wq_ref, k_ref, codes_ref, eid_ref, gate_ref, expert_ref, *, expert0):
    q = jnp.dot(x_ref[...].astype(BF16), wq_ref[...], preferred_element_type=F32)
    half = PEER_DK // 2
    nt = (((1,), (1,)), ((), ()))
    codes = codes_ref[...]
    experts, gates = [], []
    for h in range(PEER_HEADS):
        q1 = q[:, h * PEER_DK:h * PEER_DK + half].astype(BF16)
        q2 = q[:, h * PEER_DK + half:(h + 1) * PEER_DK].astype(BF16)
        s1 = lax.dot_general(k_ref[0], q1, nt, preferred_element_type=F32)
        s2 = lax.dot_general(k_ref[1], q2, nt, preferred_element_type=F32)
        v1, i1 = _topk_rows(s1)
        v2, i2 = _topk_rows(s2)
        sc, e1, e2 = _topk_pairs(v1, i1, v2, i2, codes)
        e = jnp.exp(sc - sc[0:1])
        gates.append(e / jnp.sum(e, axis=0, keepdims=True))
        experts.append(e1 * PEER_NKEYS + e2)
    expert = jnp.transpose(jnp.concatenate(experts, axis=0))
    eid_ref[...] = expert * ROW_WORDS
    expert_ref[...] = expert + expert0
    gate_ref[...] = jnp.transpose(jnp.concatenate(gates, axis=0))


def peer_route(x, wq, keys, codes, row0, expert0):
    n = (x.shape[0] - row0) // PEER_TILE
    tile = (PEER_TILE, PEER_PICKS)
    return pl.pallas_call(
        functools.partial(_peer_route_kernel, expert0=expert0),
        grid=(n,),
        in_specs=[pl.BlockSpec((PEER_TILE, D_MODEL), lambda i: (row0 // PEER_TILE + i, 0)), _full_spec(wq.shape),
                  _full_spec(keys.shape), _full_spec(codes.shape)],
        out_specs=[_row_spec(*tile), _row_spec(*tile), _row_spec(*tile)],
        out_shape=[jax.ShapeDtypeStruct((n * PEER_TILE, PEER_PICKS), jnp.int32),
                   jax.ShapeDtypeStruct((n * PEER_TILE, PEER_PICKS), F32),
                   jax.ShapeDtypeStruct((n * PEER_TILE, PEER_PICKS), jnp.int32)],
        compiler_params=_cparams(("parallel",)),
        name="peer_route",
    )(x, wq, keys, codes)


def _gather_rows(eid_ref, t, tab_ref, dst_ref):
    picks = eid_ref.at[t]
    for p in range(PEER_PICKS):
        row = pl.multiple_of(picks[p], ROW_WORDS)
        dst_ref[pl.ds(ROW_WORDS * p, ROW_WORDS), :] = tab_ref[pl.ds(row, ROW_WORDS), :]


def _chunk_mask():
    sub = lax.broadcasted_iota(jnp.int32, (D_CHUNKS, PEER_PICKS * D_CHUNKS), 0)
    col = lax.broadcasted_iota(jnp.int32, (D_CHUNKS, PEER_PICKS * D_CHUNKS), 1)
    return ((col & (D_CHUNKS - 1)) == sub).astype(F32)


def _for_tokens_pipelined(eid_ref, tab_ref, bufs, compute, tokens_per_trip):
    _gather_rows(eid_ref, 0, tab_ref, bufs[0])

    def body(i, carry):
        t0 = tokens_per_trip * i
        for k in range(tokens_per_trip):
            _gather_rows(eid_ref, jnp.minimum(t0 + k + 1, PEER_TILE - 1), tab_ref, bufs[(k + 1) % 2])
            compute(t0 + k, bufs[k % 2])
        return carry

    lax.fori_loop(0, PEER_TILE // tokens_per_trip, body, 0)


def _peer_route_u_kernel(xr_ref, xc_ref, wq_ref, k_ref, codes_ref, tab_ref, fold_ref, zeros_ref,
                         eid_ref, gate_ref, h_ref, q_scr, eid_t, gate_t, eid_vmem, eid_smem, hrep, sem, *bufs):
    i = pl.program_id(0)
    half = PEER_DK // 2
    nt = (((1,), (1,)), ((), ()))
    eid_to_smem = pltpu.make_async_copy(eid_vmem, eid_smem, sem)

    @pl.when(i == 0)
    def _():
        pltpu.sync_copy(zeros_ref, eid_smem)

    @pl.when(i > 0)
    def _():
        eid_to_smem.wait()

    q = jnp.dot(xr_ref[...].astype(BF16), wq_ref[...], preferred_element_type=F32)
    for h in range(PEER_HEADS):
        q_scr[h] = q[:, h * PEER_DK:(h + 1) * PEER_DK]
    codes = codes_ref[...]
    mask = _chunk_mask()

    def compute(t, rows_ref):
        u = pltpu.bitcast(rows_ref[...], BF16)
        xr = xc_ref[pl.ds(t, 1), :]
        x8 = jnp.concatenate([xr[:, j * 128:(j + 1) * 128] for j in range(D_CHUNKS)], axis=0)
        hi = x8.astype(BF16)
        lo = (x8 - hi.astype(F32)).astype(BF16)
        o = lax.dot_general(jnp.concatenate([hi, lo], axis=0), u, nt, preferred_element_type=F32)
        o8 = (o[0:D_CHUNKS] + o[D_CHUNKS:2 * D_CHUNKS]) * mask
        hrep[pl.ds(t, 1), :] = jnp.sum(o8, axis=0, keepdims=True)

    per_trip = PEER_TILE // PEER_HEADS
    _gather_rows(eid_smem, 0, tab_ref, bufs[0])

    def body(h, carry):
        qh = q_scr[h]
        s1 = lax.dot_general(k_ref[0], qh[:, :half].astype(BF16), nt, preferred_element_type=F32)
        s2 = lax.dot_general(k_ref[1], qh[:, half:].astype(BF16), nt, preferred_element_type=F32)
        v1, i1 = _topk_rows(s1)
        v2, i2 = _topk_rows(s2)
        sc, e1, e2 = _topk_pairs(v1, i1, v2, i2, codes)
        e = jnp.exp(sc - sc[0:1])
        gate_t[h] = e / jnp.sum(e, axis=0, keepdims=True)
        eid_t[h] = (e1 * PEER_NKEYS + e2) * ROW_WORDS
        t0 = per_trip * h
        for k in range(per_trip):
            _gather_rows(eid_smem, jnp.minimum(t0 + k + 1, PEER_TILE - 1), tab_ref, bufs[(k + 1) % 2])
            compute(t0 + k, bufs[k % 2])
        return carry

    lax.fori_loop(0, PEER_HEADS, body, 0)
    h_ref[...] = jnp.dot(hrep[...], fold_ref[...], precision=lax.Precision.HIGHEST, preferred_element_type=F32)
    eid = jnp.transpose(eid_t[...].reshape(PEER_PICKS, PEER_TILE))
    eid_ref[...] = eid
    eid_vmem[...] = eid
    gate_ref[...] = jnp.transpose(gate_t[...].reshape(PEER_PICKS, PEER_TILE))
    eid_to_smem.start()

    @pl.when(i == pl.num_programs(0) - 1)
    def _():
        eid_to_smem.wait()


def _peer_v_kernel(eid_ref, h_ref, gate_ref, x_ref, tab_ref, spread_ref, g_ref, b_ref, y_ref, a_hi, a_lo, mix, *bufs):
    a = gate_ref[...] * _gelu(h_ref[...])
    hi = a.astype(BF16)
    lo = (a - hi.astype(F32)).astype(BF16)
    a_hi[...] = jnp.dot(hi, spread_ref[...], preferred_element_type=F32)
    a_lo[...] = jnp.dot(lo, spread_ref[...], preferred_element_type=F32)
    mask = _chunk_mask()

    def compute(t, rows_ref):
        v = pltpu.bitcast(rows_ref[...], BF16)
        lhs = jnp.concatenate([(a_hi[pl.ds(t, 1), :] * mask).astype(BF16),
                               (a_lo[pl.ds(t, 1), :] * mask).astype(BF16)], axis=0)
        o = jnp.dot(lhs, v, preferred_element_type=F32)
        o8 = o[0:D_CHUNKS] + o[D_CHUNKS:2 * D_CHUNKS]
        mix[pl.ds(t, 1), :] = jnp.concatenate([o8[j:j + 1] for j in range(D_CHUNKS)], axis=1)

    _for_tokens_pipelined(eid_ref, tab_ref, bufs, compute, V_TOKENS_PER_TRIP)
    y_ref[...] = _ln(ALPHA * x_ref[...] + mix[...], g_ref[...], b_ref[...])


def _peer_specs():
    eid_spec = pl.BlockSpec((PEER_TILE, PEER_PICKS), lambda i: (i, 0), memory_space=pltpu.SMEM)
    tab_spec = pl.BlockSpec((PEER_EXPERTS * ROW_WORDS, 128), lambda i: (0, 0), pipeline_mode=pl.Buffered(1))
    bufs = [pltpu.VMEM((PEER_PICKS * ROW_WORDS, 128), jnp.int32)] * 2
    return eid_spec, tab_spec, bufs


def peer_route_u(x, wq, keys, codes, tab, fold, n_tokens):
    T = n_tokens
    n = T // PEER_TILE
    _, tab_spec, bufs = _peer_specs()
    cur = lambda i: (jnp.minimum(i, n - 1), 0)
    prev = lambda i: (jnp.maximum(i - 1, 0), 0)
    zeros = jnp.zeros((PEER_TILE, PEER_PICKS), jnp.int32)
    tile = (PEER_TILE, PEER_PICKS)
    return pl.pallas_call(
        _peer_route_u_kernel,
        grid=(n + 1,),
        in_specs=[pl.BlockSpec((PEER_TILE, D_MODEL), cur), pl.BlockSpec((PEER_TILE, D_MODEL), prev),
                  _full_spec(wq.shape), _full_spec(keys.shape), _full_spec(codes.shape), tab_spec,
                  _full_spec(fold.shape), _full_spec(zeros.shape)],
        out_specs=[pl.BlockSpec(tile, cur), pl.BlockSpec(tile, cur), pl.BlockSpec(tile, prev)],
        out_shape=[jax.ShapeDtypeStruct((T, PEER_PICKS), jnp.int32), jax.ShapeDtypeStruct((T, PEER_PICKS), F32),
                   jax.ShapeDtypeStruct((T, PEER_PICKS), F32)],
        scratch_shapes=[pltpu.VMEM((PEER_HEADS, PEER_TILE, PEER_DK), F32),
                        pltpu.VMEM((PEER_HEADS, PEER_TOPK, PEER_TILE), jnp.int32),
                        pltpu.VMEM((PEER_HEADS, PEER_TOPK, PEER_TILE), F32),
                        pltpu.VMEM(tile, jnp.int32), pltpu.SMEM(tile, jnp.int32),
                        pltpu.VMEM((PEER_TILE, PEER_PICKS * D_CHUNKS), F32),
                        pltpu.SemaphoreType.DMA(())] + bufs,
        compiler_params=_cparams(("arbitrary",)),
        name="peer_route_u",
    )(x, x, wq, keys, codes, tab, fold, zeros)


def peer_v(eid, hdn, gate, x, tab, spread, g, b, n_tokens):
    T = eid.shape[0]
    eid_spec, tab_spec, bufs = _peer_specs()
    return pl.pallas_call(
        _peer_v_kernel,
        grid=(n_tokens // PEER_TILE,),
        in_specs=[eid_spec, _row_spec(PEER_TILE, PEER_PICKS), _row_spec(PEER_TILE, PEER_PICKS),
                  _row_spec(PEER_TILE, D_MODEL), tab_spec, _full_spec(spread.shape), _full_spec(g.shape), _full_spec(b.shape)],
        out_specs=_row_spec(PEER_TILE, D_MODEL),
        out_shape=jax.ShapeDtypeStruct((T, D_MODEL), F32),
        scratch_shapes=[pltpu.VMEM((PEER_TILE, PEER_PICKS * D_CHUNKS), F32)] * 2 + [pltpu.VMEM((PEER_TILE, D_MODEL), F32)] + bufs,
        compiler_params=_cparams(("arbitrary",)),
        name="peer_v",
    )(eid, hdn, gate, x, tab, spread, g, b)


def _sc_gate_kernel(eid_ref, h_ref, gate_ref, a_ref, e_ref, *, expert0):
    a_ref[...] = gate_ref[...] * _gelu(h_ref[...])
    e_ref[...] = eid_ref[...] // ROW_WORDS + expert0


def sc_gate(eid, hdn, gate, row0, expert0):
    n = (eid.shape[0] - row0) // PEER_TILE
    tile = (PEER_TILE, PEER_PICKS)
    spec = pl.BlockSpec(tile, lambda i: (row0 // PEER_TILE + i, 0))
    return pl.pallas_call(
        functools.partial(_sc_gate_kernel, expert0=expert0),
        grid=(n,),
        in_specs=[spec, spec, spec],
        out_specs=[_row_spec(*tile), _row_spec(*tile)],
        out_shape=[jax.ShapeDtypeStruct((n * PEER_TILE, PEER_PICKS), F32),
                   jax.ShapeDtypeStruct((n * PEER_TILE, PEER_PICKS), jnp.int32)],
        compiler_params=_cparams(("parallel",)),
        name="sc_gate",
    )(eid, hdn, gate)


def _sc_pipeline(idx_v, tab_hbm, bufs, consume):
    n_chunks = PEER_PICKS // SC_ROWS

    def gather(g, slot):
        rows, sem = bufs[slot]
        idx = idx_v.at[g // n_chunks, pl.ds((g % n_chunks) * SC_ROWS, SC_ROWS)]
        return pltpu.make_async_copy(tab_hbm.at[idx], rows, sem)

    total = SC_BATCH * n_chunks
    gather(0, 0).start()

    @pl.loop(0, total // 2)
    def _(j):
        g = 2 * j
        gather(g + 1, 1).start()
        gather(g, 0).wait()
        consume(g, bufs[0][0])

        @pl.when(g + 2 < total)
        def _():
            gather(g + 2, 0).start()

        gather(g + 1, 1).wait()
        consume(g + 1, bufs[1][0])


def _sc_call(body, out_type, stage_shape, stage_dtype, result_shape, row_width, row_dtype):
    mesh = plsc.VectorSubcoreMesh(core_axis_name="c", subcore_axis_name="s")
    return pl.kernel(
        body, out_type=out_type, mesh=mesh,
        scratch_types=[pltpu.VMEM((SC_BATCH, PEER_PICKS), jnp.int32), pltpu.VMEM(stage_shape, stage_dtype),
                       pltpu.VMEM((SC_ROWS, row_width), row_dtype), pltpu.VMEM((SC_ROWS, row_width), row_dtype),
                       pltpu.VMEM(result_shape, F32), pltpu.SemaphoreType.DMA, pltpu.SemaphoreType.DMA],
        compiler_params=pltpu.CompilerParams(needs_layout_passes=False))


def _sc_first_token(n_rows):
    return (lax.axis_index("c") * (SC_SUBCORES // 2) + lax.axis_index("s")) * (n_rows // SC_SUBCORES)


def _packed_word_groups(half_idx):
    return [(s * 128 + l0, 256 * s + l0, 256 * s + 128 + l0)
            for s in (2 * half_idx, 2 * half_idx + 1) for l0 in range(0, 128, SC_LANES)]


def sc_value_mix(experts, a, tab):
    ns = experts.shape[0]
    n_chunks = PEER_PICKS // SC_ROWS

    def body(e_hbm, a_hbm, tab_hbm, o_hbm, idx_v, a_v, rows0, rows1, out_v, sem0, sem1):
        t0 = _sc_first_token(ns)

        @pl.loop(0, ns // SC_SUBCORES // SC_BATCH)
        def _(bi):
            tb = t0 + bi * SC_BATCH
            pltpu.sync_copy(e_hbm.at[pl.ds(tb, SC_BATCH)], idx_v)
            pltpu.sync_copy(a_hbm.at[pl.ds(tb * PEER_PICKS, SC_BATCH * PEER_PICKS)], a_v)

            def consume(g, rows):
                k = g // n_chunks
                c = g % n_chunks
                for hf in range(2):
                    groups = _packed_word_groups(hf)
                    accs = tuple(jnp.where(c == 0, 0.0, out_v[k, pl.ds(d, SC_LANES)])
                                 for _, d_lo, d_hi in groups for d in (d_lo, d_hi))

                    def pick(p, accs, groups=groups):
                        lane_idx = jnp.zeros((SC_LANES,), jnp.int32) + (k * PEER_PICKS + c * SC_ROWS + p)
                        ap = plsc.load_gather(a_v, [lane_idx])
                        new = []
                        for i, (w, _, _) in enumerate(groups):
                            words = rows[p, pl.ds(w, SC_LANES)]
                            new.append(accs[2 * i] + ap * lax.bitcast_convert_type(words << 16, F32))
                            new.append(accs[2 * i + 1] + ap * lax.bitcast_convert_type(words & -65536, F32))
                        return tuple(new)

                    accs = lax.fori_loop(0, SC_ROWS, pick, accs)
                    for i, (_, d_lo, d_hi) in enumerate(groups):
                        out_v[k, pl.ds(d_lo, SC_LANES)] = accs[2 * i]
                        out_v[k, pl.ds(d_hi, SC_LANES)] = accs[2 * i + 1]

            _sc_pipeline(idx_v, tab_hbm, ((rows0, sem0), (rows1, sem1)), consume)
            pltpu.sync_copy(out_v, o_hbm.at[pl.ds(tb, SC_BATCH)])

    call = _sc_call(body, jax.ShapeDtypeStruct((ns, D_MODEL), F32), (SC_BATCH * PEER_PICKS,), F32,
                    (SC_BATCH, D_MODEL), PACKED_WORDS, jnp.int32)
    return call(experts, a.reshape(-1), tab)


def sc_expert_dots(experts, x, row0, tabs):
    ns = experts.shape[0]
    n_chunks = PEER_PICKS // SC_ROWS
    half = D_MODEL // 2
    n_partial = 4

    def body(e_hbm, x_hbm, tab_hbm, o_hbm, idx_v, x_v, rows0, rows1, out_v, sem0, sem1):
        t0 = _sc_first_token(ns)
        lanes = lax.iota(jnp.int32, SC_LANES)

        @pl.loop(0, ns // SC_SUBCORES // SC_BATCH)
        def _(bi):
            tb = t0 + bi * SC_BATCH
            pltpu.sync_copy(e_hbm.at[pl.ds(tb, SC_BATCH)], idx_v)
            pltpu.sync_copy(x_hbm.at[pl.ds(row0 + tb, SC_BATCH)], x_v)

            def consume(g, rows):
                k = g // n_chunks
                c = g % n_chunks
                dots = jnp.zeros((SC_LANES,), F32)
                for base in (0, half):
                    xs = tuple(x_v[k, pl.ds(base + d * SC_LANES, SC_LANES)] for d in range(half // SC_LANES))

                    def pick(p, dots, base=base, xs=xs):
                        parts = [jnp.zeros((SC_LANES,), F32)] * n_partial
                        for d, xd in enumerate(xs):
                            parts[d % n_partial] = parts[d % n_partial] + xd * rows[p, pl.ds(base + d * SC_LANES, SC_LANES)]
                        s = jnp.sum((parts[0] + parts[1]) + (parts[2] + parts[3]))
                        return jnp.where(lanes == p, dots + s, dots)

                    dots = lax.fori_loop(0, SC_ROWS, pick, dots)
                out_v[pl.ds(k * PEER_PICKS + c * SC_ROWS, SC_ROWS)] = dots

            _sc_pipeline(idx_v, tab_hbm, ((rows0, sem0), (rows1, sem1)), consume)
            pltpu.sync_copy(out_v, o_hbm.at[pl.ds(tb * PEER_PICKS, SC_BATCH * PEER_PICKS)])

    call = _sc_call(body, jax.ShapeDtypeStruct((ns * PEER_PICKS,), F32), (SC_BATCH, D_MODEL), F32,
                    (SC_BATCH * PEER_PICKS,), D_MODEL, F32)
    return call(experts, x, tabs).reshape(ns, PEER_PICKS)


def _resid_ln_rows_kernel(x_ref, r_ref, g_ref, b_ref, y_in_ref, y_ref):
    del y_in_ref
    y_ref[...] = _ln(ALPHA * x_ref[...] + r_ref[...], g_ref[...], b_ref[...])


def resid_ln_rows(x, r, g, b, y, row0):
    n = r.shape[0] // ROW_TILE
    at = lambda i: (row0 // ROW_TILE + i, 0)
    return pl.pallas_call(
        _resid_ln_rows_kernel,
        grid=(n,),
        in_specs=[pl.BlockSpec((ROW_TILE, D_MODEL), at), _row_spec(ROW_TILE, D_MODEL), _full_spec(g.shape),
                  _full_spec(b.shape), pl.BlockSpec(memory_space=pl.ANY)],
        out_specs=pl.BlockSpec((ROW_TILE, D_MODEL), at),
        out_shape=jax.ShapeDtypeStruct(y.shape, y.dtype),
        input_output_aliases={4: 0},
        compiler_params=_cparams(("parallel",)),
        name="resid_ln_rows",
    )(x, r, g, b, y)


def _pack_kernel(t_ref, o_ref):
    n = t_ref.shape[1]
    for s in range(ROW_WORDS):
        lo = t_ref[0, :, (2 * s) * 128:(2 * s + 1) * 128].astype(BF16).astype(F32)
        hi = t_ref[0, :, (2 * s + 1) * 128:(2 * s + 2) * 128].astype(BF16).astype(F32)
        word = pltpu.bitcast(hi, jnp.uint32) | (pltpu.bitcast(lo, jnp.uint32) >> 16)
        o_ref[pl.ds(s, n, stride=ROW_WORDS), :] = pltpu.bitcast(word, jnp.int32)


def pack_table(tabs, layer):
    e = tabs.shape[1]
    return pl.pallas_call(
        _pack_kernel,
        grid=(e // PACK_TILE,),
        in_specs=[pl.BlockSpec((1, PACK_TILE, D_MODEL), lambda i: (layer, i, 0))],
        out_specs=_row_spec(PACK_TILE * ROW_WORDS, 128),
        out_shape=jax.ShapeDtypeStruct((e * ROW_WORDS, 128), jnp.int32),
        compiler_params=_cparams(("parallel",)),
        name="pack_table",
    )(tabs)


def peer_consts():
    col = jnp.arange(PEER_PICKS * D_CHUNKS) // D_CHUNKS
    spread = (col[None, :] == jnp.arange(PEER_PICKS)[:, None])
    return spread.astype(BF16), jnp.transpose(spread).astype(F32)


def peer_block(x, wq, keys, u_packed, v_packed, u_tabs, layer, spread, fold, g, b):
    T = x.shape[0]
    codes = pair_codes()
    expert0 = layer * PEER_EXPERTS
    eid_s, gate_s, expert_s = peer_route(x, wq, keys, codes, T - SC_U_TOKENS, expert0)
    hdn_s = sc_expert_dots(expert_s, x, T - SC_U_TOKENS, u_tabs.reshape(-1, D_MODEL))
    eid, gate, hdn = peer_route_u(x, wq, keys, codes, u_packed, fold, T - SC_U_TOKENS)
    eid, gate, hdn = (jnp.concatenate(p, axis=0) for p in ((eid, eid_s), (gate, gate_s), (hdn, hdn_s)))
    a_sc, e_sc = sc_gate(eid, hdn, gate, T - SC_V_TOKENS, 0)
    mix_sc = sc_value_mix(e_sc, a_sc, v_packed.reshape(PEER_EXPERTS, PACKED_WORDS))
    y = peer_v(eid, hdn, gate, x, v_packed, spread, g, b, T - SC_V_TOKENS)
    return resid_ln_rows(x, mix_sc, g, b, y, T - SC_V_TOKENS)


def _mla_kv_kernel(x_ref, w_ref, g_ref, cos_ref, sin_ref, c_ref, kpe_ref):
    kv = jnp.dot(x_ref[...].astype(BF16), w_ref[...], preferred_element_type=F32)
    c_ref[...] = _rms(kv[:, :KV_LORA], g_ref[...])
    kp = kv[:, KV_LORA:KV_LORA + QK_ROPE]
    sw = jnp.concatenate([kp[:, QK_ROPE // 2:], kp[:, :QK_ROPE // 2]], axis=1)
    kpe_ref[...] = kp * cos_ref[...] + sw * sin_ref[...]


def mla_kv(x, w, g, cos2, sin2):
    T = x.shape[0]
    return pl.pallas_call(
        _mla_kv_kernel,
        grid=(T // ROW_TILE,),
        in_specs=[_row_spec(ROW_TILE, D_MODEL), _full_spec(w.shape), _full_spec(g.shape),
                  _row_spec(ROW_TILE, QK_ROPE), _row_spec(ROW_TILE, QK_ROPE)],
        out_specs=[_row_spec(ROW_TILE, KV_LORA), _row_spec(ROW_TILE, QK_ROPE)],
        out_shape=[jax.ShapeDtypeStruct((T, KV_LORA), F32), jax.ShapeDtypeStruct((T, QK_ROPE), F32)],
        compiler_params=_cparams(("parallel",)),
        name="mla_kv",
    )(x, w, g, cos2, sin2)


def _mla_expand_kernel(c_ref, kpe_ref, wk_ref, wv_ref, k_ref, v_ref):
    cb = c_ref[...].astype(BF16)
    kn = jnp.dot(cb, wk_ref[...], preferred_element_type=F32)
    v_ref[...] = jnp.dot(cb, wv_ref[...], preferred_element_type=F32).astype(BF16)
    kp = kpe_ref[...]
    kpad = jnp.concatenate([kp, jnp.zeros_like(kp)], axis=1).astype(BF16)
    for h in range(MLA_HEADS):
        k_ref[:, h * HEAD_SLOT:h * HEAD_SLOT + QK_NOPE] = kn[:, h * QK_NOPE:(h + 1) * QK_NOPE].astype(BF16)
        k_ref[:, h * HEAD_SLOT + QK_NOPE:(h + 1) * HEAD_SLOT] = kpad


def mla_expand(c, kpe, wk, wv):
    R = c.shape[0]
    return pl.pallas_call(
        _mla_expand_kernel,
        grid=(R // ROW_TILE,),
        in_specs=[_row_spec(ROW_TILE, KV_LORA), _row_spec(ROW_TILE, QK_ROPE), _full_spec(wk.shape), _full_spec(wv.shape)],
        out_specs=[_row_spec(ROW_TILE, MLA_HEADS * HEAD_SLOT), _row_spec(ROW_TILE, MLA_HEADS * V_HEAD)],
        out_shape=[jax.ShapeDtypeStruct((R, MLA_HEADS * HEAD_SLOT), BF16), jax.ShapeDtypeStruct((R, MLA_HEADS * V_HEAD), BF16)],
        compiler_params=_cparams(("parallel",)),
        name="mla_expand",
    )(c, kpe, wk, wv)


def _mla_q_kernel(x_ref, wdq_ref, g_ref, wuq_ref, cos_ref, sin_ref, q_ref):
    cq = _rms(jnp.dot(x_ref[...].astype(BF16), wdq_ref[...], preferred_element_type=F32), g_ref[...])
    q = jnp.dot(cq.astype(BF16), wuq_ref[...], preferred_element_type=F32)
    q = q * (ATTN_SCALE * math.log2(math.e))
    lane = lax.broadcasted_iota(jnp.int32, (ROW_TILE, 128), 1)
    cos = cos_ref[...]
    sin = sin_ref[...]
    for h in range(MLA_HEADS):
        q_ref[:, h * HEAD_SLOT:h * HEAD_SLOT + QK_NOPE] = q[:, h * HEAD_SLOT:h * HEAD_SLOT + QK_NOPE].astype(BF16)
        seg = q[:, h * HEAD_SLOT + QK_NOPE:(h + 1) * HEAD_SLOT]
        sw = jnp.where(lane < QK_ROPE // 2, pltpu.roll(seg, 128 - QK_ROPE // 2, 1),
                       jnp.where(lane < QK_ROPE, pltpu.roll(seg, QK_ROPE // 2, 1), 0.0))
        q_ref[:, h * HEAD_SLOT + QK_NOPE:(h + 1) * HEAD_SLOT] = (seg * cos + sw * sin).astype(BF16)


def mla_q(x, wdq, g, wuq, cosq, sinq):
    T = x.shape[0]
    return pl.pallas_call(
        _mla_q_kernel,
        grid=(T // ROW_TILE,),
        in_specs=[_row_spec(ROW_TILE, D_MODEL), _full_spec(wdq.shape), _full_spec(g.shape), _full_spec(wuq.shape),
                  _row_spec(ROW_TILE, 128), _row_spec(ROW_TILE, 128)],
        out_specs=_row_spec(ROW_TILE, MLA_HEADS * HEAD_SLOT),
        out_shape=jax.ShapeDtypeStruct((T, MLA_HEADS * HEAD_SLOT), BF16),
        compiler_params=_cparams(("parallel",)),
        name="mla_q",
    )(x, wdq, g, wuq, cosq, sinq)


def _attn_prompt_kernel(q_ref, k_ref, v_ref, o_ref, s_even, s_odd):
    qi = pl.program_id(2)
    q = q_ref[...]
    nt = (((1,), (1,)), ((), ()))

    def scores(ki, dst):
        off = pl.multiple_of(ki * ATTN_TILE, ATTN_TILE)
        dst[...] = lax.dot_general(q, k_ref[pl.ds(off, ATTN_TILE), :], nt, preferred_element_type=F32)

    def update(ki, src, carry, diagonal):
        m, l, acc = carry
        off = pl.multiple_of(ki * ATTN_TILE, ATTN_TILE)
        s = src[...]
        if diagonal:
            qc = lax.broadcasted_iota(jnp.int32, s.shape, 0) // CHUNK
            kc = lax.broadcasted_iota(jnp.int32, s.shape, 1) // CHUNK
            s = jnp.where(kc <= qc, s, -jnp.inf)
        m_new = jnp.maximum(m, jnp.max(s, axis=1, keepdims=True))
        alpha = jnp.exp2(m - m_new)
        p = jnp.exp2(s - m_new)
        l = alpha * l + jnp.sum(p, axis=1, keepdims=True)
        acc = alpha * acc + jnp.dot(p.astype(BF16), v_ref[pl.ds(off, ATTN_TILE), :], preferred_element_type=F32)
        return m_new, l, acc

    init = (jnp.full((ATTN_TILE, 1), -jnp.inf, F32), jnp.zeros((ATTN_TILE, 1), F32), jnp.zeros((ATTN_TILE, V_HEAD), F32))
    scores(0, s_even)

    def pair(j, carry):
        scores(2 * j + 1, s_odd)
        carry = update(2 * j, s_even, carry, False)
        scores(jnp.minimum(2 * j + 2, qi), s_even)
        return update(2 * j + 1, s_odd, carry, False)

    carry = lax.fori_loop(0, qi // 2, pair, init)

    def last_even(carry):
        return update(qi, s_even, carry, True)

    def last_odd(carry):
        scores(qi, s_odd)
        carry = update(qi - 1, s_even, carry, False)
        return update(qi, s_odd, carry, True)

    m, l, acc = lax.cond(qi % 2 == 0, last_even, last_odd, carry)
    o_ref[...] = (acc / l).astype(BF16)


def attn_prompt(q, k, v, batch, seq):
    nq = seq // ATTN_TILE
    return pl.pallas_call(
        _attn_prompt_kernel,
        grid=(batch, MLA_HEADS, nq),
        in_specs=[pl.BlockSpec((ATTN_TILE, HEAD_SLOT), lambda b, h, i: (b * nq + i, h)),
                  pl.BlockSpec((seq, HEAD_SLOT), lambda b, h, i: (b, h)),
                  pl.BlockSpec((seq, V_HEAD), lambda b, h, i: (b, h))],
        out_specs=pl.BlockSpec((ATTN_TILE, V_HEAD), lambda b, h, i: (b * nq + i, h)),
        out_shape=jax.ShapeDtypeStruct((batch * seq, MLA_HEADS * V_HEAD), BF16),
        scratch_shapes=[pltpu.VMEM((ATTN_TILE, ATTN_TILE), F32)] * 2,
        compiler_params=_cparams(("parallel", "parallel", "arbitrary")),
        name="attn_prompt",
    )(q, k, v)


def _attn_sample_kernel(q_ref, k_ref, v_ref, o_ref):
    nt = (((1,), (1,)), ((), ()))
    s = lax.dot_general(q_ref[...], k_ref[...], nt, preferred_element_type=F32)
    p = jnp.exp2(s - jnp.max(s, axis=1, keepdims=True))
    o = jnp.dot(p.astype(BF16), v_ref[...], preferred_element_type=F32)
    o_ref[...] = (o / jnp.sum(p, axis=1, keepdims=True)).astype(BF16)


def attn_sample(q, k, v, batch, q_len, kv_len, q_block0):
    return pl.pallas_call(
        _attn_sample_kernel,
        grid=(batch, MLA_HEADS),
        in_specs=[pl.BlockSpec((q_len, HEAD_SLOT), lambda b, h: (q_block0 + b, h)),
                  pl.BlockSpec((kv_len, HEAD_SLOT), lambda b, h: (b, h)),
                  pl.BlockSpec((kv_len, V_HEAD), lambda b, h: (b, h))],
        out_specs=pl.BlockSpec((q_len, V_HEAD), lambda b, h: (b, h)),
        out_shape=jax.ShapeDtypeStruct((batch * q_len, MLA_HEADS * V_HEAD), BF16),
        compiler_params=_cparams(("parallel", "parallel")),
        name="attn_sample",
    )(q, k, v)


def _proj_ln_kernel(x_ref, o_ref_in, w_ref, g_ref, b_ref, y_ref):
    mix = jnp.dot(o_ref_in[...], w_ref[...], preferred_element_type=F32)
    y_ref[...] = _ln(ALPHA * x_ref[...] + mix, g_ref[...], b_ref[...])


def proj_ln(x, o, w, g, b):
    T = x.shape[0]
    return pl.pallas_call(
        _proj_ln_kernel,
        grid=(T // ROW_TILE,),
        in_specs=[_row_spec(ROW_TILE, D_MODEL), _row_spec(ROW_TILE, o.shape[1]), _full_spec(w.shape),
                  _full_spec(g.shape), _full_spec(b.shape)],
        out_specs=_row_spec(ROW_TILE, D_MODEL),
        out_shape=jax.ShapeDtypeStruct((T, D_MODEL), F32),
        compiler_params=_cparams(("parallel",)),
        name="proj_ln",
    )(x, o, w, g, b)


def _rope_tables(pos):
    inv = 1.0 / (ROPE_BASE ** (jnp.arange(0, QK_ROPE, 2, dtype=F32) / QK_ROPE))
    ang = pos.astype(F32)[:, None] * inv[None, :]
    cos, sin = jnp.cos(ang), jnp.sin(ang)
    cos2 = jnp.concatenate([cos, cos], axis=1)
    sin2 = jnp.concatenate([-sin, sin], axis=1)
    pad = jnp.zeros_like(cos2)
    return cos2, sin2, jnp.concatenate([cos2, pad], axis=1), jnp.concatenate([sin2, pad], axis=1)


def _gm_chunk_mats(w_s, b_s, dec_seq):
    i = jnp.arange(GM_CHUNK)
    mask = (i[None, :] // CHUNK) <= (i[:, None] // CHUNK)
    full = jnp.where(mask[None], w_s, 0.0)
    rep = GM_CHUNK // dec_seq
    blk = w_s[:, :dec_seq, :dec_seq]
    same = (i[:, None] // dec_seq) == (i[None, :] // dec_seq)
    diag = jnp.where(same[None], jnp.tile(blk, (1, rep, rep)), 0.0)
    ws = jnp.stack([full, diag], axis=0).astype(BF16)
    bias = jnp.stack([b_s, jnp.tile(b_s[:, :dec_seq], (1, rep))], axis=0)
    bs = jnp.broadcast_to(bias[..., None], bias.shape + (GM_GROUP_DIM,)).astype(F32)
    return ws, bs


def _pad_q_weight(w_uq):
    w = w_uq.reshape(Q_LORA, MLA_HEADS, QK_NOPE + QK_ROPE)
    w = jnp.pad(w, ((0, 0), (0, 0), (0, HEAD_SLOT - QK_NOPE - QK_ROPE)))
    return w.reshape(Q_LORA, MLA_HEADS * HEAD_SLOT).astype(BF16)


def kernel(x_prompt, x_sample, cache_ckv, cache_kpe, ln1_g, ln1_b, ln2_g, ln2_b, gm_w_in, gm_b_in, gm_ln_g, gm_ln_b,
           gm_w_s, gm_b_s, gm_w_out, mla_w_dkv, mla_kv_norm_g, mla_w_ukv, mla_w_dq, mla_q_norm_g, mla_w_uq, mla_w_o,
           peer_w_q, peer_subkeys, peer_u, peer_v):
    B, S, _ = x_prompt.shape
    DB, DS, _ = x_sample.shape
    PAST = cache_ckv.shape[1]
    TP, TS = B * S, DB * DS
    T = TP + TS
    assert TP % ROW_TILE == 0 and TS % ROW_TILE == 0 and T % PEER_TILE == 0 and S % ATTN_TILE == 0
    for n_sc in (SC_U_TOKENS, SC_V_TOKENS):
        assert n_sc % (SC_SUBCORES * SC_BATCH) == 0 and n_sc % ROW_TILE == 0 and (T - n_sc) % ROW_TILE == 0
    assert GM_CHUNK % DS == 0 and DS <= CHUNK and PAST % CHUNK == 0 and (PAST + DS) % 16 == 0

    x = jnp.concatenate([x_prompt.reshape(TP, D_MODEL), x_sample.reshape(TS, D_MODEL)], axis=0)
    pos = jnp.concatenate([jnp.tile(jnp.arange(S, dtype=jnp.int32), B),
                           jnp.tile(PAST + jnp.arange(DS, dtype=jnp.int32), DB)])
    cos2, sin2, cosq, sinq = _rope_tables(pos)
    spread, fold = peer_consts()
    row = lambda a: a.reshape(1, -1)

    def peer_layer(x, l):
        return peer_block(x, peer_w_q[l].astype(BF16), peer_subkeys[l].astype(BF16), pack_table(peer_u, l),
                          pack_table(peer_v, l), peer_u, l, spread, fold, row(ln2_g[l]), row(ln2_b[l]))

    gm_rows = []
    for l in range(N_A):
        u, v = gm_in(x, gm_w_in[l].astype(BF16), row(gm_b_in[l]), row(gm_ln_g[l]), row(gm_ln_b[l]))
        gm_rows.append(v[TP:].reshape(DB, DS, GM_HALF))
        ws, bs = _gm_chunk_mats(gm_w_s[l], gm_b_s[l], DS)
        x = gm_mix(x, u, v, ws, bs, gm_w_out[l].astype(BF16), row(ln1_g[l]), row(ln1_b[l]), TP // ROW_TILE)
        x = peer_layer(x, l)

    w_dkv = jnp.pad(mla_w_dkv, ((0, 0), (0, 128 - QK_ROPE))).astype(BF16)
    c_new, kpe_new = mla_kv(x, w_dkv, row(mla_kv_norm_g), cos2, sin2)
    w_ukv = mla_w_ukv.reshape(KV_LORA, MLA_HEADS, QK_NOPE + V_HEAD)
    w_uk = w_ukv[:, :, :QK_NOPE].reshape(KV_LORA, MLA_HEADS * QK_NOPE).astype(BF16)
    w_uv = w_ukv[:, :, QK_NOPE:].reshape(KV_LORA, MLA_HEADS * V_HEAD).astype(BF16)
    kp_cat, vp = mla_expand(c_new[:TP], kpe_new[:TP], w_uk, w_uv)
    KV = PAST + DS
    c_all = jnp.concatenate([cache_ckv, c_new[TP:].reshape(DB, DS, KV_LORA)], axis=1).reshape(DB * KV, KV_LORA)
    kpe_all = jnp.concatenate([cache_kpe, kpe_new[TP:].reshape(DB, DS, QK_ROPE)], axis=1).reshape(DB * KV, QK_ROPE)
    ks_cat, vs = mla_expand(c_all, kpe_all, w_uk, w_uv)

    for j in range(DEPTH - N_A):
        l = N_A + j
        q = mla_q(x, mla_w_dq[j].astype(BF16), row(mla_q_norm_g[j]), _pad_q_weight(mla_w_uq[j]), cosq, sinq)
        o = jnp.concatenate([attn_prompt(q, kp_cat, vp, B, S),
                             attn_sample(q, ks_cat, vs, DB, DS, KV, TP // DS)], axis=0)
        x = proj_ln(x, o, mla_w_o[j].astype(BF16), row(ln1_g[l]), row(ln1_b[l]))
        x = peer_layer(x, l)

    return (x[:TP].reshape(B, S, D_MODEL), x[TP:].reshape(DB, DS, D_MODEL), jnp.stack(gm_rows, axis=0),
            c_new[:TP].reshape(B, S, KV_LORA), kpe_new[:TP].reshape(B, S, QK_ROPE),
            c_new[TP:].reshape(DB, DS, KV_LORA), kpe_new[TP:].reshape(DB, DS, QK_ROPE))
```

```python
import functools
import math

import jax
import jax.numpy as jnp
from jax import lax
from jax.experimental import pallas as pl
from jax.experimental.pallas import tpu as pltpu
from jax.experimental.pallas import tpu_sc as plsc

F32 = jnp.float32
BF16 = jnp.bfloat16

D_MODEL = 1024
DEPTH = 4
N_A = DEPTH // 2
CHUNK = 64
ALPHA = (2.0 * DEPTH) ** 0.25
LN_EPS = 1e-5
RMS_EPS = 1e-6

GM_CHUNK = 128
GM_HALF = 2 * D_MODEL
GM_GROUPS = 8
GM_GROUP_DIM = GM_HALF // GM_GROUPS

MLA_HEADS = 8
QK_NOPE = 128
QK_ROPE = 64
V_HEAD = 128
KV_LORA = D_MODEL // 4
Q_LORA = 3 * D_MODEL // 8
ROPE_BASE = 10000.0
ATTN_SCALE = (QK_NOPE + QK_ROPE) ** -0.5
HEAD_SLOT = 256

PEER_HEADS = 8
PEER_NKEYS = 128
PEER_EXPERTS = PEER_NKEYS * PEER_NKEYS
PEER_DK = 256
PEER_TOPK = 16
PEER_PICKS = PEER_HEADS * PEER_TOPK
TOPK_SHIFT = PEER_TOPK.bit_length() - 1
PAIR_SENTINEL = 1 << 20
ROW_WORDS = 4
D_CHUNKS = D_MODEL // 128

ROW_TILE = 256
PEER_TILE = 128
PACK_TILE = 512
SC_U_TOKENS = 13312
SC_U_SPLIT = 6144
SC_V_TOKENS = 12288
SC_BATCH = 32
PACKED_WORDS = D_MODEL // 2
SC_SUBCORES = 32
SC_LANES = 16
SC_ROWS = 16
V_TOKENS_PER_TRIP = 8
ATTN_TILE = 512
VMEM_LIMIT = 56 * 1024 * 1024


def _cparams(sem):
    return pltpu.CompilerParams(dimension_semantics=sem, vmem_limit_bytes=VMEM_LIMIT)


def _gelu(x):
    return 0.5 * x * (1.0 + lax.erf(x * (1.0 / math.sqrt(2.0))))


def _ln(x, g, b):
    mu = jnp.mean(x, axis=-1, keepdims=True)
    xc = x - mu
    var = jnp.mean(xc * xc, axis=-1, keepdims=True)
    return xc * lax.rsqrt(var + LN_EPS) * g + b


def _rms(x, g):
    return x * lax.rsqrt(jnp.mean(x * x, axis=-1, keepdims=True) + RMS_EPS) * g


def _row_spec(tile, width):
    return pl.BlockSpec((tile, width), lambda i: (i, 0))


def _full_spec(shape):
    nd = len(shape)
    return pl.BlockSpec(shape, lambda i: (0,) * nd)


def _gm_in_kernel(x_ref, w_ref, b_ref, lg_ref, lb_ref, u_ref, v_ref):
    z = jnp.dot(x_ref[...].astype(BF16), w_ref[...], preferred_element_type=F32) + b_ref[...]
    z = _gelu(z)
    u_ref[...] = z[:, :GM_HALF]
    v_ref[...] = _ln(z[:, GM_HALF:], lg_ref[...], lb_ref[...])


def gm_in(x, w, b, lg, lb):
    T = x.shape[0]
    return pl.pallas_call(
        _gm_in_kernel,
        grid=(T // ROW_TILE,),
        in_specs=[_row_spec(ROW_TILE, D_MODEL), _full_spec(w.shape), _full_spec(b.shape),
                  _full_spec(lg.shape), _full_spec(lb.shape)],
        out_specs=[_row_spec(ROW_TILE, GM_HALF), _row_spec(ROW_TILE, GM_HALF)],
        out_shape=[jax.ShapeDtypeStruct((T, GM_HALF), F32)] * 2,
        compiler_params=_cparams(("parallel",)),
        name="gm_in",
    )(x, w, b, lg, lb)


def _gm_mix_kernel(x_ref, u_ref, v_ref, ws_ref, bs_ref, wo_ref, g_ref, b_ref, o_ref, s_scr):
    for c in range(ROW_TILE // GM_CHUNK):
        r = slice(c * GM_CHUNK, (c + 1) * GM_CHUNK)
        for g in range(GM_GROUPS):
            cs = slice(g * GM_GROUP_DIM, (g + 1) * GM_GROUP_DIM)
            sv = jnp.dot(ws_ref[0, g], v_ref[r, cs].astype(BF16), preferred_element_type=F32) + bs_ref[0, g]
            s_scr[r, cs] = (u_ref[r, cs] * sv).astype(BF16)
    mix = jnp.dot(s_scr[...], wo_ref[...], preferred_element_type=F32)
    o_ref[...] = _ln(ALPHA * x_ref[...] + mix, g_ref[...], b_ref[...])


def gm_mix(x, u, v, ws, bs, wo, g, b, n_prompt_tiles):
    T = x.shape[0]
    sel = lambda i: (jnp.where(i >= n_prompt_tiles, 1, 0), 0, 0, 0)
    return pl.pallas_call(
        _gm_mix_kernel,
        grid=(T // ROW_TILE,),
        in_specs=[_row_spec(ROW_TILE, D_MODEL), _row_spec(ROW_TILE, GM_HALF), _row_spec(ROW_TILE, GM_HALF),
                  pl.BlockSpec((1,) + ws.shape[1:], sel), pl.BlockSpec((1,) + bs.shape[1:], sel),
                  _full_spec(wo.shape), _full_spec(g.shape), _full_spec(b.shape)],
        out_specs=_row_spec(ROW_TILE, D_MODEL),
        out_shape=jax.ShapeDtypeStruct((T, D_MODEL), F32),
        scratch_shapes=[pltpu.VMEM((ROW_TILE, GM_HALF), BF16)],
        compiler_params=_cparams(("parallel",)),
        name="gm_mix",
    )(x, u, v, ws, bs, wo, g, b)


def _extract_max(s, codes, sentinel):
    m = jnp.max(s, axis=0, keepdims=True)
    c = jnp.min(jnp.where(s == m, codes, sentinel), axis=0, keepdims=True)
    return m, c, jnp.where(codes == c, -jnp.inf, s)


def _topk_rows(s):
    rows = lax.broadcasted_iota(jnp.int32, s.shape, 0).astype(F32)
    vals, idxs = [], []
    for _ in range(PEER_TOPK):
        m, idx, s = _extract_max(s, rows, float(s.shape[0]))
        vals.append(m)
        idxs.append(idx)
    return jnp.concatenate(vals, axis=0), jnp.concatenate(idxs, axis=0).astype(jnp.int32)


_PAIR_GROUPS = ((None, 0, 16), (None, 1, 8), (0, None, 16), (1, None, 8), (2, None, 8), (3, None, 8), (4, None, 8))


def pair_codes():
    codes = []
    for b_fix, a_fix, n in _PAIR_GROUPS:
        for r in range(n):
            a, b = (a_fix, r) if b_fix is None else (r, b_fix)
            first = a_fix is not None or a >= 2
            ok = first and (a + 1) * (b + 1) <= PEER_TOPK
            codes.append(a * PEER_TOPK + b if ok else PAIR_SENTINEL)
    assert sorted(c for c in codes if c != PAIR_SENTINEL) == sorted(
        a * PEER_TOPK + b for a in range(PEER_TOPK) for b in range(PEER_TOPK) if (a + 1) * (b + 1) <= PEER_TOPK)
    return jnp.broadcast_to(jnp.asarray(codes, F32)[:, None], (len(codes), PEER_TILE))


def _topk_pairs(v1, i1, v2, i2, codes):
    parts = []
    for b_fix, a_fix, n in _PAIR_GROUPS:
        if b_fix is None:
            parts.append(v1[a_fix:a_fix + 1] + v2[0:n])
        else:
            parts.append(v1[0:n] + v2[b_fix:b_fix + 1])
    cand = jnp.where(codes < PAIR_SENTINEL, jnp.concatenate(parts, axis=0), -jnp.inf)
    krow = lax.broadcasted_iota(jnp.int32, v1.shape, 0)
    vals, k1, k2 = [], [], []
    for _ in range(PEER_TOPK):
        m, c, cand = _extract_max(cand, codes, float(PAIR_SENTINEL))
        vals.append(m)
        c = c.astype(jnp.int32)
        k1.append(jnp.sum(jnp.where(krow == (c >> TOPK_SHIFT), i1, 0), axis=0, keepdims=True))
        k2.append(jnp.sum(jnp.where(krow == (c & (PEER_TOPK - 1)), i2, 0), axis=0, keepdims=True))
    return jnp.concatenate(vals, axis=0), jnp.concatenate(k1, axis=0), jnp.concatenate(k2, axis=0)


def _peer_route_kernel(x_ref, wq_ref, k_ref, codes_ref, eid_ref, gate_ref, expert_ref, *, expert0):
    q = jnp.dot(x_ref[...].astype(BF16), wq_ref[...], preferred_element_type=F32)
    half = PEER_DK // 2
    nt = (((1,), (1,)), ((), ()))
    codes = codes_ref[...]
    experts, gates = [], []
    for h in range(PEER_HEADS):
        q1 = q[:, h * PEER_DK:h * PEER_DK + half].astype(BF16)
        q2 = q[:, h * PEER_DK + half:(h + 1) * PEER_DK].astype(BF16)
        s1 = lax.dot_general(k_ref[0], q1, nt, preferred_element_type=F32)
        s2 = lax.dot_general(k_ref[1], q2, nt, preferred_element_type=F32)
        v1, i1 = _topk_rows(s1)
        v2, i2 = _topk_rows(s2)
        sc, e1, e2 = _topk_pairs(v1, i1, v2, i2, codes)
        e = jnp.exp(sc - sc[0:1])
        gates.append(e / jnp.sum(e, axis=0, keepdims=True))
        experts.append(e1 * PEER_NKEYS + e2)
    expert = jnp.transpose(jnp.concatenate(experts, axis=0))
    eid_ref[...] = expert * ROW_WORDS
    expert_ref[...] = expert + expert0
    gate_ref[...] = jnp.transpose(jnp.concatenate(gates, axis=0))


def peer_route(x, wq, keys, codes, row0, n_rows, expert0):
    n = n_rows // PEER_TILE
    tile = (PEER_TILE, PEER_PICKS)
    return pl.pallas_call(
        functools.partial(_peer_route_kernel, expert0=expert0),
        grid=(n,),
        in_specs=[pl.BlockSpec((PEER_TILE, D_MODEL), lambda i: (row0 // PEER_TILE + i, 0)), _full_spec(wq.shape),
                  _full_spec(keys.shape), _full_spec(codes.shape)],
        out_specs=[_row_spec(*tile), _row_spec(*tile), _row_spec(*tile)],
        out_shape=[jax.ShapeDtypeStruct((n * PEER_TILE, PEER_PICKS), jnp.int32),
                   jax.ShapeDtypeStruct((n * PEER_TILE, PEER_PICKS), F32),
                   jax.ShapeDtypeStruct((n * PEER_TILE, PEER_PICKS), jnp.int32)],
        compiler_params=_cparams(("parallel",)),
        name="peer_route",
    )(x, wq, keys, codes)


def _gather_rows(eid_ref, t, tab_ref, dst_ref):
    picks = eid_ref.at[t]
    for p in range(PEER_PICKS):
        row = pl.multiple_of(picks[p], ROW_WORDS)
        dst_ref[pl.ds(ROW_WORDS * p, ROW_WORDS), :] = tab_ref[pl.ds(row, ROW_WORDS), :]


def _chunk_mask():
    sub = lax.broadcasted_iota(jnp.int32, (D_CHUNKS, PEER_PICKS * D_CHUNKS), 0)
    col = lax.broadcasted_iota(jnp.int32, (D_CHUNKS, PEER_PICKS * D_CHUNKS), 1)
    return ((col & (D_CHUNKS - 1)) == sub).astype(F32)


def _for_tokens_pipelined(eid_ref, tab_ref, bufs, compute, tokens_per_trip):
    _gather_rows(eid_ref, 0, tab_ref, bufs[0])

    def body(i, carry):
        t0 = tokens_per_trip * i
        for k in range(tokens_per_trip):
            _gather_rows(eid_ref, jnp.minimum(t0 + k + 1, PEER_TILE - 1), tab_ref, bufs[(k + 1) % 2])
            compute(t0 + k, bufs[k % 2])
        return carry

    lax.fori_loop(0, PEER_TILE // tokens_per_trip, body, 0)


def _peer_route_u_kernel(xr_ref, xc_ref, wq_ref, k_ref, codes_ref, tab_ref, fold_ref, zeros_ref,
                         eid_ref, gate_ref, h_ref, q_scr, eid_t, gate_t, eid_vmem, eid_smem, hrep, sem, *bufs):
    i = pl.program_id(0)
    half = PEER_DK // 2
    nt = (((1,), (1,)), ((), ()))
    eid_to_smem = pltpu.make_async_copy(eid_vmem, eid_smem, sem)

    @pl.when(i == 0)
    def _():
        pltpu.sync_copy(zeros_ref, eid_smem)

    @pl.when(i > 0)
    def _():
        eid_to_smem.wait()

    q = jnp.dot(xr_ref[...].astype(BF16), wq_ref[...], preferred_element_type=F32)
    for h in range(PEER_HEADS):
        q_scr[h] = q[:, h * PEER_DK:(h + 1) * PEER_DK]
    codes = codes_ref[...]
    mask = _chunk_mask()

    def compute(t, rows_ref):
        u = pltpu.bitcast(rows_ref[...], BF16)
        xr = xc_ref[pl.ds(t, 1), :]
        x8 = jnp.concatenate([xr[:, j * 128:(j + 1) * 128] for j in range(D_CHUNKS)], axis=0)
        hi = x8.astype(BF16)
        lo = (x8 - hi.astype(F32)).astype(BF16)
        o = lax.dot_general(jnp.concatenate([hi, lo], axis=0), u, nt, preferred_element_type=F32)
        o8 = (o[0:D_CHUNKS] + o[D_CHUNKS:2 * D_CHUNKS]) * mask
        hrep[pl.ds(t, 1), :] = jnp.sum(o8, axis=0, keepdims=True)

    per_trip = PEER_TILE // PEER_HEADS
    _gather_rows(eid_smem, 0, tab_ref, bufs[0])

    def body(h, carry):
        qh = q_scr[h]
        s1 = lax.dot_general(k_ref[0], qh[:, :half].astype(BF16), nt, preferred_element_type=F32)
        s2 = lax.dot_general(k_ref[1], qh[:, half:].astype(BF16), nt, preferred_element_type=F32)
        v1, i1 = _topk_rows(s1)
        v2, i2 = _topk_rows(s2)
        sc, e1, e2 = _topk_pairs(v1, i1, v2, i2, codes)
        e = jnp.exp(sc - sc[0:1])
        gate_t[h] = e / jnp.sum(e, axis=0, keepdims=True)
        eid_t[h] = (e1 * PEER_NKEYS + e2) * ROW_WORDS
        t0 = per_trip * h
        for k in range(per_trip):
            _gather_rows(eid_smem, jnp.minimum(t0 + k + 1, PEER_TILE - 1), tab_ref, bufs[(k + 1) % 2])
            compute(t0 + k, bufs[k % 2])
        return carry

    lax.fori_loop(0, PEER_HEADS, body, 0)
    h_ref[...] = jnp.dot(hrep[...], fold_ref[...], precision=lax.Precision.HIGHEST, preferred_element_type=F32)
    eid = jnp.transpose(eid_t[...].reshape(PEER_PICKS, PEER_TILE))
    eid_ref[...] = eid
    eid_vmem[...] = eid
    gate_ref[...] = jnp.transpose(gate_t[...].reshape(PEER_PICKS, PEER_TILE))
    eid_to_smem.start()

    @pl.when(i == pl.num_programs(0) - 1)
    def _():
        eid_to_smem.wait()


def _peer_v_kernel(eid_ref, h_ref, gate_ref, x_ref, tab_ref, spread_ref, g_ref, b_ref, y_ref, a_hi, a_lo, mix, *bufs):
    a = gate_ref[...] * _gelu(h_ref[...])
    hi = a.astype(BF16)
    lo = (a - hi.astype(F32)).astype(BF16)
    a_hi[...] = jnp.dot(hi, spread_ref[...], preferred_element_type=F32)
    a_lo[...] = jnp.dot(lo, spread_ref[...], preferred_element_type=F32)
    mask = _chunk_mask()

    def compute(t, rows_ref):
        v = pltpu.bitcast(rows_ref[...], BF16)
        lhs = jnp.concatenate([(a_hi[pl.ds(t, 1), :] * mask).astype(BF16),
                               (a_lo[pl.ds(t, 1), :] * mask).astype(BF16)], axis=0)
        o = jnp.dot(lhs, v, preferred_element_type=F32)
        o8 = o[0:D_CHUNKS] + o[D_CHUNKS:2 * D_CHUNKS]
        mix[pl.ds(t, 1), :] = jnp.concatenate([o8[j:j + 1] for j in range(D_CHUNKS)], axis=1)

    _for_tokens_pipelined(eid_ref, tab_ref, bufs, compute, V_TOKENS_PER_TRIP)
    y_ref[...] = _ln(ALPHA * x_ref[...] + mix[...], g_ref[...], b_ref[...])


def _peer_specs():
    eid_spec = pl.BlockSpec((PEER_TILE, PEER_PICKS), lambda i: (i, 0), memory_space=pltpu.SMEM)
    tab_spec = pl.BlockSpec((PEER_EXPERTS * ROW_WORDS, 128), lambda i: (0, 0), pipeline_mode=pl.Buffered(1))
    bufs = [pltpu.VMEM((PEER_PICKS * ROW_WORDS, 128), jnp.int32)] * 2
    return eid_spec, tab_spec, bufs


def peer_route_u(x, wq, keys, codes, tab, fold, n_tokens):
    T = n_tokens
    n = T // PEER_TILE
    _, tab_spec, bufs = _peer_specs()
    cur = lambda i: (jnp.minimum(i, n - 1), 0)
    prev = lambda i: (jnp.maximum(i - 1, 0), 0)
    zeros = jnp.zeros((PEER_TILE, PEER_PICKS), jnp.int32)
    tile = (PEER_TILE, PEER_PICKS)
    return pl.pallas_call(
        _peer_route_u_kernel,
        grid=(n + 1,),
        in_specs=[pl.BlockSpec((PEER_TILE, D_MODEL), cur), pl.BlockSpec((PEER_TILE, D_MODEL), prev),
                  _full_spec(wq.shape), _full_spec(keys.shape), _full_spec(codes.shape), tab_spec,
                  _full_spec(fold.shape), _full_spec(zeros.shape)],
        out_specs=[pl.BlockSpec(tile, cur), pl.BlockSpec(tile, cur), pl.BlockSpec(tile, prev)],
        out_shape=[jax.ShapeDtypeStruct((T, PEER_PICKS), jnp.int32), jax.ShapeDtypeStruct((T, PEER_PICKS), F32),
                   jax.ShapeDtypeStruct((T, PEER_PICKS), F32)],
        scratch_shapes=[pltpu.VMEM((PEER_HEADS, PEER_TILE, PEER_DK), F32),
                        pltpu.VMEM((PEER_HEADS, PEER_TOPK, PEER_TILE), jnp.int32),
                        pltpu.VMEM((PEER_HEADS, PEER_TOPK, PEER_TILE), F32),
                        pltpu.VMEM(tile, jnp.int32), pltpu.SMEM(tile, jnp.int32),
                        pltpu.VMEM((PEER_TILE, PEER_PICKS * D_CHUNKS), F32),
                        pltpu.SemaphoreType.DMA(())] + bufs,
        compiler_params=_cparams(("arbitrary",)),
        name="peer_route_u",
    )(x, x, wq, keys, codes, tab, fold, zeros)


def peer_v(eid, hdn, gate, x, tab, spread, g, b, n_tokens):
    T = eid.shape[0]
    eid_spec, tab_spec, bufs = _peer_specs()
    return pl.pallas_call(
        _peer_v_kernel,
        grid=(n_tokens // PEER_TILE,),
        in_specs=[eid_spec, _row_spec(PEER_TILE, PEER_PICKS), _row_spec(PEER_TILE, PEER_PICKS),
                  _row_spec(PEER_TILE, D_MODEL), tab_spec, _full_spec(spread.shape), _full_spec(g.shape), _full_spec(b.shape)],
        out_specs=_row_spec(PEER_TILE, D_MODEL),
        out_shape=jax.ShapeDtypeStruct((T, D_MODEL), F32),
        scratch_shapes=[pltpu.VMEM((PEER_TILE, PEER_PICKS * D_CHUNKS), F32)] * 2 + [pltpu.VMEM((PEER_TILE, D_MODEL), F32)] + bufs,
        compiler_params=_cparams(("arbitrary",)),
        name="peer_v",
    )(eid, hdn, gate, x, tab, spread, g, b)


def _sc_gate_kernel(eid_ref, h_ref, gate_ref, a_ref, e_ref, *, expert0):
    a_ref[...] = gate_ref[...] * _gelu(h_ref[...])
    e_ref[...] = eid_ref[...] // ROW_WORDS + expert0


def sc_gate(eid, hdn, gate, row0, expert0):
    n = (eid.shape[0] - row0) // PEER_TILE
    tile = (PEER_TILE, PEER_PICKS)
    spec = pl.BlockSpec(tile, lambda i: (row0 // PEER_TILE + i, 0))
    return pl.pallas_call(
        functools.partial(_sc_gate_kernel, expert0=expert0),
        grid=(n,),
        in_specs=[spec, spec, spec],
        out_specs=[_row_spec(*tile), _row_spec(*tile)],
        out_shape=[jax.ShapeDtypeStruct((n * PEER_TILE, PEER_PICKS), F32),
                   jax.ShapeDtypeStruct((n * PEER_TILE, PEER_PICKS), jnp.int32)],
        compiler_params=_cparams(("parallel",)),
        name="sc_gate",
    )(eid, hdn, gate)


def _sc_pipeline(idx_v, tab_hbm, bufs, consume):
    n_chunks = PEER_PICKS // SC_ROWS

    def gather(g, slot):
        rows, sem = bufs[slot]
        idx = idx_v.at[g // n_chunks, pl.ds((g % n_chunks) * SC_ROWS, SC_ROWS)]
        return pltpu.make_async_copy(tab_hbm.at[idx], rows, sem)

    total = SC_BATCH * n_chunks
    gather(0, 0).start()

    @pl.loop(0, total // 2)
    def _(j):
        g = 2 * j
        gather(g + 1, 1).start()
        gather(g, 0).wait()
        consume(g, bufs[0][0])

        @pl.when(g + 2 < total)
        def _():
            gather(g + 2, 0).start()

        gather(g + 1, 1).wait()
        consume(g + 1, bufs[1][0])


def _sc_call(body, out_type, stage_shape, stage_dtype, result_shape, row_width, row_dtype):
    mesh = plsc.VectorSubcoreMesh(core_axis_name="c", subcore_axis_name="s")
    return pl.kernel(
        body, out_type=out_type, mesh=mesh,
        scratch_types=[pltpu.VMEM((SC_BATCH, PEER_PICKS), jnp.int32), pltpu.VMEM(stage_shape, stage_dtype),
                       pltpu.VMEM((SC_ROWS, row_width), row_dtype), pltpu.VMEM((SC_ROWS, row_width), row_dtype),
                       pltpu.VMEM(result_shape, F32), pltpu.SemaphoreType.DMA, pltpu.SemaphoreType.DMA],
        compiler_params=pltpu.CompilerParams(needs_layout_passes=False))


def _sc_first_token(n_rows):
    return (lax.axis_index("c") * (SC_SUBCORES // 2) + lax.axis_index("s")) * (n_rows // SC_SUBCORES)


def _packed_word_groups(half_idx):
    return [(s * 128 + l0, 256 * s + l0, 256 * s + 128 + l0)
            for s in (2 * half_idx, 2 * half_idx + 1) for l0 in range(0, 128, SC_LANES)]


def sc_value_mix(experts, a, tab):
    ns = experts.shape[0]
    n_chunks = PEER_PICKS // SC_ROWS

    def body(e_hbm, a_hbm, tab_hbm, o_hbm, idx_v, a_v, rows0, rows1, out_v, sem0, sem1):
        t0 = _sc_first_token(ns)

        @pl.loop(0, ns // SC_SUBCORES // SC_BATCH)
        def _(bi):
            tb = t0 + bi * SC_BATCH
            pltpu.sync_copy(e_hbm.at[pl.ds(tb, SC_BATCH)], idx_v)
            pltpu.sync_copy(a_hbm.at[pl.ds(tb * PEER_PICKS, SC_BATCH * PEER_PICKS)], a_v)

            def consume(g, rows):
                k = g // n_chunks
                c = g % n_chunks
                for hf in range(2):
                    groups = _packed_word_groups(hf)
                    accs = tuple(jnp.where(c == 0, 0.0, out_v[k, pl.ds(d, SC_LANES)])
                                 for _, d_lo, d_hi in groups for d in (d_lo, d_hi))

                    def pick(p, accs, groups=groups):
                        lane_idx = jnp.zeros((SC_LANES,), jnp.int32) + (k * PEER_PICKS + c * SC_ROWS + p)
                        ap = plsc.load_gather(a_v, [lane_idx])
                        new = []
                        for i, (w, _, _) in enumerate(groups):
                            words = rows[p, pl.ds(w, SC_LANES)]
                            new.append(accs[2 * i] + ap * lax.bitcast_convert_type(words << 16, F32))
                            new.append(accs[2 * i + 1] + ap * lax.bitcast_convert_type(words & -65536, F32))
                        return tuple(new)

                    accs = lax.fori_loop(0, SC_ROWS, pick, accs)
                    for i, (_, d_lo, d_hi) in enumerate(groups):
                        out_v[k, pl.ds(d_lo, SC_LANES)] = accs[2 * i]
                        out_v[k, pl.ds(d_hi, SC_LANES)] = accs[2 * i + 1]

            _sc_pipeline(idx_v, tab_hbm, ((rows0, sem0), (rows1, sem1)), consume)
            pltpu.sync_copy(out_v, o_hbm.at[pl.ds(tb, SC_BATCH)])

    call = _sc_call(body, jax.ShapeDtypeStruct((ns, D_MODEL), F32), (SC_BATCH * PEER_PICKS,), F32,
                    (SC_BATCH, D_MODEL), PACKED_WORDS, jnp.int32)
    return call(experts, a.reshape(-1), tab)


def sc_expert_dots(experts, x, row0, tabs):
    ns = experts.shape[0]
    n_chunks = PEER_PICKS // SC_ROWS
    half = D_MODEL // 2
    n_partial = 4

    def body(e_hbm, x_hbm, tab_hbm, o_hbm, idx_v, x_v, rows0, rows1, out_v, sem0, sem1):
        t0 = _sc_first_token(ns)
        lanes = lax.iota(jnp.int32, SC_LANES)

        @pl.loop(0, ns // SC_SUBCORES // SC_BATCH)
        def _(bi):
            tb = t0 + bi * SC_BATCH
            pltpu.sync_copy(e_hbm.at[pl.ds(tb, SC_BATCH)], idx_v)
            pltpu.sync_copy(x_hbm.at[pl.ds(row0 + tb, SC_BATCH)], x_v)

            def consume(g, rows):
                k = g // n_chunks
                c = g % n_chunks
                dots = jnp.zeros((SC_LANES,), F32)
                for base in (0, half):
                    xs = tuple(x_v[k, pl.ds(base + d * SC_LANES, SC_LANES)] for d in range(half // SC_LANES))

                    def pick(p, dots, base=base, xs=xs):
                        parts = [jnp.zeros((SC_LANES,), F32)] * n_partial
                        for d, xd in enumerate(xs):
                            parts[d % n_partial] = parts[d % n_partial] + xd * rows[p, pl.ds(base + d * SC_LANES, SC_LANES)]
                        s = jnp.sum((parts[0] + parts[1]) + (parts[2] + parts[3]))
                        return jnp.where(lanes == p, dots + s, dots)

                    dots = lax.fori_loop(0, SC_ROWS, pick, dots)
                out_v[pl.ds(k * PEER_PICKS + c * SC_ROWS, SC_ROWS)] = dots

            _sc_pipeline(idx_v, tab_hbm, ((rows0, sem0), (rows1, sem1)), consume)
            pltpu.sync_copy(out_v, o_hbm.at[pl.ds(tb * PEER_PICKS, SC_BATCH * PEER_PICKS)])

    call = _sc_call(body, jax.ShapeDtypeStruct((ns * PEER_PICKS,), F32), (SC_BATCH, D_MODEL), F32,
                    (SC_BATCH * PEER_PICKS,), D_MODEL, F32)
    return call(experts, x, tabs).reshape(ns, PEER_PICKS)


def _resid_ln_rows_kernel(x_ref, r_ref, g_ref, b_ref, y_in_ref, y_ref):
    del y_in_ref
    y_ref[...] = _ln(ALPHA * x_ref[...] + r_ref[...], g_ref[...], b_ref[...])


def resid_ln_rows(x, r, g, b, y, row0):
    n = r.shape[0] // ROW_TILE
    at = lambda i: (row0 // ROW_TILE + i, 0)
    return pl.pallas_call(
        _resid_ln_rows_kernel,
        grid=(n,),
        in_specs=[pl.BlockSpec((ROW_TILE, D_MODEL), at), _row_spec(ROW_TILE, D_MODEL), _full_spec(g.shape),
                  _full_spec(b.shape), pl.BlockSpec(memory_space=pl.ANY)],
        out_specs=pl.BlockSpec((ROW_TILE, D_MODEL), at),
        out_shape=jax.ShapeDtypeStruct(y.shape, y.dtype),
        input_output_aliases={4: 0},
        compiler_params=_cparams(("parallel",)),
        name="resid_ln_rows",
    )(x, r, g, b, y)


def _pack_kernel(t_ref, o_ref):
    n = t_ref.shape[1]
    for s in range(ROW_WORDS):
        lo = t_ref[0, :, (2 * s) * 128:(2 * s + 1) * 128].astype(BF16).astype(F32)
        hi = t_ref[0, :, (2 * s + 1) * 128:(2 * s + 2) * 128].astype(BF16).astype(F32)
        word = pltpu.bitcast(hi, jnp.uint32) | (pltpu.bitcast(lo, jnp.uint32) >> 16)
        o_ref[pl.ds(s, n, stride=ROW_WORDS), :] = pltpu.bitcast(word, jnp.int32)


def pack_table(tabs, layer):
    e = tabs.shape[1]
    return pl.pallas_call(
        _pack_kernel,
        grid=(e // PACK_TILE,),
        in_specs=[pl.BlockSpec((1, PACK_TILE, D_MODEL), lambda i: (layer, i, 0))],
        out_specs=_row_spec(PACK_TILE * ROW_WORDS, 128),
        out_shape=jax.ShapeDtypeStruct((e * ROW_WORDS, 128), jnp.int32),
        compiler_params=_cparams(("parallel",)),
        name="pack_table",
    )(tabs)


def peer_consts():
    col = jnp.arange(PEER_PICKS * D_CHUNKS) // D_CHUNKS
    spread = (col[None, :] == jnp.arange(PEER_PICKS)[:, None])
    return spread.astype(BF16), jnp.transpose(spread).astype(F32)


def peer_block(x, wq, keys, u_packed, v_packed, u_tabs, layer, spread, fold, g, b):
    T = x.shape[0]
    codes = pair_codes()
    expert0 = layer * PEER_EXPERTS
    parts = []
    for row0, n_rows in ((T - SC_U_TOKENS, SC_U_SPLIT), (T - SC_U_TOKENS + SC_U_SPLIT, SC_U_TOKENS - SC_U_SPLIT)):
        eid_s, gate_s, expert_s = peer_route(x, wq, keys, codes, row0, n_rows, expert0)
        parts.append((eid_s, gate_s, sc_expert_dots(expert_s, x, row0, u_tabs.reshape(-1, D_MODEL))))
    parts.insert(0, peer_route_u(x, wq, keys, codes, u_packed, fold, T - SC_U_TOKENS))
    eid, gate, hdn = (jnp.concatenate(p, axis=0) for p in zip(*parts))
    a_sc, e_sc = sc_gate(eid, hdn, gate, T - SC_V_TOKENS, 0)
    mix_sc = sc_value_mix(e_sc, a_sc, v_packed.reshape(PEER_EXPERTS, PACKED_WORDS))
    y = peer_v(eid, hdn, gate, x, v_packed, spread, g, b, T - SC_V_TOKENS)
    return resid_ln_rows(x, mix_sc, g, b, y, T - SC_V_TOKENS)


def _mla_kv_kernel(x_ref, w_ref, g_ref, cos_ref, sin_ref, c_ref, kpe_ref):
    kv = jnp.dot(x_ref[...].astype(BF16), w_ref[...], preferred_element_type=F32)
    c_ref[...] = _rms(kv[:, :KV_LORA], g_ref[...])
    kp = kv[:, KV_LORA:KV_LORA + QK_ROPE]
    sw = jnp.concatenate([kp[:, QK_ROPE // 2:], kp[:, :QK_ROPE // 2]], axis=1)
    kpe_ref[...] = kp * cos_ref[...] + sw * sin_ref[...]


def mla_kv(x, w, g, cos2, sin2):
    T = x.shape[0]
    return pl.pallas_call(
        _mla_kv_kernel,
        grid=(T // ROW_TILE,),
        in_specs=[_row_spec(ROW_TILE, D_MODEL), _full_spec(w.shape), _full_spec(g.shape),
                  _row_spec(ROW_TILE, QK_ROPE), _row_spec(ROW_TILE, QK_ROPE)],
        out_specs=[_row_spec(ROW_TILE, KV_LORA), _row_spec(ROW_TILE, QK_ROPE)],
        out_shape=[jax.ShapeDtypeStruct((T, KV_LORA), F32), jax.ShapeDtypeStruct((T, QK_ROPE), F32)],
        compiler_params=_cparams(("parallel",)),
        name="mla_kv",
    )(x, w, g, cos2, sin2)


def _mla_expand_kernel(c_ref, kpe_ref, wk_ref, wv_ref, k_ref, v_ref):
    cb = c_ref[...].astype(BF16)
    kn = jnp.dot(cb, wk_ref[...], preferred_element_type=F32)
    v_ref[...] = jnp.dot(cb, wv_ref[...], preferred_element_type=F32).astype(BF16)
    kp = kpe_ref[...]
    kpad = jnp.concatenate([kp, jnp.zeros_like(kp)], axis=1).astype(BF16)
    for h in range(MLA_HEADS):
        k_ref[:, h * HEAD_SLOT:h * HEAD_SLOT + QK_NOPE] = kn[:, h * QK_NOPE:(h + 1) * QK_NOPE].astype(BF16)
        k_ref[:, h * HEAD_SLOT + QK_NOPE:(h + 1) * HEAD_SLOT] = kpad


def mla_expand(c, kpe, wk, wv):
    R = c.shape[0]
    return pl.pallas_call(
        _mla_expand_kernel,
        grid=(R // ROW_TILE,),
        in_specs=[_row_spec(ROW_TILE, KV_LORA), _row_spec(ROW_TILE, QK_ROPE), _full_spec(wk.shape), _full_spec(wv.shape)],
        out_specs=[_row_spec(ROW_TILE, MLA_HEADS * HEAD_SLOT), _row_spec(ROW_TILE, MLA_HEADS * V_HEAD)],
        out_shape=[jax.ShapeDtypeStruct((R, MLA_HEADS * HEAD_SLOT), BF16), jax.ShapeDtypeStruct((R, MLA_HEADS * V_HEAD), BF16)],
        compiler_params=_cparams(("parallel",)),
        name="mla_expand",
    )(c, kpe, wk, wv)


def _mla_q_kernel(x_ref, wdq_ref, g_ref, wuq_ref, cos_ref, sin_ref, q_ref):
    cq = _rms(jnp.dot(x_ref[...].astype(BF16), wdq_ref[...], preferred_element_type=F32), g_ref[...])
    q = jnp.dot(cq.astype(BF16), wuq_ref[...], preferred_element_type=F32)
    q = q * (ATTN_SCALE * math.log2(math.e))
    lane = lax.broadcasted_iota(jnp.int32, (ROW_TILE, 128), 1)
    cos = cos_ref[...]
    sin = sin_ref[...]
    for h in range(MLA_HEADS):
        q_ref[:, h * HEAD_SLOT:h * HEAD_SLOT + QK_NOPE] = q[:, h * HEAD_SLOT:h * HEAD_SLOT + QK_NOPE].astype(BF16)
        seg = q[:, h * HEAD_SLOT + QK_NOPE:(h + 1) * HEAD_SLOT]
        sw = jnp.where(lane < QK_ROPE // 2, pltpu.roll(seg, 128 - QK_ROPE // 2, 1),
                       jnp.where(lane < QK_ROPE, pltpu.roll(seg, QK_ROPE // 2, 1), 0.0))
        q_ref[:, h * HEAD_SLOT + QK_NOPE:(h + 1) * HEAD_SLOT] = (seg * cos + sw * sin).astype(BF16)


def mla_q(x, wdq, g, wuq, cosq, sinq):
    T = x.shape[0]
    return pl.pallas_call(
        _mla_q_kernel,
        grid=(T // ROW_TILE,),
        in_specs=[_row_spec(ROW_TILE, D_MODEL), _full_spec(wdq.shape), _full_spec(g.shape), _full_spec(wuq.shape),
                  _row_spec(ROW_TILE, 128), _row_spec(ROW_TILE, 128)],
        out_specs=_row_spec(ROW_TILE, MLA_HEADS * HEAD_SLOT),
        out_shape=jax.ShapeDtypeStruct((T, MLA_HEADS * HEAD_SLOT), BF16),
        compiler_params=_cparams(("parallel",)),
        name="mla_q",
    )(x, wdq, g, wuq, cosq, sinq)


def _attn_prompt_kernel(q_ref, k_ref, v_ref, o_ref, s_even, s_odd):
    qi = pl.program_id(2)
    q = q_ref[...]
    nt = (((1,), (1,)), ((), ()))

    def scores(ki, dst):
        off = pl.multiple_of(ki * ATTN_TILE, ATTN_TILE)
        dst[...] = lax.dot_general(q, k_ref[pl.ds(off, ATTN_TILE), :], nt, preferred_element_type=F32)

    def update(ki, src, carry, diagonal):
        m, l, acc = carry
        off = pl.multiple_of(ki * ATTN_TILE, ATTN_TILE)
        s = src[...]
        if diagonal:
            qc = lax.broadcasted_iota(jnp.int32, s.shape, 0) // CHUNK
            kc = lax.broadcasted_iota(jnp.int32, s.shape, 1) // CHUNK
            s = jnp.where(kc <= qc, s, -jnp.inf)
        m_new = jnp.maximum(m, jnp.max(s, axis=1, keepdims=True))
        alpha = jnp.exp2(m - m_new)
        p = jnp.exp2(s - m_new)
        l = alpha * l + jnp.sum(p, axis=1, keepdims=True)
        acc = alpha * acc + jnp.dot(p.astype(BF16), v_ref[pl.ds(off, ATTN_TILE), :], preferred_element_type=F32)
        return m_new, l, acc

    init = (jnp.full((ATTN_TILE, 1), -jnp.inf, F32), jnp.zeros((ATTN_TILE, 1), F32), jnp.zeros((ATTN_TILE, V_HEAD), F32))
    scores(0, s_even)

    def pair(j, carry):
        scores(2 * j + 1, s_odd)
        carry = update(2 * j, s_even, carry, False)
        scores(jnp.minimum(2 * j + 2, qi), s_even)
        return update(2 * j + 1, s_odd, carry, False)

    carry = lax.fori_loop(0, qi // 2, pair, init)

    def last_even(carry):
        return update(qi, s_even, carry, True)

    def last_odd(carry):
        scores(qi, s_odd)
        carry = update(qi - 1, s_even, carry, False)
        return update(qi, s_odd, carry, True)

    m, l, acc = lax.cond(qi % 2 == 0, last_even, last_odd, carry)
    o_ref[...] = (acc / l).astype(BF16)


def attn_prompt(q, k, v, batch, seq):
    nq = seq // ATTN_TILE
    return pl.pallas_call(
        _attn_prompt_kernel,
        grid=(batch, MLA_HEADS, nq),
        in_specs=[pl.BlockSpec((ATTN_TILE, HEAD_SLOT), lambda b, h, i: (b * nq + i, h)),
                  pl.BlockSpec((seq, HEAD_SLOT), lambda b, h, i: (b, h)),
                  pl.BlockSpec((seq, V_HEAD), lambda b, h, i: (b, h))],
        out_specs=pl.BlockSpec((ATTN_TILE, V_HEAD), lambda b, h, i: (b * nq + i, h)),
        out_shape=jax.ShapeDtypeStruct((batch * seq, MLA_HEADS * V_HEAD), BF16),
        scratch_shapes=[pltpu.VMEM((ATTN_TILE, ATTN_TILE), F32)] * 2,
        compiler_params=_cparams(("parallel", "parallel", "arbitrary")),
        name="attn_prompt",
    )(q, k, v)


def _attn_sample_kernel(q_ref, k_ref, v_ref, o_ref):
    nt = (((1,), (1,)), ((), ()))
    s = lax.dot_general(q_ref[...], k_ref[...], nt, preferred_element_type=F32)
    p = jnp.exp2(s - jnp.max(s, axis=1, keepdims=True))
    o = jnp.dot(p.astype(BF16), v_ref[...], preferred_element_type=F32)
    o_ref[...] = (o / jnp.sum(p, axis=1, keepdims=True)).astype(BF16)


def attn_sample(q, k, v, batch, q_len, kv_len, q_block0):
    return pl.pallas_call(
        _attn_sample_kernel,
        grid=(batch, MLA_HEADS),
        in_specs=[pl.BlockSpec((q_len, HEAD_SLOT), lambda b, h: (q_block0 + b, h)),
                  pl.BlockSpec((kv_len, HEAD_SLOT), lambda b, h: (b, h)),
                  pl.BlockSpec((kv_len, V_HEAD), lambda b, h: (b, h))],
        out_specs=pl.BlockSpec((q_len, V_HEAD), lambda b, h: (b, h)),
        out_shape=jax.ShapeDtypeStruct((batch * q_len, MLA_HEADS * V_HEAD), BF16),
        compiler_params=_cparams(("parallel", "parallel")),
        name="attn_sample",
    )(q, k, v)


def _proj_ln_kernel(x_ref, o_ref_in, w_ref, g_ref, b_ref, y_ref):
    mix = jnp.dot(o_ref_in[...], w_ref[...], preferred_element_type=F32)
    y_ref[...] = _ln(ALPHA * x_ref[...] + mix, g_ref[...], b_ref[...])


def proj_ln(x, o, w, g, b):
    T = x.shape[0]
    return pl.pallas_call(
        _proj_ln_kernel,
        grid=(T // ROW_TILE,),
        in_specs=[_row_spec(ROW_TILE, D_MODEL), _row_spec(ROW_TILE, o.shape[1]), _full_spec(w.shape),
                  _full_spec(g.shape), _full_spec(b.shape)],
        out_specs=_row_spec(ROW_TILE, D_MODEL),
        out_shape=jax.ShapeDtypeStruct((T, D_MODEL), F32),
        compiler_params=_cparams(("parallel",)),
        name="proj_ln",
    )(x, o, w, g, b)


def _rope_tables(pos):
    inv = 1.0 / (ROPE_BASE ** (jnp.arange(0, QK_ROPE, 2, dtype=F32) / QK_ROPE))
    ang = pos.astype(F32)[:, None] * inv[None, :]
    cos, sin = jnp.cos(ang), jnp.sin(ang)
    cos2 = jnp.concatenate([cos, cos], axis=1)
    sin2 = jnp.concatenate([-sin, sin], axis=1)
    pad = jnp.zeros_like(cos2)
    return cos2, sin2, jnp.concatenate([cos2, pad], axis=1), jnp.concatenate([sin2, pad], axis=1)


def _gm_chunk_mats(w_s, b_s, dec_seq):
    i = jnp.arange(GM_CHUNK)
    mask = (i[None, :] // CHUNK) <= (i[:, None] // CHUNK)
    full = jnp.where(mask[None], w_s, 0.0)
    rep = GM_CHUNK // dec_seq
    blk = w_s[:, :dec_seq, :dec_seq]
    same = (i[:, None] // dec_seq) == (i[None, :] // dec_seq)
    diag = jnp.where(same[None], jnp.tile(blk, (1, rep, rep)), 0.0)
    ws = jnp.stack([full, diag], axis=0).astype(BF16)
    bias = jnp.stack([b_s, jnp.tile(b_s[:, :dec_seq], (1, rep))], axis=0)
    bs = jnp.broadcast_to(bias[..., None], bias.shape + (GM_GROUP_DIM,)).astype(F32)
    return ws, bs


def _pad_q_weight(w_uq):
    w = w_uq.reshape(Q_LORA, MLA_HEADS, QK_NOPE + QK_ROPE)
    w = jnp.pad(w, ((0, 0), (0, 0), (0, HEAD_SLOT - QK_NOPE - QK_ROPE)))
    return w.reshape(Q_LORA, MLA_HEADS * HEAD_SLOT).astype(BF16)


def kernel(x_prompt, x_sample, cache_ckv, cache_kpe, ln1_g, ln1_b, ln2_g, ln2_b, gm_w_in, gm_b_in, gm_ln_g, gm_ln_b,
           gm_w_s, gm_b_s, gm_w_out, mla_w_dkv, mla_kv_norm_g, mla_w_ukv, mla_w_dq, mla_q_norm_g, mla_w_uq, mla_w_o,
           peer_w_q, peer_subkeys, peer_u, peer_v):
    B, S, _ = x_prompt.shape
    DB, DS, _ = x_sample.shape
    PAST = cache_ckv.shape[1]
    TP, TS = B * S, DB * DS
    T = TP + TS
    assert TP % ROW_TILE == 0 and TS % ROW_TILE == 0 and T % PEER_TILE == 0 and S % ATTN_TILE == 0
    for n_sc in (SC_U_SPLIT, SC_U_TOKENS - SC_U_SPLIT, SC_V_TOKENS):
        assert n_sc % (SC_SUBCORES * SC_BATCH) == 0 and n_sc % ROW_TILE == 0
    assert (T - SC_U_TOKENS) % ROW_TILE == 0 and (T - SC_V_TOKENS) % ROW_TILE == 0
    assert GM_CHUNK % DS == 0 and DS <= CHUNK and PAST % CHUNK == 0 and (PAST + DS) % 16 == 0

    x = jnp.concatenate([x_prompt.reshape(TP, D_MODEL), x_sample.reshape(TS, D_MODEL)], axis=0)
    pos = jnp.concatenate([jnp.tile(jnp.arange(S, dtype=jnp.int32), B),
                           jnp.tile(PAST + jnp.arange(DS, dtype=jnp.int32), DB)])
    cos2, sin2, cosq, sinq = _rope_tables(pos)
    spread, fold = peer_consts()
    row = lambda a: a.reshape(1, -1)

    def peer_layer(x, l):
        return peer_block(x, peer_w_q[l].astype(BF16), peer_subkeys[l].astype(BF16), pack_table(peer_u, l),
                          pack_table(peer_v, l), peer_u, l, spread, fold, row(ln2_g[l]), row(ln2_b[l]))

    gm_rows = []
    for l in range(N_A):
        u, v = gm_in(x, gm_w_in[l].astype(BF16), row(gm_b_in[l]), row(gm_ln_g[l]), row(gm_ln_b[l]))
        gm_rows.append(v[TP:].reshape(DB, DS, GM_HALF))
        ws, bs = _gm_chunk_mats(gm_w_s[l], gm_b_s[l], DS)
        x = gm_mix(x, u, v, ws, bs, gm_w_out[l].astype(BF16), row(ln1_g[l]), row(ln1_b[l]), TP // ROW_TILE)
        x = peer_layer(x, l)

    w_dkv = jnp.pad(mla_w_dkv, ((0, 0), (0, 128 - QK_ROPE))).astype(BF16)
    c_new, kpe_new = mla_kv(x, w_dkv, row(mla_kv_norm_g), cos2, sin2)
    w_ukv = mla_w_ukv.reshape(KV_LORA, MLA_HEADS, QK_NOPE + V_HEAD)
    w_uk = w_ukv[:, :, :QK_NOPE].reshape(KV_LORA, MLA_HEADS * QK_NOPE).astype(BF16)
    w_uv = w_ukv[:, :, QK_NOPE:].reshape(KV_LORA, MLA_HEADS * V_HEAD).astype(BF16)
    kp_cat, vp = mla_expand(c_new[:TP], kpe_new[:TP], w_uk, w_uv)
    KV = PAST + DS
    c_all = jnp.concatenate([cache_ckv, c_new[TP:].reshape(DB, DS, KV_LORA)], axis=1).reshape(DB * KV, KV_LORA)
    kpe_all = jnp.concatenate([cache_kpe, kpe_new[TP:].reshape(DB, DS, QK_ROPE)], axis=1).reshape(DB * KV, QK_ROPE)
    ks_cat, vs = mla_expand(c_all, kpe_all, w_uk, w_uv)

    for j in range(DEPTH - N_A):
        l = N_A + j
        q = mla_q(x, mla_w_dq[j].astype(BF16), row(mla_q_norm_g[j]), _pad_q_weight(mla_w_uq[j]), cosq, sinq)
        o = jnp.concatenate([attn_prompt(q, kp_cat, vp, B, S),
                             attn_sample(q, ks_cat, vs, DB, DS, KV, TP // DS)], axis=0)
        x = proj_ln(x, o, mla_w_o[j].astype(BF16), row(ln1_g[l]), row(ln1_b[l]))
        x = peer_layer(x, l)

    return (x[:TP].reshape(B, S, D_MODEL), x[TP:].reshape(DB, DS, D_MODEL), jnp.stack(gm_rows, axis=0),
            c_new[:TP].reshape(B, S, KV_LORA), kpe_new[:TP].reshape(B, S, QK_ROPE),
            c_new[TP:].reshape(DB, DS, KV_LORA), kpe_new[TP:].reshape(DB, DS, QK_ROPE))
```

```python
import functools
import math

import jax
import jax.numpy as jnp
from jax import lax
from jax.experimental import pallas as pl
from jax.experimental.pallas import tpu as pltpu
from jax.experimental.pallas import tpu_sc as plsc

F32 = jnp.float32
BF16 = jnp.bfloat16

D_MODEL = 1024
DEPTH = 4
N_A = DEPTH // 2
CHUNK = 64
ALPHA = (2.0 * DEPTH) ** 0.25
LN_EPS = 1e-5
RMS_EPS = 1e-6

GM_CHUNK = 128
GM_HALF = 2 * D_MODEL
GM_GROUPS = 8
GM_GROUP_DIM = GM_HALF // GM_GROUPS

MLA_HEADS = 8
QK_NOPE = 128
QK_ROPE = 64
V_HEAD = 128
KV_LORA = D_MODEL // 4
Q_LORA = 3 * D_MODEL // 8
ROPE_BASE = 10000.0
ATTN_SCALE = (QK_NOPE + QK_ROPE) ** -0.5
HEAD_SLOT = 256

PEER_HEADS = 8
PEER_NKEYS = 128
PEER_EXPERTS = PEER_NKEYS * PEER_NKEYS
PEER_DK = 256
PEER_TOPK = 16
PEER_PICKS = PEER_HEADS * PEER_TOPK
TOPK_SHIFT = PEER_TOPK.bit_length() - 1
PAIR_SENTINEL = 1 << 20
ROW_WORDS = 4
D_CHUNKS = D_MODEL // 128

ROW_TILE = 256
PEER_TILE = 128
PACK_TILE = 512
SC_U_TOKENS = 13312
SC_U_SPLIT = 6144
SC_V_TOKENS = 12288
SC_BATCH = 32
PACKED_WORDS = D_MODEL // 2
SC_SUBCORES = 32
SC_LANES = 16
SC_ROWS = 16
V_TOKENS_PER_TRIP = 8
ATTN_TILE = 512
VMEM_LIMIT = 56 * 1024 * 1024


def _cparams(sem):
    return pltpu.CompilerParams(dimension_semantics=sem, vmem_limit_bytes=VMEM_LIMIT)


def _gelu(x):
    return 0.5 * x * (1.0 + lax.erf(x * (1.0 / math.sqrt(2.0))))


def _ln(x, g, b):
    mu = jnp.mean(x, axis=-1, keepdims=True)
    xc = x - mu
    var = jnp.mean(xc * xc, axis=-1, keepdims=True)
    return xc * lax.rsqrt(var + LN_EPS) * g + b


def _rms(x, g):
    return x * lax.rsqrt(jnp.mean(x * x, axis=-1, keepdims=True) + RMS_EPS) * g


def _row_spec(tile, width):
    return pl.BlockSpec((tile, width), lambda i: (i, 0))


def _full_spec(shape):
    nd = len(shape)
    return pl.BlockSpec(shape, lambda i: (0,) * nd)


def _gm_in_kernel(x_ref, w_ref, b_ref, lg_ref, lb_ref, u_ref, v_ref):
    z = jnp.dot(x_ref[...].astype(BF16), w_ref[...], preferred_element_type=F32) + b_ref[...]
    z = _gelu(z)
    u_ref[...] = z[:, :GM_HALF]
    v_ref[...] = _ln(z[:, GM_HALF:], lg_ref[...], lb_ref[...])


def gm_in(x, w, b, lg, lb):
    T = x.shape[0]
    return pl.pallas_call(
        _gm_in_kernel,
        grid=(T // ROW_TILE,),
        in_specs=[_row_spec(ROW_TILE, D_MODEL), _full_spec(w.shape), _full_spec(b.shape),
                  _full_spec(lg.shape), _full_spec(lb.shape)],
        out_specs=[_row_spec(ROW_TILE, GM_HALF), _row_spec(ROW_TILE, GM_HALF)],
        out_shape=[jax.ShapeDtypeStruct((T, GM_HALF), F32)] * 2,
        compiler_params=_cparams(("parallel",)),
        name="gm_in",
    )(x, w, b, lg, lb)


def _gm_mix_kernel(x_ref, u_ref, v_ref, ws_ref, bs_ref, wo_ref, g_ref, b_ref, o_ref, s_scr):
    for c in range(ROW_TILE // GM_CHUNK):
        r = slice(c * GM_CHUNK, (c + 1) * GM_CHUNK)
        for g in range(GM_GROUPS):
            cs = slice(g * GM_GROUP_DIM, (g + 1) * GM_GROUP_DIM)
            sv = jnp.dot(ws_ref[0, g], v_ref[r, cs].astype(BF16), preferred_element_type=F32) + bs_ref[0, g]
            s_scr[r, cs] = (u_ref[r, cs] * sv).astype(BF16)
    mix = jnp.dot(s_scr[...], wo_ref[...], preferred_element_type=F32)
    o_ref[...] = _ln(ALPHA * x_ref[...] + mix, g_ref[...], b_ref[...])


def gm_mix(x, u, v, ws, bs, wo, g, b, n_prompt_tiles):
    T = x.shape[0]
    sel = lambda i: (jnp.where(i >= n_prompt_tiles, 1, 0), 0, 0, 0)
    return pl.pallas_call(
        _gm_mix_kernel,
        grid=(T // ROW_TILE,),
        in_specs=[_row_spec(ROW_TILE, D_MODEL), _row_spec(ROW_TILE, GM_HALF), _row_spec(ROW_TILE, GM_HALF),
                  pl.BlockSpec((1,) + ws.shape[1:], sel), pl.BlockSpec((1,) + bs.shape[1:], sel),
                  _full_spec(wo.shape), _full_spec(g.shape), _full_spec(b.shape)],
        out_specs=_row_spec(ROW_TILE, D_MODEL),
        out_shape=jax.ShapeDtypeStruct((T, D_MODEL), F32),
        scratch_shapes=[pltpu.VMEM((ROW_TILE, GM_HALF), BF16)],
        compiler_params=_cparams(("parallel",)),
        name="gm_mix",
    )(x, u, v, ws, bs, wo, g, b)


def _extract_max(s, codes, sentinel):
    m = jnp.max(s, axis=0, keepdims=True)
    c = jnp.min(jnp.where(s == m, codes, sentinel), axis=0, keepdims=True)
    return m, c, jnp.where(codes == c, -jnp.inf, s)


def _topk_rows(s):
    rows = lax.broadcasted_iota(jnp.int32, s.shape, 0).astype(F32)
    vals, idxs = [], []
    for _ in range(PEER_TOPK):
        m, idx, s = _extract_max(s, rows, float(s.shape[0]))
        vals.append(m)
        idxs.append(idx)
    return jnp.concatenate(vals, axis=0), jnp.concatenate(idxs, axis=0).astype(jnp.int32)


_PAIR_GROUPS = ((None, 0, 16), (None, 1, 8), (0, None, 16), (1, None, 8), (2, None, 8), (3, None, 8), (4, None, 8))


def pair_codes():
    codes = []
    for b_fix, a_fix, n in _PAIR_GROUPS:
        for r in range(n):
            a, b = (a_fix, r) if b_fix is None else (r, b_fix)
            first = a_fix is not None or a >= 2
            ok = first and (a + 1) * (b + 1) <= PEER_TOPK
            codes.append(a * PEER_TOPK + b if ok else PAIR_SENTINEL)
    assert sorted(c for c in codes if c != PAIR_SENTINEL) == sorted(
        a * PEER_TOPK + b for a in range(PEER_TOPK) for b in range(PEER_TOPK) if (a + 1) * (b + 1) <= PEER_TOPK)
    return jnp.broadcast_to(jnp.asarray(codes, F32)[:, None], (len(codes), PEER_TILE))


def _topk_pairs(v1, i1, v2, i2, codes):
    parts = []
    for b_fix, a_fix, n in _PAIR_GROUPS:
        if b_fix is None:
            parts.append(v1[a_fix:a_fix + 1] + v2[0:n])
        else:
            parts.append(v1[0:n] + v2[b_fix:b_fix + 1])
    cand = jnp.where(codes < PAIR_SENTINEL, jnp.concatenate(parts, axis=0), -jnp.inf)
    krow = lax.broadcasted_iota(jnp.int32, v1.shape, 0)
    vals, k1, k2 = [], [], []
    for _ in range(PEER_TOPK):
        m, c, cand = _extract_max(cand, codes, float(PAIR_SENTINEL))
        vals.append(m)
        c = c.astype(jnp.int32)
        k1.append(jnp.sum(jnp.where(krow == (c >> TOPK_SHIFT), i1, 0), axis=0, keepdims=True))
        k2.append(jnp.sum(jnp.where(krow == (c & (PEER_TOPK - 1)), i2, 0), axis=0, keepdims=True))
    return jnp.concatenate(vals, axis=0), jnp.concatenate(k1, axis=0), jnp.concatenate(k2, axis=0)


def _peer_route_kernel(x_ref, wq_ref, k_ref, codes_ref, eid_ref, gate_ref, expert_ref, *, expert0):
    q = jnp.dot(x_ref[...].astype(BF16), wq_ref[...], preferred_element_type=F32)
    half = PEER_DK // 2
    nt = (((1,), (1,)), ((), ()))
    codes = codes_ref[...]
    experts, gates = [], []
    for h in range(PEER_HEADS):
        q1 = q[:, h * PEER_DK:h * PEER_DK + half].astype(BF16)
        q2 = q[:, h * PEER_DK + half:(h + 1) * PEER_DK].astype(BF16)
        s1 = lax.dot_general(k_ref[0], q1, nt, preferred_element_type=F32)
        s2 = lax.dot_general(k_ref[1], q2, nt, preferred_element_type=F32)
        v1, i1 = _topk_rows(s1)
        v2, i2 = _topk_rows(s2)
        sc, e1, e2 = _topk_pairs(v1, i1, v2, i2, codes)
        e = jnp.exp(sc - sc[0:1])
        gates.append(e / jnp.sum(e, axis=0, keepdims=True))
        experts.append(e1 * PEER_NKEYS + e2)
    expert = jnp.transpose(jnp.concatenate(experts, axis=0))
    eid_ref[...] = expert * ROW_WORDS
    expert_ref[...] = expert + expert0
    gate_ref[...] = jnp.transpose(jnp.concatenate(gates, axis=0))


def peer_route(x, wq, keys, codes, row0, n_rows, expert0):
    n = n_rows // PEER_TILE
    tile = (PEER_TILE, PEER_PICKS)
    return pl.pallas_call(
        functools.partial(_peer_route_kernel, expert0=expert0),
        grid=(n,),
        in_specs=[pl.BlockSpec((PEER_TILE, D_MODEL), lambda i: (row0 // PEER_TILE + i, 0)), _full_spec(wq.shape),
                  _full_spec(keys.shape), _full_spec(codes.shape)],
        out_specs=[_row_spec(*tile), _row_spec(*tile), _row_spec(*tile)],
        out_shape=[jax.ShapeDtypeStruct((n * PEER_TILE, PEER_PICKS), jnp.int32),
                   jax.ShapeDtypeStruct((n * PEER_TILE, PEER_PICKS), F32),
                   jax.ShapeDtypeStruct((n * PEER_TILE, PEER_PICKS), jnp.int32)],
        compiler_params=_cparams(("parallel",)),
        name="peer_route",
    )(x, wq, keys, codes)


def _gather_rows(eid_ref, t, tab_ref, dst_ref):
    picks = eid_ref.at[t]
    for p in range(PEER_PICKS):
        row = pl.multiple_of(picks[p], ROW_WORDS)
        dst_ref[pl.ds(ROW_WORDS * p, ROW_WORDS), :] = tab_ref[pl.ds(row, ROW_WORDS), :]


def _chunk_mask():
    sub = lax.broadcasted_iota(jnp.int32, (D_CHUNKS, PEER_PICKS * D_CHUNKS), 0)
    col = lax.broadcasted_iota(jnp.int32, (D_CHUNKS, PEER_PICKS * D_CHUNKS), 1)
    return ((col & (D_CHUNKS - 1)) == sub).astype(F32)


def _fold_exact(r, fold):
    hi = r.astype(BF16)
    rem = r - hi.astype(F32)
    mid = rem.astype(BF16)
    lo = (rem - mid.astype(F32)).astype(BF16)
    return (jnp.dot(hi, fold, preferred_element_type=F32) + jnp.dot(mid, fold, preferred_element_type=F32)
            + jnp.dot(lo, fold, preferred_element_type=F32))


def _for_tokens_pipelined(eid_ref, tab_ref, bufs, compute, tokens_per_trip):
    _gather_rows(eid_ref, 0, tab_ref, bufs[0])

    def body(i, carry):
        t0 = tokens_per_trip * i
        for k in range(tokens_per_trip):
            _gather_rows(eid_ref, jnp.minimum(t0 + k + 1, PEER_TILE - 1), tab_ref, bufs[(k + 1) % 2])
            compute(t0 + k, bufs[k % 2])
        return carry

    lax.fori_loop(0, PEER_TILE // tokens_per_trip, body, 0)


def _peer_route_u_kernel(xr_ref, xc_ref, wq_ref, k_ref, codes_ref, tab_ref, fold_ref, zeros_ref,
                         eid_ref, gate_ref, h_ref, q_scr, eid_t, gate_t, eid_vmem, eid_smem, hrep, sem, *bufs):
    i = pl.program_id(0)
    half = PEER_DK // 2
    nt = (((1,), (1,)), ((), ()))
    eid_to_smem = pltpu.make_async_copy(eid_vmem, eid_smem, sem)

    @pl.when(i == 0)
    def _():
        pltpu.sync_copy(zeros_ref, eid_smem)

    @pl.when(i > 0)
    def _():
        eid_to_smem.wait()

    q = jnp.dot(xr_ref[...].astype(BF16), wq_ref[...], preferred_element_type=F32)
    for h in range(PEER_HEADS):
        q_scr[h] = q[:, h * PEER_DK:(h + 1) * PEER_DK]
    codes = codes_ref[...]
    mask = _chunk_mask()

    def compute(t, rows_ref):
        u = pltpu.bitcast(rows_ref[...], BF16)
        xr = xc_ref[pl.ds(t, 1), :]
        x8 = jnp.concatenate([xr[:, j * 128:(j + 1) * 128] for j in range(D_CHUNKS)], axis=0)
        hi = x8.astype(BF16)
        lo = (x8 - hi.astype(F32)).astype(BF16)
        o = lax.dot_general(jnp.concatenate([hi, lo], axis=0), u, nt, preferred_element_type=F32)
        o8 = (o[0:D_CHUNKS] + o[D_CHUNKS:2 * D_CHUNKS]) * mask
        hrep[pl.ds(t, 1), :] = jnp.sum(o8, axis=0, keepdims=True)

    per_trip = PEER_TILE // PEER_HEADS
    _gather_rows(eid_smem, 0, tab_ref, bufs[0])

    def body(h, carry):
        qh = q_scr[h]
        s1 = lax.dot_general(k_ref[0], qh[:, :half].astype(BF16), nt, preferred_element_type=F32)
        s2 = lax.dot_general(k_ref[1], qh[:, half:].astype(BF16), nt, preferred_element_type=F32)
        v1, i1 = _topk_rows(s1)
        v2, i2 = _topk_rows(s2)
        sc, e1, e2 = _topk_pairs(v1, i1, v2, i2, codes)
        e = jnp.exp(sc - sc[0:1])
        gate_t[h] = e / jnp.sum(e, axis=0, keepdims=True)
        eid_t[h] = (e1 * PEER_NKEYS + e2) * ROW_WORDS
        t0 = per_trip * h
        for k in range(per_trip):
            _gather_rows(eid_smem, jnp.minimum(t0 + k + 1, PEER_TILE - 1), tab_ref, bufs[(k + 1) % 2])
            compute(t0 + k, bufs[k % 2])
        return carry

    lax.fori_loop(0, PEER_HEADS, body, 0)
    h_ref[...] = _fold_exact(hrep[...], fold_ref[...])
    eid = jnp.transpose(eid_t[...].reshape(PEER_PICKS, PEER_TILE))
    eid_ref[...] = eid
    eid_vmem[...] = eid
    gate_ref[...] = jnp.transpose(gate_t[...].reshape(PEER_PICKS, PEER_TILE))
    eid_to_smem.start()

    @pl.when(i == pl.num_programs(0) - 1)
    def _():
        eid_to_smem.wait()


def _peer_v_kernel(eid_ref, h_ref, gate_ref, x_ref, tab_ref, spread_ref, g_ref, b_ref, y_ref, a_hi, a_lo, mix, *bufs):
    a = gate_ref[...] * _gelu(h_ref[...])
    hi = a.astype(BF16)
    lo = (a - hi.astype(F32)).astype(BF16)
    a_hi[...] = jnp.dot(hi, spread_ref[...], preferred_element_type=F32)
    a_lo[...] = jnp.dot(lo, spread_ref[...], preferred_element_type=F32)
    mask = _chunk_mask()

    def compute(t, rows_ref):
        v = pltpu.bitcast(rows_ref[...], BF16)
        lhs = jnp.concatenate([(a_hi[pl.ds(t, 1), :] * mask).astype(BF16),
                               (a_lo[pl.ds(t, 1), :] * mask).astype(BF16)], axis=0)
        o = jnp.dot(lhs, v, preferred_element_type=F32)
        o8 = o[0:D_CHUNKS] + o[D_CHUNKS:2 * D_CHUNKS]
        mix[pl.ds(t, 1), :] = jnp.concatenate([o8[j:j + 1] for j in range(D_CHUNKS)], axis=1)

    _for_tokens_pipelined(eid_ref, tab_ref, bufs, compute, V_TOKENS_PER_TRIP)
    y_ref[...] = _ln(ALPHA * x_ref[...] + mix[...], g_ref[...], b_ref[...])


def _peer_specs():
    eid_spec = pl.BlockSpec((PEER_TILE, PEER_PICKS), lambda i: (i, 0), memory_space=pltpu.SMEM)
    tab_spec = pl.BlockSpec((PEER_EXPERTS * ROW_WORDS, 128), lambda i: (0, 0), pipeline_mode=pl.Buffered(1))
    bufs = [pltpu.VMEM((PEER_PICKS * ROW_WORDS, 128), jnp.int32)] * 2
    return eid_spec, tab_spec, bufs


def peer_route_u(x, wq, keys, codes, tab, fold, n_tokens):
    T = n_tokens
    n = T // PEER_TILE
    _, tab_spec, bufs = _peer_specs()
    cur = lambda i: (jnp.minimum(i, n - 1), 0)
    prev = lambda i: (jnp.maximum(i - 1, 0), 0)
    zeros = jnp.zeros((PEER_TILE, PEER_PICKS), jnp.int32)
    tile = (PEER_TILE, PEER_PICKS)
    return pl.pallas_call(
        _peer_route_u_kernel,
        grid=(n + 1,),
        in_specs=[pl.BlockSpec((PEER_TILE, D_MODEL), cur), pl.BlockSpec((PEER_TILE, D_MODEL), prev),
                  _full_spec(wq.shape), _full_spec(keys.shape), _full_spec(codes.shape), tab_spec,
                  _full_spec(fold.shape), _full_spec(zeros.shape)],
        out_specs=[pl.BlockSpec(tile, cur), pl.BlockSpec(tile, cur), pl.BlockSpec(tile, prev)],
        out_shape=[jax.ShapeDtypeStruct((T, PEER_PICKS), jnp.int32), jax.ShapeDtypeStruct((T, PEER_PICKS), F32),
                   jax.ShapeDtypeStruct((T, PEER_PICKS), F32)],
        scratch_shapes=[pltpu.VMEM((PEER_HEADS, PEER_TILE, PEER_DK), F32),
                        pltpu.VMEM((PEER_HEADS, PEER_TOPK, PEER_TILE), jnp.int32),
                        pltpu.VMEM((PEER_HEADS, PEER_TOPK, PEER_TILE), F32),
                        pltpu.VMEM(tile, jnp.int32), pltpu.SMEM(tile, jnp.int32),
                        pltpu.VMEM((PEER_TILE, PEER_PICKS * D_CHUNKS), F32),
                        pltpu.SemaphoreType.DMA(())] + bufs,
        compiler_params=_cparams(("arbitrary",)),
        name="peer_route_u",
    )(x, x, wq, keys, codes, tab, fold, zeros)


def peer_v(eid, hdn, gate, x, tab, spread, g, b, n_tokens):
    T = eid.shape[0]
    eid_spec, tab_spec, bufs = _peer_specs()
    return pl.pallas_call(
        _peer_v_kernel,
        grid=(n_tokens // PEER_TILE,),
        in_specs=[eid_spec, _row_spec(PEER_TILE, PEER_PICKS), _row_spec(PEER_TILE, PEER_PICKS),
                  _row_spec(PEER_TILE, D_MODEL), tab_spec, _full_spec(spread.shape), _full_spec(g.shape), _full_spec(b.shape)],
        out_specs=_row_spec(PEER_TILE, D_MODEL),
        out_shape=jax.ShapeDtypeStruct((T, D_MODEL), F32),
        scratch_shapes=[pltpu.VMEM((PEER_TILE, PEER_PICKS * D_CHUNKS), F32)] * 2 + [pltpu.VMEM((PEER_TILE, D_MODEL), F32)] + bufs,
        compiler_params=_cparams(("arbitrary",)),
        name="peer_v",
    )(eid, hdn, gate, x, tab, spread, g, b)


def _sc_gate_kernel(eid_ref, h_ref, gate_ref, a_ref, e_ref, *, expert0):
    a_ref[...] = gate_ref[...] * _gelu(h_ref[...])
    e_ref[...] = eid_ref[...] // ROW_WORDS + expert0


def sc_gate(eid, hdn, gate, row0, expert0):
    n = (eid.shape[0] - row0) // PEER_TILE
    tile = (PEER_TILE, PEER_PICKS)
    spec = pl.BlockSpec(tile, lambda i: (row0 // PEER_TILE + i, 0))
    return pl.pallas_call(
        functools.partial(_sc_gate_kernel, expert0=expert0),
        grid=(n,),
        in_specs=[spec, spec, spec],
        out_specs=[_row_spec(*tile), _row_spec(*tile)],
        out_shape=[jax.ShapeDtypeStruct((n * PEER_TILE, PEER_PICKS), F32),
                   jax.ShapeDtypeStruct((n * PEER_TILE, PEER_PICKS), jnp.int32)],
        compiler_params=_cparams(("parallel",)),
        name="sc_gate",
    )(eid, hdn, gate)


def _sc_pipeline(idx_v, tab_hbm, bufs, consume):
    n_chunks = PEER_PICKS // SC_ROWS

    def gather(g, slot):
        rows, sem = bufs[slot]
        idx = idx_v.at[g // n_chunks, pl.ds((g % n_chunks) * SC_ROWS, SC_ROWS)]
        return pltpu.make_async_copy(tab_hbm.at[idx], rows, sem)

    total = SC_BATCH * n_chunks
    gather(0, 0).start()

    @pl.loop(0, total // 2)
    def _(j):
        g = 2 * j
        gather(g + 1, 1).start()
        gather(g, 0).wait()
        consume(g, bufs[0][0])

        @pl.when(g + 2 < total)
        def _():
            gather(g + 2, 0).start()

        gather(g + 1, 1).wait()
        consume(g + 1, bufs[1][0])


def _sc_call(body, out_type, stage_shape, stage_dtype, result_shape, row_width, row_dtype):
    mesh = plsc.VectorSubcoreMesh(core_axis_name="c", subcore_axis_name="s")
    return pl.kernel(
        body, out_type=out_type, mesh=mesh,
        scratch_types=[pltpu.VMEM((SC_BATCH, PEER_PICKS), jnp.int32), pltpu.VMEM(stage_shape, stage_dtype),
                       pltpu.VMEM((SC_ROWS, row_width), row_dtype), pltpu.VMEM((SC_ROWS, row_width), row_dtype),
                       pltpu.VMEM(result_shape, F32), pltpu.SemaphoreType.DMA, pltpu.SemaphoreType.DMA],
        compiler_params=pltpu.CompilerParams(needs_layout_passes=False))


def _sc_first_token(n_rows):
    return (lax.axis_index("c") * (SC_SUBCORES // 2) + lax.axis_index("s")) * (n_rows // SC_SUBCORES)


def _packed_word_groups(half_idx):
    return [(s * 128 + l0, 256 * s + l0, 256 * s + 128 + l0)
            for s in (2 * half_idx, 2 * half_idx + 1) for l0 in range(0, 128, SC_LANES)]


def sc_value_mix(experts, a, tab):
    ns = experts.shape[0]
    n_chunks = PEER_PICKS // SC_ROWS

    def body(e_hbm, a_hbm, tab_hbm, o_hbm, idx_v, a_v, rows0, rows1, out_v, sem0, sem1):
        t0 = _sc_first_token(ns)

        @pl.loop(0, ns // SC_SUBCORES // SC_BATCH)
        def _(bi):
            tb = t0 + bi * SC_BATCH
            pltpu.sync_copy(e_hbm.at[pl.ds(tb, SC_BATCH)], idx_v)
            pltpu.sync_copy(a_hbm.at[pl.ds(tb * PEER_PICKS, SC_BATCH * PEER_PICKS)], a_v)

            def consume(g, rows):
                k = g // n_chunks
                c = g % n_chunks
                for hf in range(2):
                    groups = _packed_word_groups(hf)
                    accs = tuple(jnp.where(c == 0, 0.0, out_v[k, pl.ds(d, SC_LANES)])
                                 for _, d_lo, d_hi in groups for d in (d_lo, d_hi))

                    def pick(p, accs, groups=groups):
                        lane_idx = jnp.zeros((SC_LANES,), jnp.int32) + (k * PEER_PICKS + c * SC_ROWS + p)
                        ap = plsc.load_gather(a_v, [lane_idx])
                        new = []
                        for i, (w, _, _) in enumerate(groups):
                            words = rows[p, pl.ds(w, SC_LANES)]
                            new.append(accs[2 * i] + ap * lax.bitcast_convert_type(words << 16, F32))
                            new.append(accs[2 * i + 1] + ap * lax.bitcast_convert_type(words & -65536, F32))
                        return tuple(new)

                    accs = lax.fori_loop(0, SC_ROWS, pick, accs)
                    for i, (_, d_lo, d_hi) in enumerate(groups):
                        out_v[k, pl.ds(d_lo, SC_LANES)] = accs[2 * i]
                        out_v[k, pl.ds(d_hi, SC_LANES)] = accs[2 * i + 1]

            _sc_pipeline(idx_v, tab_hbm, ((rows0, sem0), (rows1, sem1)), consume)
            pltpu.sync_copy(out_v, o_hbm.at[pl.ds(tb, SC_BATCH)])

    call = _sc_call(body, jax.ShapeDtypeStruct((ns, D_MODEL), F32), (SC_BATCH * PEER_PICKS,), F32,
                    (SC_BATCH, D_MODEL), PACKED_WORDS, jnp.int32)
    return call(experts, a.reshape(-1), tab)


def sc_expert_dots(experts, x, row0, tabs):
    ns = experts.shape[0]
    n_chunks = PEER_PICKS // SC_ROWS
    half = D_MODEL // 2
    n_partial = 4

    def body(e_hbm, x_hbm, tab_hbm, o_hbm, idx_v, x_v, rows0, rows1, out_v, sem0, sem1):
        t0 = _sc_first_token(ns)
        lanes = lax.iota(jnp.int32, SC_LANES)

        @pl.loop(0, ns // SC_SUBCORES // SC_BATCH)
        def _(bi):
            tb = t0 + bi * SC_BATCH
            pltpu.sync_copy(e_hbm.at[pl.ds(tb, SC_BATCH)], idx_v)
            pltpu.sync_copy(x_hbm.at[pl.ds(row0 + tb, SC_BATCH)], x_v)

            def consume(g, rows):
                k = g // n_chunks
                c = g % n_chunks
                dots = jnp.zeros((SC_LANES,), F32)
                for base in (0, half):
                    xs = tuple(x_v[k, pl.ds(base + d * SC_LANES, SC_LANES)] for d in range(half // SC_LANES))

                    def pick(p, dots, base=base, xs=xs):
                        parts = [jnp.zeros((SC_LANES,), F32)] * n_partial
                        for d, xd in enumerate(xs):
                            parts[d % n_partial] = parts[d % n_partial] + xd * rows[p, pl.ds(base + d * SC_LANES, SC_LANES)]
                        s = jnp.sum((parts[0] + parts[1]) + (parts[2] + parts[3]))
                        return jnp.where(lanes == p, dots + s, dots)

                    dots = lax.fori_loop(0, SC_ROWS, pick, dots)
                out_v[pl.ds(k * PEER_PICKS + c * SC_ROWS, SC_ROWS)] = dots

            _sc_pipeline(idx_v, tab_hbm, ((rows0, sem0), (rows1, sem1)), consume)
            pltpu.sync_copy(out_v, o_hbm.at[pl.ds(tb * PEER_PICKS, SC_BATCH * PEER_PICKS)])

    call = _sc_call(body, jax.ShapeDtypeStruct((ns * PEER_PICKS,), F32), (SC_BATCH, D_MODEL), F32,
                    (SC_BATCH * PEER_PICKS,), D_MODEL, F32)
    return call(experts, x, tabs).reshape(ns, PEER_PICKS)


def _resid_ln_rows_kernel(x_ref, r_ref, g_ref, b_ref, y_in_ref, y_ref):
    del y_in_ref
    y_ref[...] = _ln(ALPHA * x_ref[...] + r_ref[...], g_ref[...], b_ref[...])


def resid_ln_rows(x, r, g, b, y, row0):
    n = r.shape[0] // ROW_TILE
    at = lambda i: (row0 // ROW_TILE + i, 0)
    return pl.pallas_call(
        _resid_ln_rows_kernel,
        grid=(n,),
        in_specs=[pl.BlockSpec((ROW_TILE, D_MODEL), at), _row_spec(ROW_TILE, D_MODEL), _full_spec(g.shape),
                  _full_spec(b.shape), pl.BlockSpec(memory_space=pl.ANY)],
        out_specs=pl.BlockSpec((ROW_TILE, D_MODEL), at),
        out_shape=jax.ShapeDtypeStruct(y.shape, y.dtype),
        input_output_aliases={4: 0},
        compiler_params=_cparams(("parallel",)),
        name="resid_ln_rows",
    )(x, r, g, b, y)


def _pack_kernel(t_ref, o_ref):
    n = t_ref.shape[1]
    for s in range(ROW_WORDS):
        lo = t_ref[0, :, (2 * s) * 128:(2 * s + 1) * 128].astype(BF16).astype(F32)
        hi = t_ref[0, :, (2 * s + 1) * 128:(2 * s + 2) * 128].astype(BF16).astype(F32)
        word = pltpu.bitcast(hi, jnp.uint32) | (pltpu.bitcast(lo, jnp.uint32) >> 16)
        o_ref[pl.ds(s, n, stride=ROW_WORDS), :] = pltpu.bitcast(word, jnp.int32)


def pack_table(tabs, layer):
    e = tabs.shape[1]
    return pl.pallas_call(
        _pack_kernel,
        grid=(e // PACK_TILE,),
        in_specs=[pl.BlockSpec((1, PACK_TILE, D_MODEL), lambda i: (layer, i, 0))],
        out_specs=_row_spec(PACK_TILE * ROW_WORDS, 128),
        out_shape=jax.ShapeDtypeStruct((e * ROW_WORDS, 128), jnp.int32),
        compiler_params=_cparams(("parallel",)),
        name="pack_table",
    )(tabs)


def peer_consts():
    col = jnp.arange(PEER_PICKS * D_CHUNKS) // D_CHUNKS
    spread = (col[None, :] == jnp.arange(PEER_PICKS)[:, None])
    return spread.astype(BF16), jnp.transpose(spread).astype(BF16)


def peer_block(x, wq, keys, u_packed, v_packed, u_tabs, layer, spread, fold, g, b):
    T = x.shape[0]
    codes = pair_codes()
    expert0 = layer * PEER_EXPERTS
    parts = []
    for row0, n_rows in ((T - SC_U_TOKENS, SC_U_SPLIT), (T - SC_U_TOKENS + SC_U_SPLIT, SC_U_TOKENS - SC_U_SPLIT)):
        eid_s, gate_s, expert_s = peer_route(x, wq, keys, codes, row0, n_rows, expert0)
        parts.append((eid_s, gate_s, sc_expert_dots(expert_s, x, row0, u_tabs.reshape(-1, D_MODEL))))
    parts.insert(0, peer_route_u(x, wq, keys, codes, u_packed, fold, T - SC_U_TOKENS))
    eid, gate, hdn = (jnp.concatenate(p, axis=0) for p in zip(*parts))
    a_sc, e_sc = sc_gate(eid, hdn, gate, T - SC_V_TOKENS, 0)
    mix_sc = sc_value_mix(e_sc, a_sc, v_packed.reshape(PEER_EXPERTS, PACKED_WORDS))
    y = peer_v(eid, hdn, gate, x, v_packed, spread, g, b, T - SC_V_TOKENS)
    return resid_ln_rows(x, mix_sc, g, b, y, T - SC_V_TOKENS)


def _mla_kv_kernel(x_ref, w_ref, g_ref, cos_ref, sin_ref, c_ref, kpe_ref):
    kv = jnp.dot(x_ref[...].astype(BF16), w_ref[...], preferred_element_type=F32)
    c_ref[...] = _rms(kv[:, :KV_LORA], g_ref[...])
    kp = kv[:, KV_LORA:KV_LORA + QK_ROPE]
    sw = jnp.concatenate([kp[:, QK_ROPE // 2:], kp[:, :QK_ROPE // 2]], axis=1)
    kpe_ref[...] = kp * cos_ref[...] + sw * sin_ref[...]


def mla_kv(x, w, g, cos2, sin2):
    T = x.shape[0]
    return pl.pallas_call(
        _mla_kv_kernel,
        grid=(T // ROW_TILE,),
        in_specs=[_row_spec(ROW_TILE, D_MODEL), _full_spec(w.shape), _full_spec(g.shape),
                  _row_spec(ROW_TILE, QK_ROPE), _row_spec(ROW_TILE, QK_ROPE)],
        out_specs=[_row_spec(ROW_TILE, KV_LORA), _row_spec(ROW_TILE, QK_ROPE)],
        out_shape=[jax.ShapeDtypeStruct((T, KV_LORA), F32), jax.ShapeDtypeStruct((T, QK_ROPE), F32)],
        compiler_params=_cparams(("parallel",)),
        name="mla_kv",
    )(x, w, g, cos2, sin2)


def _mla_expand_kernel(c_ref, kpe_ref, wk_ref, wv_ref, k_ref, v_ref):
    cb = c_ref[...].astype(BF16)
    kn = jnp.dot(cb, wk_ref[...], preferred_element_type=F32)
    v_ref[...] = jnp.dot(cb, wv_ref[...], preferred_element_type=F32).astype(BF16)
    kp = kpe_ref[...]
    kpad = jnp.concatenate([kp, jnp.zeros_like(kp)], axis=1).astype(BF16)
    for h in range(MLA_HEADS):
        k_ref[:, h * HEAD_SLOT:h * HEAD_SLOT + QK_NOPE] = kn[:, h * QK_NOPE:(h + 1) * QK_NOPE].astype(BF16)
        k_ref[:, h * HEAD_SLOT + QK_NOPE:(h + 1) * HEAD_SLOT] = kpad


def mla_expand(c, kpe, wk, wv):
    R = c.shape[0]
    return pl.pallas_call(
        _mla_expand_kernel,
        grid=(R // ROW_TILE,),
        in_specs=[_row_spec(ROW_TILE, KV_LORA), _row_spec(ROW_TILE, QK_ROPE), _full_spec(wk.shape), _full_spec(wv.shape)],
        out_specs=[_row_spec(ROW_TILE, MLA_HEADS * HEAD_SLOT), _row_spec(ROW_TILE, MLA_HEADS * V_HEAD)],
        out_shape=[jax.ShapeDtypeStruct((R, MLA_HEADS * HEAD_SLOT), BF16), jax.ShapeDtypeStruct((R, MLA_HEADS * V_HEAD), BF16)],
        compiler_params=_cparams(("parallel",)),
        name="mla_expand",
    )(c, kpe, wk, wv)


def _mla_q_kernel(x_ref, wdq_ref, g_ref, wuq_ref, cos_ref, sin_ref, q_ref):
    cq = _rms(jnp.dot(x_ref[...].astype(BF16), wdq_ref[...], preferred_element_type=F32), g_ref[...])
    q = jnp.dot(cq.astype(BF16), wuq_ref[...], preferred_element_type=F32)
    q = q * (ATTN_SCALE * math.log2(math.e))
    lane = lax.broadcasted_iota(jnp.int32, (ROW_TILE, 128), 1)
    cos = cos_ref[...]
    sin = sin_ref[...]
    for h in range(MLA_HEADS):
        q_ref[:, h * HEAD_SLOT:h * HEAD_SLOT + QK_NOPE] = q[:, h * HEAD_SLOT:h * HEAD_SLOT + QK_NOPE].astype(BF16)
        seg = q[:, h * HEAD_SLOT + QK_NOPE:(h + 1) * HEAD_SLOT]
        sw = jnp.where(lane < QK_ROPE // 2, pltpu.roll(seg, 128 - QK_ROPE // 2, 1),
                       jnp.where(lane < QK_ROPE, pltpu.roll(seg, QK_ROPE // 2, 1), 0.0))
        q_ref[:, h * HEAD_SLOT + QK_NOPE:(h + 1) * HEAD_SLOT] = (seg * cos + sw * sin).astype(BF16)


def mla_q(x, wdq, g, wuq, cosq, sinq):
    T = x.shape[0]
    return pl.pallas_call(
        _mla_q_kernel,
        grid=(T // ROW_TILE,),
        in_specs=[_row_spec(ROW_TILE, D_MODEL), _full_spec(wdq.shape), _full_spec(g.shape), _full_spec(wuq.shape),
                  _row_spec(ROW_TILE, 128), _row_spec(ROW_TILE, 128)],
        out_specs=_row_spec(ROW_TILE, MLA_HEADS * HEAD_SLOT),
        out_shape=jax.ShapeDtypeStruct((T, MLA_HEADS * HEAD_SLOT), BF16),
        compiler_params=_cparams(("parallel",)),
        name="mla_q",
    )(x, wdq, g, wuq, cosq, sinq)


def _attn_prompt_kernel(q_ref, k_ref, v_ref, o_ref, s_even, s_odd):
    qi = pl.program_id(2)
    q = q_ref[...]
    nt = (((1,), (1,)), ((), ()))

    def scores(ki, dst):
        off = pl.multiple_of(ki * ATTN_TILE, ATTN_TILE)
        dst[...] = lax.dot_general(q, k_ref[pl.ds(off, ATTN_TILE), :], nt, preferred_element_type=F32)

    def update(ki, src, carry, diagonal):
        m, l, acc = carry
        off = pl.multiple_of(ki * ATTN_TILE, ATTN_TILE)
        s = src[...]
        if diagonal:
            qc = lax.broadcasted_iota(jnp.int32, s.shape, 0) // CHUNK
            kc = lax.broadcasted_iota(jnp.int32, s.shape, 1) // CHUNK
            s = jnp.where(kc <= qc, s, -jnp.inf)
        m_new = jnp.maximum(m, jnp.max(s, axis=1, keepdims=True))
        alpha = jnp.exp2(m - m_new)
        p = jnp.exp2(s - m_new)
        l = alpha * l + jnp.sum(p, axis=1, keepdims=True)
        acc = alpha * acc + jnp.dot(p.astype(BF16), v_ref[pl.ds(off, ATTN_TILE), :], preferred_element_type=F32)
        return m_new, l, acc

    init = (jnp.full((ATTN_TILE, 1), -jnp.inf, F32), jnp.zeros((ATTN_TILE, 1), F32), jnp.zeros((ATTN_TILE, V_HEAD), F32))
    scores(0, s_even)

    def pair(j, carry):
        scores(2 * j + 1, s_odd)
        carry = update(2 * j, s_even, carry, False)
        scores(jnp.minimum(2 * j + 2, qi), s_even)
        return update(2 * j + 1, s_odd, carry, False)

    carry = lax.fori_loop(0, qi // 2, pair, init)

    def last_even(carry):
        return update(qi, s_even, carry, True)

    def last_odd(carry):
        scores(qi, s_odd)
        carry = update(qi - 1, s_even, carry, False)
        return update(qi, s_odd, carry, True)

    m, l, acc = lax.cond(qi % 2 == 0, last_even, last_odd, carry)
    o_ref[...] = (acc / l).astype(BF16)


def attn_prompt(q, k, v, batch, seq):
    nq = seq // ATTN_TILE
    return pl.pallas_call(
        _attn_prompt_kernel,
        grid=(batch, MLA_HEADS, nq),
        in_specs=[pl.BlockSpec((ATTN_TILE, HEAD_SLOT), lambda b, h, i: (b * nq + i, h)),
                  pl.BlockSpec((seq, HEAD_SLOT), lambda b, h, i: (b, h)),
                  pl.BlockSpec((seq, V_HEAD), lambda b, h, i: (b, h))],
        out_specs=pl.BlockSpec((ATTN_TILE, V_HEAD), lambda b, h, i: (b * nq + i, h)),
        out_shape=jax.ShapeDtypeStruct((batch * seq, MLA_HEADS * V_HEAD), BF16),
        scratch_shapes=[pltpu.VMEM((ATTN_TILE, ATTN_TILE), F32)] * 2,
        compiler_params=_cparams(("parallel", "parallel", "arbitrary")),
        name="attn_prompt",
    )(q, k, v)


def _attn_sample_kernel(q_ref, k_ref, v_ref, o_ref):
    nt = (((1,), (1,)), ((), ()))
    s = lax.dot_general(q_ref[...], k_ref[...], nt, preferred_element_type=F32)
    p = jnp.exp2(s - jnp.max(s, axis=1, keepdims=True))
    o = jnp.dot(p.astype(BF16), v_ref[...], preferred_element_type=F32)
    o_ref[...] = (o / jnp.sum(p, axis=1, keepdims=True)).astype(BF16)


def attn_sample(q, k, v, batch, q_len, kv_len, q_block0):
    return pl.pallas_call(
        _attn_sample_kernel,
        grid=(batch, MLA_HEADS),
        in_specs=[pl.BlockSpec((q_len, HEAD_SLOT), lambda b, h: (q_block0 + b, h)),
                  pl.BlockSpec((kv_len, HEAD_SLOT), lambda b, h: (b, h)),
                  pl.BlockSpec((kv_len, V_HEAD), lambda b, h: (b, h))],
        out_specs=pl.BlockSpec((q_len, V_HEAD), lambda b, h: (b, h)),
        out_shape=jax.ShapeDtypeStruct((batch * q_len, MLA_HEADS * V_HEAD), BF16),
        compiler_params=_cparams(("parallel", "parallel")),
        name="attn_sample",
    )(q, k, v)


def _proj_ln_kernel(x_ref, o_ref_in, w_ref, g_ref, b_ref, y_ref):
    mix = jnp.dot(o_ref_in[...], w_ref[...], preferred_element_type=F32)
    y_ref[...] = _ln(ALPHA * x_ref[...] + mix, g_ref[...], b_ref[...])


def proj_ln(x, o, w, g, b):
    T = x.shape[0]
    return pl.pallas_call(
        _proj_ln_kernel,
        grid=(T // ROW_TILE,),
        in_specs=[_row_spec(ROW_TILE, D_MODEL), _row_spec(ROW_TILE, o.shape[1]), _full_spec(w.shape),
                  _full_spec(g.shape), _full_spec(b.shape)],
        out_specs=_row_spec(ROW_TILE, D_MODEL),
        out_shape=jax.ShapeDtypeStruct((T, D_MODEL), F32),
        compiler_params=_cparams(("parallel",)),
        name="proj_ln",
    )(x, o, w, g, b)


def _rope_tables(pos):
    inv = 1.0 / (ROPE_BASE ** (jnp.arange(0, QK_ROPE, 2, dtype=F32) / QK_ROPE))
    ang = pos.astype(F32)[:, None] * inv[None, :]
    cos, sin = jnp.cos(ang), jnp.sin(ang)
    cos2 = jnp.concatenate([cos, cos], axis=1)
    sin2 = jnp.concatenate([-sin, sin], axis=1)
    pad = jnp.zeros_like(cos2)
    return cos2, sin2, jnp.concatenate([cos2, pad], axis=1), jnp.concatenate([sin2, pad], axis=1)


def _gm_chunk_mats(w_s, b_s, dec_seq):
    i = jnp.arange(GM_CHUNK)
    mask = (i[None, :] // CHUNK) <= (i[:, None] // CHUNK)
    full = jnp.where(mask[None], w_s, 0.0)
    rep = GM_CHUNK // dec_seq
    blk = w_s[:, :dec_seq, :dec_seq]
    same = (i[:, None] // dec_seq) == (i[None, :] // dec_seq)
    diag = jnp.where(same[None], jnp.tile(blk, (1, rep, rep)), 0.0)
    ws = jnp.stack([full, diag], axis=0).astype(BF16)
    bias = jnp.stack([b_s, jnp.tile(b_s[:, :dec_seq], (1, rep))], axis=0)
    bs = jnp.broadcast_to(bias[..., None], bias.shape + (GM_GROUP_DIM,)).astype(F32)
    return ws, bs


def _pad_q_weight(w_uq):
    w = w_uq.reshape(Q_LORA, MLA_HEADS, QK_NOPE + QK_ROPE)
    w = jnp.pad(w, ((0, 0), (0, 0), (0, HEAD_SLOT - QK_NOPE - QK_ROPE)))
    return w.reshape(Q_LORA, MLA_HEADS * HEAD_SLOT).astype(BF16)


def kernel(x_prompt, x_sample, cache_ckv, cache_kpe, ln1_g, ln1_b, ln2_g, ln2_b, gm_w_in, gm_b_in, gm_ln_g, gm_ln_b,
           gm_w_s, gm_b_s, gm_w_out, mla_w_dkv, mla_kv_norm_g, mla_w_ukv, mla_w_dq, mla_q_norm_g, mla_w_uq, mla_w_o,
           peer_w_q, peer_subkeys, peer_u, peer_v):
    B, S, _ = x_prompt.shape
    DB, DS, _ = x_sample.shape
    PAST = cache_ckv.shape[1]
    TP, TS = B * S, DB * DS
    T = TP + TS
    assert TP % ROW_TILE == 0 and TS % ROW_TILE == 0 and T % PEER_TILE == 0 and S % ATTN_TILE == 0
    for n_sc in (SC_U_SPLIT, SC_U_TOKENS - SC_U_SPLIT, SC_V_TOKENS):
        assert n_sc % (SC_SUBCORES * SC_BATCH) == 0 and n_sc % ROW_TILE == 0
    assert (T - SC_U_TOKENS) % ROW_TILE == 0 and (T - SC_V_TOKENS) % ROW_TILE == 0
    assert GM_CHUNK % DS == 0 and DS <= CHUNK and PAST % CHUNK == 0 and (PAST + DS) % 16 == 0

    x = jnp.concatenate([x_prompt.reshape(TP, D_MODEL), x_sample.reshape(TS, D_MODEL)], axis=0)
    pos = jnp.concatenate([jnp.tile(jnp.arange(S, dtype=jnp.int32), B),
                           jnp.tile(PAST + jnp.arange(DS, dtype=jnp.int32), DB)])
    cos2, sin2, cosq, sinq = _rope_tables(pos)
    spread, fold = peer_consts()
    row = lambda a: a.reshape(1, -1)

    def peer_layer(x, l):
        return peer_block(x, peer_w_q[l].astype(BF16), peer_subkeys[l].astype(BF16), pack_table(peer_u, l),
                          pack_table(peer_v, l), peer_u, l, spread, fold, row(ln2_g[l]), row(ln2_b[l]))

    gm_rows = []
    for l in range(N_A):
        u, v = gm_in(x, gm_w_in[l].astype(BF16), row(gm_b_in[l]), row(gm_ln_g[l]), row(gm_ln_b[l]))
        gm_rows.append(v[TP:].reshape(DB, DS, GM_HALF))
        ws, bs = _gm_chunk_mats(gm_w_s[l], gm_b_s[l], DS)
        x = gm_mix(x, u, v, ws, bs, gm_w_out[l].astype(BF16), row(ln1_g[l]), row(ln1_b[l]), TP // ROW_TILE)
        x = peer_layer(x, l)

    w_dkv = jnp.pad(mla_w_dkv, ((0, 0), (0, 128 - QK_ROPE))).astype(BF16)
    c_new, kpe_new = mla_kv(x, w_dkv, row(mla_kv_norm_g), cos2, sin2)
    w_ukv = mla_w_ukv.reshape(KV_LORA, MLA_HEADS, QK_NOPE + V_HEAD)
    w_uk = w_ukv[:, :, :QK_NOPE].reshape(KV_LORA, MLA_HEADS * QK_NOPE).astype(BF16)
    w_uv = w_ukv[:, :, QK_NOPE:].reshape(KV_LORA, MLA_HEADS * V_HEAD).astype(BF16)
    kp_cat, vp = mla_expand(c_new[:TP], kpe_new[:TP], w_uk, w_uv)
    KV = PAST + DS
    c_all = jnp.concatenate([cache_ckv, c_new[TP:].reshape(DB, DS, KV_LORA)], axis=1).reshape(DB * KV, KV_LORA)
    kpe_all = jnp.concatenate([cache_kpe, kpe_new[TP:].reshape(DB, DS, QK_ROPE)], axis=1).reshape(DB * KV, QK_ROPE)
    ks_cat, vs = mla_expand(c_all, kpe_all, w_uk, w_uv)

    for j in range(DEPTH - N_A):
        l = N_A + j
        q = mla_q(x, mla_w_dq[j].astype(BF16), row(mla_q_norm_g[j]), _pad_q_weight(mla_w_uq[j]), cosq, sinq)
        o = jnp.concatenate([attn_prompt(q, kp_cat, vp, B, S),
                             attn_sample(q, ks_cat, vs, DB, DS, KV, TP // DS)], axis=0)
        x = proj_ln(x, o, mla_w_o[j].astype(BF16), row(ln1_g[l]), row(ln1_b[l]))
        x = peer_layer(x, l)

    return (x[:TP].reshape(B, S, D_MODEL), x[TP:].reshape(DB, DS, D_MODEL), jnp.stack(gm_rows, axis=0),
            c_new[:TP].reshape(B, S, KV_LORA), kpe_new[:TP].reshape(B, S, QK_ROPE),
            c_new[TP:].reshape(DB, DS, KV_LORA), kpe_new[TP:].reshape(DB, DS, QK_ROPE))
```
